```python
import math
import jax, jax.numpy as jnp
from jax import lax
import numpy as np

D_MODEL = 1024
BATCH = 8
SEQ = 8192
DEPTH = 2
DEC_BATCH = 2
DEC_SEQ = 8192
PAST_LEN = 128

BRANCH_W = D_MODEL // 2
N_BRANCH = 4
CONV_W = 4
NORM_EPS = 1e-6

LRU_W = BRANCH_W
LRU_BLOCKS = 4
LRU_BLOCK_W = LRU_W // LRU_BLOCKS
LRU_C = 8.0

GLA_HEADS = 4
GLA_DK = BRANCH_W // 2 // GLA_HEADS
GLA_DV = BRANCH_W // GLA_HEADS
GLA_RANK = 16
GLA_TAU = 16.0
GLA_CHUNK = 64

DN_HEADS = 4
DN_DK = BRANCH_W // DN_HEADS
DN_DV = BRANCH_W // DN_HEADS
DN_CHUNK = 64

S5_GROUP_W = 16
S5_GROUPS = BRANCH_W // S5_GROUP_W
S5_STATE = 64

IN_WIDTHS = (
    LRU_W, LRU_W,
    GLA_HEADS * GLA_DK, GLA_HEADS * GLA_DK, GLA_HEADS * GLA_DV, BRANCH_W, 2 * GLA_RANK,
    3 * BRANCH_W, BRANCH_W, 4 * DN_HEADS,
    BRANCH_W, BRANCH_W,
)
D_IN = sum(IN_WIDTHS)

kernel_name = 'hybrid_bidir_rglru_gla_gdn_s5'


def rms_norm(x, g):
    xf = x.astype(jnp.float32)
    y = xf * lax.rsqrt(jnp.mean(xf * xf, axis=-1, keepdims=True) + NORM_EPS)
    return (y * g.astype(jnp.float32)).astype(x.dtype)


def l2norm(x):
    return x * lax.rsqrt(jnp.sum(x * x, axis=-1, keepdims=True) + NORM_EPS)


def _flip(t):
    return jnp.flip(t, axis=1)


def centred_dwconv(x, w):
    left = CONV_W // 2
    return lax.conv_general_dilated(
        x, w[:, None, :], window_strides=(1,), padding=[(left, CONV_W - 1 - left)],
        dimension_numbers=('NWC', 'WIO', 'NWC'), feature_group_count=x.shape[-1])


def _linear_combine(left, right):
    a1, b1 = left
    a2, b2 = right
    return a1 * a2, a2 * b1 + b2


def _complex_combine(left, right):
    ar1, ai1, br1, bi1 = left
    ar2, ai2, br2, bi2 = right
    return (ar2 * ar1 - ai2 * ai1, ar2 * ai1 + ai2 * ar1,
            ar2 * br1 - ai2 * bi1 + br2, ar2 * bi1 + ai2 * br1 + bi2)


def rglru_branch(u, gate, conv_w, conv_b, w_a, b_a, w_x, b_x, lam):
    f32 = jnp.float32
    bsz, L, _ = u.shape
    xc = centred_dwconv(u.astype(f32), conv_w.astype(f32)) + conv_b.astype(f32)
    xb = xc.reshape(bsz, L, LRU_BLOCKS, LRU_BLOCK_W)
    hs = []
    for d, rev in enumerate((False, True)):
        r = jax.nn.sigmoid(jnp.einsum('blhi,hij->blhj', xb, w_a[d].astype(f32)).reshape(bsz, L, LRU_W) + b_a[d].astype(f32))
        i = jax.nn.sigmoid(jnp.einsum('blhi,hij->blhj', xb, w_x[d].astype(f32)).reshape(bsz, L, LRU_W) + b_x[d].astype(f32))
        log_a = -LRU_C * jax.nn.softplus(-lam[d].astype(f32)) * r
        drive = jnp.sqrt(-jnp.expm1(2.0 * log_a)) * (i * xc)
        _, h = lax.associative_scan(_linear_combine, (jnp.exp(log_a), drive), axis=1, reverse=rev)
        hs.append(h)
    return ((hs[0] + hs[1]) * jax.nn.silu(gate.astype(f32))).astype(u.dtype)


def gla_chunked(q, k, v, log_a):
    bsz, L, H, DK = q.shape
    DV = v.shape[-1]
    C = GLA_CHUNK
    N = L // C
    rs = lambda t: t.reshape(bsz, N, C, H, t.shape[-1])
    q, k, v, log_a = rs(q), rs(k), rs(v), rs(log_a)
    b = jnp.cumsum(log_a, axis=2)
    b_end = b[:, :, -1:]
    q_dec = q * jnp.exp(b)
    k_inv = k * jnp.exp(-b)
    k_end = k * jnp.exp(b_end - b)
    causal = jnp.tril(jnp.ones((C, C), bool))
    scores = jnp.where(causal, jnp.einsum('bnthk,bnshk->bnhts', q_dec, k_inv), 0.0)
    o_intra = jnp.einsum('bnhts,bnshv->bnthv', scores, v)

    def step(S, inp):
        qd, ke, vv, de = inp
        o = jnp.einsum('bthk,bhkv->bthv', qd, S)
        S = de[..., None] * S + jnp.einsum('bshk,bshv->bhkv', ke, vv)
        return S, o

    s0 = jnp.zeros((bsz, H, DK, DV), jnp.float32)
    xs = (jnp.moveaxis(q_dec, 1, 0), jnp.moveaxis(k_end, 1, 0), jnp.moveaxis(v, 1, 0),
          jnp.moveaxis(jnp.exp(b_end[:, :, 0]), 1, 0))
    _, o_inter = lax.scan(step, s0, xs)
    return (o_intra + jnp.moveaxis(o_inter, 0, 1)).reshape(bsz, L, H, DV)


def gla_branch(q, k, v, gate, lr, w_up, b_up, norm_g):
    f32 = jnp.float32
    bsz, L, _ = q.shape
    q = q.astype(f32).reshape(bsz, L, GLA_HEADS, GLA_DK) * GLA_DK ** -0.5
    k = k.astype(f32).reshape(bsz, L, GLA_HEADS, GLA_DK)
    v = v.astype(f32).reshape(bsz, L, GLA_HEADS, GLA_DV)
    lr = lr.astype(f32).reshape(bsz, L, 2, GLA_RANK)
    log_a = [
        (jax.nn.log_sigmoid(jnp.einsum('blr,rk->blk', lr[:, :, d], w_up[d].astype(f32)) + b_up[d].astype(f32))
         / GLA_TAU).reshape(bsz, L, GLA_HEADS, GLA_DK)
        for d in range(2)]
    o = gla_chunked(q, k, v, log_a[0]) + _flip(gla_chunked(_flip(q), _flip(k), _flip(v), _flip(log_a[1])))
    o = rms_norm(o, norm_g).reshape(bsz, L, BRANCH_W)
    return (o * jax.nn.silu(gate.astype(f32))).astype(gate.dtype)


def gated_delta_chunked(q, k, v, beta, g):
    bsz, L, H, DK = q.shape
    DV = v.shape[-1]
    C = DN_CHUNK
    N = L // C
    chunk4 = lambda t: t.reshape(bsz, N, C, H, t.shape[-1]).transpose(0, 1, 3, 2, 4)
    chunk3 = lambda t: t.reshape(bsz, N, C, H).transpose(0, 1, 3, 2)
    q, k, v = chunk4(q), chunk4(k), chunk4(v)
    beta = chunk3(beta)
    G = jnp.cumsum(chunk3(g), axis=-1)
    causal = jnp.tril(jnp.ones((C, C), bool))
    strict = jnp.tril(jnp.ones((C, C), bool), -1)
    diff = G[..., :, None] - G[..., None, :]
    gamma = jnp.where(causal, jnp.exp(jnp.where(causal, diff, 0.0)), 0.0)
    k_beta = k * beta[..., None]
    a_mat = jnp.where(strict, jnp.einsum('bnhtk,bnhsk->bnhts', k_beta, k) * gamma, 0.0)
    eye = jnp.eye(C, dtype=jnp.float32)
    t_mat = lax.linalg.triangular_solve(a_mat + eye, jnp.broadcast_to(eye, a_mat.shape),
                                        left_side=True, lower=True, unit_diagonal=True)
    w = jnp.einsum('bnhts,bnhsk->bnhtk', t_mat, k_beta * jnp.exp(G)[..., None])
    u = jnp.einsum('bnhts,bnhsv->bnhtv', t_mat, v * beta[..., None])
    attn = jnp.where(causal, jnp.einsum('bnhtk,bnhsk->bnhts', q, k) * gamma, 0.0)
    q_dec = q * jnp.exp(G)[..., None]
    k_end = k * jnp.exp(G[..., -1:] - G)[..., None]
    dec_end = jnp.exp(G[..., -1])

    def step(S, inp):
        qd, ke, ww, uu, at, de = inp
        v_new = uu - jnp.einsum('bhtk,bhkv->bhtv', ww, S)
        o = jnp.einsum('bhtk,bhkv->bhtv', qd, S) + jnp.einsum('bhts,bhsv->bhtv', at, v_new)
        S = de[..., None, None] * S + jnp.einsum('bhtk,bhtv->bhkv', ke, v_new)
        return S, o

    s0 = jnp.zeros((bsz, H, DK, DV), jnp.float32)
    xs = tuple(jnp.moveaxis(t, 1, 0) for t in (q_dec, k_end, w, u, attn, dec_end))
    _, o = lax.scan(step, s0, xs)
    return o.transpose(1, 0, 3, 2, 4).reshape(bsz, L, H, DV)


def deltanet_branch(qkv, gate, ba, conv_w, a_log, dt_bias, norm_g):
    f32 = jnp.float32
    bsz, L, _ = qkv.shape
    qkv = jax.nn.silu(centred_dwconv(qkv.astype(f32), conv_w.astype(f32)))
    q, k, v = jnp.split(qkv, 3, axis=-1)
    q = l2norm(q.reshape(bsz, L, DN_HEADS, DN_DK)) * DN_DK ** -0.5
    k = l2norm(k.reshape(bsz, L, DN_HEADS, DN_DK))
    v = v.reshape(bsz, L, DN_HEADS, DN_DV)
    ba = ba.astype(f32).reshape(bsz, L, 2, 2, DN_HEADS)
    beta = jax.nn.sigmoid(ba[:, :, :, 0])
    g = -jnp.exp(a_log.astype(f32)) * jax.nn.softplus(ba[:, :, :, 1] + dt_bias.astype(f32))
    o = (gated_delta_chunked(q, k, v, beta[:, :, 0], g[:, :, 0])
         + _flip(gated_delta_chunked(_flip(q), _flip(k), _flip(v), _flip(beta[:, :, 1]), _flip(g[:, :, 1]))))
    o = rms_norm(o, norm_g).reshape(bsz, L, BRANCH_W)
    return (o * jax.nn.silu(gate.astype(f32))).astype(gate.dtype)


def _s5_discretise(lam_re, lam_im, log_dt, b_re, b_im):
    dt = jnp.exp(log_dt)[:, None]
    mag = jnp.exp(lam_re * dt)
    a_re = mag * jnp.cos(lam_im * dt)
    a_im = mag * jnp.sin(lam_im * dt)
    den = lam_re * lam_re + lam_im * lam_im
    n_re = a_re - 1.0
    f_re = (n_re * lam_re + a_im * lam_im) / den
    f_im = (a_im * lam_re - n_re * lam_im) / den
    bb_re = f_re[..., None] * b_re - f_im[..., None] * b_im
    bb_im = f_re[..., None] * b_im + f_im[..., None] * b_re
    return a_re, a_im, bb_re, bb_im


def s5_branch(u, gate, lam_re, lam_im, log_dt, b_re, b_im, c_re, c_im, d_skip, w_glu, b_glu):
    f32 = jnp.float32
    bsz, L, _ = u.shape
    uf = u.astype(f32)
    disc = [_s5_discretise(lam_re[d].astype(f32), lam_im[d].astype(f32), log_dt[d].astype(f32),
                           b_re[d].astype(f32), b_im[d].astype(f32)) for d in range(2)]
    cs = [(c_re[d].astype(f32), c_im[d].astype(f32)) for d in range(2)]

    def one_sequence(us):
        ys = []
        for d, rev in enumerate((False, True)):
            a_re, a_im, bb_re, bb_im = disc[d]
            bu_re = jnp.einsum('lgi,gpi->lgp', us, bb_re)
            bu_im = jnp.einsum('lgi,gpi->lgp', us, bb_im)
            _, _, s_re, s_im = lax.associative_scan(
                _complex_combine,
                (jnp.broadcast_to(a_re, bu_re.shape), jnp.broadcast_to(a_im, bu_re.shape), bu_re, bu_im),
                axis=0, reverse=rev)
            ys.append(jnp.einsum('lgp,gip->lgi', s_re, cs[d][0]) - jnp.einsum('lgp,gip->lgi', s_im, cs[d][1]))
        return ys[0] + ys[1]

    y = lax.map(one_sequence, uf.reshape(bsz, L, S5_GROUPS, S5_GROUP_W)).reshape(bsz, L, BRANCH_W)
    y = y + d_skip.astype(f32) * uf
    z = jax.nn.gelu(y)
    z = z * jax.nn.sigmoid(z @ w_glu.astype(f32) + b_glu.astype(f32))
    return (z * jax.nn.silu(gate.astype(f32))).astype(u.dtype)


def trunk_layer(x, norm_g, w_in, lru_conv_w, lru_conv_b, lru_w_a, lru_b_a, lru_w_x, lru_b_x, lru_lambda,
                gla_w_up, gla_b_up, gla_norm_g, dn_conv_w, dn_a_log, dn_dt_bias, dn_norm_g,
                s5_lambda_re, s5_lambda_im, s5_log_dt, s5_b_re, s5_b_im, s5_c_re, s5_c_im, s5_d,
                s5_w_glu, s5_b_glu, w_branch, w_merge_gate, b_merge_gate, w_out):
    xn = rms_norm(x, norm_g)
    proj = jnp.einsum('bld,de->ble', xn, w_in)
    split_at = np.cumsum(IN_WIDTHS)[:-1].tolist()
    (lru_x, lru_gate, gla_q, gla_k, gla_v, gla_gate, gla_lr,
     dn_qkv, dn_gate, dn_ba, s5_u, s5_gate) = jnp.split(proj, split_at, axis=-1)
    branches = (
        rglru_branch(lru_x, lru_gate, lru_conv_w, lru_conv_b, lru_w_a, lru_b_a, lru_w_x, lru_b_x, lru_lambda),
        gla_branch(gla_q, gla_k, gla_v, gla_gate, gla_lr, gla_w_up, gla_b_up, gla_norm_g),
        deltanet_branch(dn_qkv, dn_gate, dn_ba, dn_conv_w, dn_a_log, dn_dt_bias, dn_norm_g),
        s5_branch(s5_u, s5_gate, s5_lambda_re, s5_lambda_im, s5_log_dt, s5_b_re, s5_b_im,
                  s5_c_re, s5_c_im, s5_d, s5_w_glu, s5_b_glu),
    )
    merged = None
    for n, y in enumerate(branches):
        gate = jax.nn.sigmoid(xn @ w_merge_gate[n] + b_merge_gate[n])
        term = gate * (y @ w_branch[n])
        merged = term if merged is None else merged + term
    return x + merged @ w_out


def setup_inputs(seed: int = 0) -> dict:
    key = jax.random.key(seed)
    ks = iter(jax.random.split(key, 48))
    f32 = jnp.float32

    def nrm(shape, scale):
        return scale * jax.random.normal(next(ks), shape, f32)

    def unif(shape, lo, hi):
        return jax.random.uniform(next(ks), shape, f32, lo, hi)

    L = DEPTH
    x_prompt = nrm((BATCH, SEQ, D_MODEL), 1.0)
    x_sample = nrm((DEC_BATCH, DEC_SEQ, D_MODEL), 1.0)
    norm_g = 1.0 + nrm((L, D_MODEL), 0.02)
    w_in = nrm((L, D_MODEL, D_IN), D_MODEL ** -0.5)
    lru_conv_w = nrm((L, CONV_W, LRU_W), CONV_W ** -0.5)
    lru_conv_b = nrm((L, LRU_W), 0.01)
    lru_w_a = nrm((L, 2, LRU_BLOCKS, LRU_BLOCK_W, LRU_BLOCK_W), LRU_BLOCK_W ** -0.5)
    lru_b_a = nrm((L, 2, LRU_W), 0.01)
    lru_w_x = nrm((L, 2, LRU_BLOCKS, LRU_BLOCK_W, LRU_BLOCK_W), LRU_BLOCK_W ** -0.5)
    lru_b_x = nrm((L, 2, LRU_W), 0.01)
    a0 = unif((L, 2, LRU_W), 0.9, 0.999)
    s = a0 ** (1.0 / LRU_C)
    lru_lambda = jnp.log(s) - jnp.log1p(-s)
    gla_w_up = nrm((L, 2, GLA_RANK, GLA_HEADS * GLA_DK), GLA_RANK ** -0.5)
    gla_b_up = nrm((L, 2, GLA_HEADS * GLA_DK), 0.01)
    gla_norm_g = 1.0 + nrm((L, GLA_DV), 0.02)
    dn_conv_w = nrm((L, CONV_W, 3 * BRANCH_W), CONV_W ** -0.5)
    dn_a_log = jnp.log(unif((L, 2, DN_HEADS), 1.0, 16.0))
    dt = jnp.exp(unif((L, 2, DN_HEADS), math.log(1e-3), math.log(1e-1)))
    dn_dt_bias = dt + jnp.log(-jnp.expm1(-dt))
    dn_norm_g = 1.0 + nrm((L, DN_DV), 0.02)
    s5_shape = (L, 2, S5_GROUPS, S5_STATE)
    s5_lambda_re = -0.5 + nrm(s5_shape, 0.01)
    s5_lambda_im = jnp.pi * jnp.arange(S5_STATE, dtype=f32) + nrm(s5_shape, 0.01)
    s5_log_dt = unif((L, 2, S5_GROUPS), math.log(1e-3), math.log(1e-1))
    b_scale = (2.0 * S5_GROUP_W) ** -0.5
    s5_b_re = nrm((L, 2, S5_GROUPS, S5_STATE, S5_GROUP_W), b_scale)
    s5_b_im = nrm((L, 2, S5_GROUPS, S5_STATE, S5_GROUP_W), b_scale)
    c_scale = S5_STATE ** -0.5
    s5_c_re = nrm((L, 2, S5_GROUPS, S5_GROUP_W, S5_STATE), c_scale)
    s5_c_im = nrm((L, 2, S5_GROUPS, S5_GROUP_W, S5_STATE), c_scale)
    s5_d = nrm((L, BRANCH_W), 1.0)
    s5_w_glu = nrm((L, BRANCH_W, BRANCH_W), BRANCH_W ** -0.5)
    s5_b_glu = nrm((L, BRANCH_W), 0.01)
    w_branch = nrm((L, N_BRANCH, BRANCH_W, D_MODEL), BRANCH_W ** -0.5)
    w_merge_gate = nrm((L, N_BRANCH, D_MODEL, D_MODEL), D_MODEL ** -0.5)
    b_merge_gate = nrm((L, N_BRANCH, D_MODEL), 0.01)
    w_out = nrm((L, D_MODEL, D_MODEL), D_MODEL ** -0.5)
    final_norm_g = 1.0 + nrm((D_MODEL,), 0.02)
    return {
        'x_prompt': x_prompt, 'x_sample': x_sample,
        'norm_g': norm_g, 'w_in': w_in,
        'lru_conv_w': lru_conv_w, 'lru_conv_b': lru_conv_b, 'lru_w_a': lru_w_a, 'lru_b_a': lru_b_a,
        'lru_w_x': lru_w_x, 'lru_b_x': lru_b_x, 'lru_lambda': lru_lambda,
        'gla_w_up': gla_w_up, 'gla_b_up': gla_b_up, 'gla_norm_g': gla_norm_g,
        'dn_conv_w': dn_conv_w, 'dn_a_log': dn_a_log, 'dn_dt_bias': dn_dt_bias, 'dn_norm_g': dn_norm_g,
        's5_lambda_re': s5_lambda_re, 's5_lambda_im': s5_lambda_im, 's5_log_dt': s5_log_dt,
        's5_b_re': s5_b_re, 's5_b_im': s5_b_im, 's5_c_re': s5_c_re, 's5_c_im': s5_c_im,
        's5_d': s5_d, 's5_w_glu': s5_w_glu, 's5_b_glu': s5_b_glu,
        'w_branch': w_branch, 'w_merge_gate': w_merge_gate, 'b_merge_gate': b_merge_gate,
        'w_out': w_out, 'final_norm_g': final_norm_g,
    }


def reference(x_prompt, x_sample, norm_g, w_in, lru_conv_w, lru_conv_b, lru_w_a, lru_b_a, lru_w_x,
              lru_b_x, lru_lambda, gla_w_up, gla_b_up, gla_norm_g, dn_conv_w, dn_a_log, dn_dt_bias,
              dn_norm_g, s5_lambda_re, s5_lambda_im, s5_log_dt, s5_b_re, s5_b_im, s5_c_re, s5_c_im,
              s5_d, s5_w_glu, s5_b_glu, w_branch, w_merge_gate, b_merge_gate, w_out, final_norm_g):
    layer_params = (norm_g, w_in, lru_conv_w, lru_conv_b, lru_w_a, lru_b_a, lru_w_x, lru_b_x, lru_lambda,
                    gla_w_up, gla_b_up, gla_norm_g, dn_conv_w, dn_a_log, dn_dt_bias, dn_norm_g,
                    s5_lambda_re, s5_lambda_im, s5_log_dt, s5_b_re, s5_b_im, s5_c_re, s5_c_im, s5_d,
                    s5_w_glu, s5_b_glu, w_branch, w_merge_gate, b_merge_gate, w_out)

    def trunk(x):
        for l in range(DEPTH):
            x = trunk_layer(x, *[p[l] for p in layer_params])
        return rms_norm(x, final_norm_g)

    y_prompt = trunk(x_prompt)
    y_sample = trunk(x_sample)
    return (y_prompt, y_sample)
```

```python
import functools
import math

import jax
import jax.numpy as jnp
from jax import lax
from jax.experimental import pallas as pl
from jax.experimental.pallas import tpu as pltpu

F32 = jnp.float32
MXU_DTYPE = jnp.bfloat16
HI = lax.Precision.HIGHEST

NORM_EPS = 1e-6
CONV_W = 4
CONV_HALO = 8
LRU_C = 8.0
GLA_TAU = 16.0
GLA_HEADS = 4
CHUNK = 64
LANES = 128
SUBLANES = 8
SEG_PAD = 4
VMEM_LIMIT = 56 * 1024 * 1024

TM_PROJ = 256
TL_MIX = 256
SOLVE_LANES = 256


def _mm(a, b):
    return jnp.dot(a.astype(MXU_DTYPE), b.astype(MXU_DTYPE), preferred_element_type=F32)


def _mm_nt(a, b):
    return lax.dot_general(a.astype(MXU_DTYPE), b.astype(MXU_DTYPE), (((1,), (1,)), ((), ())),
                           preferred_element_type=F32)


def _mm_tn(a, b):
    return lax.dot_general(a.astype(MXU_DTYPE), b.astype(MXU_DTYPE), (((0,), (0,)), ((), ())),
                           preferred_element_type=F32)


def _mm_hi(a, b):
    return jnp.dot(a, b, precision=HI, preferred_element_type=F32)


def _mm_nt_hi(a, b):
    return lax.dot_general(a, b, (((1,), (1,)), ((), ())), precision=HI, preferred_element_type=F32)


def _mm_tn_hi(a, b):
    return lax.dot_general(a, b, (((0,), (0,)), ((), ())), precision=HI, preferred_element_type=F32)


def _sigmoid(x):
    return 1.0 / (1.0 + jnp.exp(-x))


def _silu(x):
    return x * _sigmoid(x)


def _softplus(x):
    return jnp.maximum(x, 0.0) + jnp.log1p(jnp.exp(-jnp.abs(x)))


def _rms(x, g):
    return x * lax.rsqrt(jnp.mean(x * x, axis=-1, keepdims=True) + NORM_EPS) * g


def _params(*sem):
    return pltpu.CompilerParams(dimension_semantics=sem, vmem_limit_bytes=VMEM_LIMIT)


def _const_spec(shape):
    nd = len(shape)
    return pl.BlockSpec(shape, lambda *_: (0,) * nd)


def _seq_pos(rev):
    if rev:
        return lambda i, n: n - 1 - i
    return lambda i, n: i


def _tri_mask(rev, strict):
    t = lax.broadcasted_iota(jnp.int32, (CHUNK, CHUNK), 0)
    s = lax.broadcasted_iota(jnp.int32, (CHUNK, CHUNK), 1)
    if rev:
        return (s > t) if strict else (s >= t)
    return (s < t) if strict else (s <= t)


def _in_proj_kernel(widths, x_ref, g_ref, w_ref, *o_refs):
    xn = _rms(x_ref[...], g_ref[...]).astype(MXU_DTYPE)
    off = 0
    for o_ref, wd in zip(o_refs, widths):
        o_ref[...] = jnp.dot(xn, w_ref[:, off:off + wd], preferred_element_type=F32)
        off += wd


def _in_proj(x2, g, w_cat, widths):
    t, d = x2.shape
    n = w_cat.shape[1]
    return pl.pallas_call(
        functools.partial(_in_proj_kernel, widths),
        grid=(t // TM_PROJ,),
        in_specs=[pl.BlockSpec((TM_PROJ, d), lambda i: (i, 0)), _const_spec((1, d)), _const_spec((d, n))],
        out_specs=[pl.BlockSpec((TM_PROJ, wd), lambda i: (i, 0)) for wd in widths],
        out_shape=[jax.ShapeDtypeStruct((t, wd), F32) for wd in widths],
        compiler_params=_params("parallel"),
        name="in_proj",
    )(x2, g, w_cat)


def _out_proj_kernel(final, x_ref, y0, y1, y2, y3, g_ref, wmg_ref, bmg_ref, wbr_ref, wout_ref, fg_ref, o_ref):
    x = x_ref[...]
    xn = _rms(x, g_ref[...]).astype(MXU_DTYPE)
    merged = None
    for n, y_ref in enumerate((y0, y1, y2, y3)):
        gate = _sigmoid(jnp.dot(xn, wmg_ref[n], preferred_element_type=F32) + bmg_ref[n])
        term = gate * _mm(y_ref[...], wbr_ref[n])
        merged = term if merged is None else merged + term
    out = x + _mm(merged, wout_ref[...])
    if final:
        out = _rms(out, fg_ref[...])
    o_ref[...] = out


def _out_proj(x2, ys, g, wmg, bmg, wbr, wout, fg, final):
    t, d = x2.shape
    bw = ys[0].shape[1]
    row = lambda wd: pl.BlockSpec((TM_PROJ, wd), lambda i: (i, 0))
    return pl.pallas_call(
        functools.partial(_out_proj_kernel, final),
        grid=(t // TM_PROJ,),
        in_specs=[row(d)] + [row(bw)] * 4 + [_const_spec((1, d)), _const_spec(wmg.shape), _const_spec(bmg.shape),
                                             _const_spec(wbr.shape), _const_spec(wout.shape), _const_spec((1, d))],
        out_specs=row(d),
        out_shape=jax.ShapeDtypeStruct((t, d), F32),
        compiler_params=_params("parallel"),
        name="out_proj",
    )(x2, *ys, g, wmg, bmg, wbr, wout, fg)


def _halo_specs(rev, tl, width, seq_len):
    pos = _seq_pos(rev)
    per = tl // CONV_HALO
    last = seq_len // CONV_HALO - 1
    nblk = seq_len // tl
    prev = pl.BlockSpec((None, CONV_HALO, width),
                        lambda b, i: (b, jnp.maximum(pos(i, nblk) * per - 1, 0), 0))
    nxt = pl.BlockSpec((None, CONV_HALO, width),
                       lambda b, i: (b, jnp.minimum((pos(i, nblk) + 1) * per, last), 0))
    return prev, nxt


def _centred_conv(ext_ref, u, prev, nxt, at_start, at_end, w_ref):
    tl = u.shape[0]
    ext_ref[0:CONV_HALO, :] = jnp.where(at_start, 0.0, prev)
    ext_ref[CONV_HALO:CONV_HALO + tl, :] = u
    ext_ref[CONV_HALO + tl:, :] = jnp.where(at_end, 0.0, nxt)
    left = CONV_W // 2
    acc = None
    for j in range(CONV_W):
        term = ext_ref[pl.ds(CONV_HALO + j - left, tl), :] * w_ref[j:j + 1, :]
        acc = term if acc is None else acc + term
    return acc


def _lru_kernel(rev, final, tl, *refs):
    if final:
        (main_ref, prev_ref, next_ref, hf_ref, cw_ref, cb_ref, wg_ref, bg_ref, lam_ref,
         o_ref, carry_ref, ext_ref, a_scr, d_scr) = refs
    else:
        (main_ref, prev_ref, next_ref, cw_ref, cb_ref, wg_ref, bg_ref, lam_ref,
         o_ref, carry_ref, ext_ref, a_scr, d_scr) = refs
    i = pl.program_id(1)
    nblk = pl.num_programs(1)
    blk = _seq_pos(rev)(i, nblk)
    width = o_ref.shape[1]
    nslab = width // LANES
    seg = tl // SUBLANES
    pitch = seg + SEG_PAD

    @pl.when(i == 0)
    def _():
        carry_ref[...] = jnp.zeros_like(carry_ref)

    u = main_ref[:, 0:width]
    xc = _centred_conv(ext_ref, u, prev_ref[...], next_ref[...], blk == 0, blk == nblk - 1, cw_ref) + cb_ref[...]
    c_all = -LRU_C * _softplus(-lam_ref[...])
    for h in range(nslab):
        sl = slice(h * LANES, (h + 1) * LANES)
        xh = xc[:, sl]
        pre = _mm(xh, wg_ref[h]) + bg_ref[h]
        r = _sigmoid(pre[:, 0:LANES])
        ig = _sigmoid(pre[:, LANES:])
        a = jnp.exp(c_all[:, sl] * r)
        drive = jnp.sqrt(1.0 - a * a) * (ig * xh)
        for s in range(SUBLANES):
            a_scr[h, s * pitch:s * pitch + seg, :] = a[s * seg:(s + 1) * seg, :]
            d_scr[h, s * pitch:s * pitch + seg, :] = drive[s * seg:(s + 1) * seg, :]

    order = range(seg - 1, -1, -1) if rev else range(seg)
    for h in range(nslab):
        hloc = jnp.zeros((SUBLANES, LANES), F32)
        prod = jnp.ones((SUBLANES, LANES), F32)
        for k in order:
            ak = a_scr[h, pl.ds(k, SUBLANES, stride=pitch), :]
            dk = d_scr[h, pl.ds(k, SUBLANES, stride=pitch), :]
            hloc = ak * hloc + dk
            prod = prod * ak
            d_scr[h, pl.ds(k, SUBLANES, stride=pitch), :] = hloc
            a_scr[h, pl.ds(k, SUBLANES, stride=pitch), :] = prod

    last = 0 if rev else seg - 1
    seg_order = range(SUBLANES - 1, -1, -1) if rev else range(SUBLANES)
    for h in range(nslab):
        sl = slice(h * LANES, (h + 1) * LANES)
        c = carry_ref[0:1, sl]
        for s in seg_order:
            rows = slice(s * seg, (s + 1) * seg)
            hs = d_scr[h, s * pitch:s * pitch + seg, :] + a_scr[h, s * pitch:s * pitch + seg, :] * c
            if final:
                gate = main_ref[rows, width + h * LANES:width + (h + 1) * LANES]
                o_ref[rows, sl] = (hf_ref[rows, sl] + hs) * _silu(gate)
            else:
                o_ref[rows, sl] = hs
            c = a_scr[h, pl.ds(s * pitch + last, 1), :] * c + d_scr[h, pl.ds(s * pitch + last, 1), :]
        carry_ref[0:1, sl] = c


def _lru_dir(rev, proj, hf, cw, cb, wg, bg, lam):
    nb, seq_len, two_w = proj.shape
    width = two_w // 2
    tl = TL_MIX
    nblk = seq_len // tl
    pos = _seq_pos(rev)
    blk_spec = lambda wd: pl.BlockSpec((None, tl, wd), lambda b, i: (b, pos(i, nblk), 0))
    prev, nxt = _halo_specs(rev, tl, width, seq_len)
    final = hf is not None
    pitch = tl // SUBLANES + SEG_PAD
    ins = [proj, proj, proj] + ([hf] if final else []) + [cw, cb, wg, bg, lam]
    specs = [blk_spec(two_w), prev, nxt] + ([blk_spec(width)] if final else []) + [
        _const_spec(cw.shape), _const_spec(cb.shape), _const_spec(wg.shape), _const_spec(bg.shape),
        _const_spec(lam.shape)]
    return pl.pallas_call(
        functools.partial(_lru_kernel, rev, final, tl),
        grid=(nb, nblk),
        in_specs=specs,
        out_specs=blk_spec(width),
        out_shape=jax.ShapeDtypeStruct((nb, seq_len, width), F32),
        scratch_shapes=[pltpu.VMEM((SUBLANES, width), F32),
                        pltpu.VMEM((tl + 2 * CONV_HALO, width), F32),
                        pltpu.VMEM((width // LANES, SUBLANES * pitch, LANES), F32),
                        pltpu.VMEM((width // LANES, SUBLANES * pitch, LANES), F32)],
        compiler_params=_params("parallel", "arbitrary"),
        name="lru_bwd" if rev else "lru_fwd",
    )(*ins)


def _gla_kernel(rev, final, tl, heads, dk, dv, *refs):
    if final:
        (main_ref, small_ref, of_ref, wup_ref, bup_ref, ng_ref, o_ref, s_ref, la_scr) = refs
    else:
        (main_ref, small_ref, wup_ref, bup_ref, o_ref, s_ref, la_scr) = refs
    i = pl.program_id(1)
    hk = heads * dk
    hv = heads * dv

    @pl.when(i == 0)
    def _():
        s_ref[...] = jnp.zeros_like(s_ref)

    pre = _mm(small_ref[...], wup_ref[...]) + bup_ref[...]
    la_scr[...] = (jnp.minimum(pre, 0.0) - jnp.log1p(jnp.exp(-jnp.abs(pre)))) * (1.0 / GLA_TAU)
    incl = _tri_mask(rev, strict=False)
    cum = jnp.where(incl, 1.0, 0.0)
    ones_cols = jnp.ones((CHUNK, dv), F32)
    end = 0 if rev else CHUNK - 1
    nchunk = tl // CHUNK

    def chunk(ci, carry):
        c = (nchunk - 1 - ci) if rev else ci
        r0 = pl.multiple_of(c * CHUNK, CHUNK)
        rows = pl.ds(r0, CHUNK)
        la = la_scr[rows, :]
        b = _mm_hi(cum, la)
        b_end = b[end:end + 1, :]
        dec_cols = jnp.exp(_mm_tn_hi(la, ones_cols))
        q = main_ref[rows, 0:hk] * (dk ** -0.5)
        k = main_ref[rows, hk:2 * hk]
        q_dec = q * jnp.exp(b)
        k_inv = k * jnp.exp(-b)
        k_end = k * jnp.exp(b_end - b)
        outs = []
        for h in range(heads):
            ks = slice(h * dk, (h + 1) * dk)
            v = main_ref[rows, 2 * hk + h * dv:2 * hk + (h + 1) * dv]
            scores = jnp.where(incl, _mm_nt(q_dec[:, ks], k_inv[:, ks]), 0.0)
            s_old = s_ref[h]
            outs.append(_mm(scores, v) + _mm(q_dec[:, ks], s_old))
            s_ref[h] = dec_cols[h * dk:(h + 1) * dk, :] * s_old + _mm_tn(k_end[:, ks], v)
        o = jnp.concatenate(outs, axis=1)
        if final:
            o = o + of_ref[rows, :]
            gate = main_ref[rows, 2 * hk + hv:2 * hk + 2 * hv]
            normed = [_rms(o[:, h * dv:(h + 1) * dv], ng_ref[...]) for h in range(heads)]
            o = jnp.concatenate(normed, axis=1) * _silu(gate)
        o_ref[rows, :] = o
        return carry

    lax.fori_loop(0, nchunk, chunk, 0)


def _gla_dir(rev, proj, small, of, wup, bup, ng, heads, dk, dv):
    nb, seq_len, pw = proj.shape
    tl = TL_MIX
    nblk = seq_len // tl
    pos = _seq_pos(rev)
    blk_spec = lambda wd: pl.BlockSpec((None, tl, wd), lambda b, i: (b, pos(i, nblk), 0))
    final = of is not None
    hv = heads * dv
    ins = [proj, small] + ([of] if final else []) + [wup, bup] + ([ng] if final else [])
    specs = [blk_spec(pw), blk_spec(small.shape[2])] + ([blk_spec(hv)] if final else []) + [
        _const_spec(wup.shape), _const_spec(bup.shape)] + ([_const_spec(ng.shape)] if final else [])
    return pl.pallas_call(
        functools.partial(_gla_kernel, rev, final, tl, heads, dk, dv),
        grid=(nb, nblk),
        in_specs=specs,
        out_specs=blk_spec(hv),
        out_shape=jax.ShapeDtypeStruct((nb, seq_len, hv), F32),
        scratch_shapes=[pltpu.VMEM((heads, dk, dv), F32), pltpu.VMEM((tl, heads * dk), F32)],
        compiler_params=_params("parallel", "arbitrary"),
        name="gla_bwd" if rev else "gla_fwd",
    )(*ins)


def _dn_gates(small, gc_ref):
    beta = _sigmoid(small)
    g = -jnp.exp(gc_ref[0:1, :]) * _softplus(small + gc_ref[1:2, :])
    return jnp.where(gc_ref[2:3, :] > 0.5, beta, jnp.where(gc_ref[3:4, :] > 0.5, g, 0.0))


def _dn_chunk_terms(rev, bg, g_lane, heads):
    incl = _tri_mask(rev, strict=False)
    gc = _mm_hi(jnp.where(incl, 1.0, 0.0), bg)
    lane = lax.broadcasted_iota(jnp.int32, (CHUNK, LANES), 1)
    g_rows = [_mm_nt_hi(jnp.where(lane == g_lane + h, 1.0, 0.0), gc) for h in range(heads)]
    end = 0 if rev else CHUNK - 1
    return gc, g_rows, gc[end:end + 1, :]


def _dn_prep_kernel(tl, heads, dh, main_ref, prev_ref, next_ref, small_ref, cw_ref, gc_ref,
                    qkv_ref, bg_ref, af_ref, ab_ref, ext_ref):
    i = pl.program_id(1)
    nblk = pl.num_programs(1)
    hw = heads * dh
    x = main_ref[:, 0:3 * hw]
    y = _silu(_centred_conv(ext_ref, x, prev_ref[...], next_ref[...], i == 0, i == nblk - 1, cw_ref))
    for h in range(heads):
        for part, scale in ((0, dh ** -0.5), (1, 1.0)):
            sl = slice(part * hw + h * dh, part * hw + (h + 1) * dh)
            z = y[:, sl]
            qkv_ref[:, sl] = z * lax.rsqrt(jnp.sum(z * z, axis=-1, keepdims=True) + NORM_EPS) * scale
    qkv_ref[:, 2 * hw:] = y[:, 2 * hw:]
    bg_ref[...] = _dn_gates(small_ref[...], gc_ref)
    for rev, a_ref in ((False, af_ref), (True, ab_ref)):
        strict = _tri_mask(rev, strict=True)
        incl = _tri_mask(rev, strict=False)
        base = 2 * heads if rev else 0

        def chunk(c, carry, rev=rev, a_ref=a_ref, strict=strict, incl=incl, base=base):
            rows = pl.ds(pl.multiple_of(c * CHUNK, CHUNK), CHUNK)
            bg = bg_ref[rows, :]
            gc, g_rows, _ = _dn_chunk_terms(rev, bg, base + heads, heads)
            for h in range(heads):
                k = qkv_ref[rows, hw + h * dh:hw + (h + 1) * dh]
                diff = gc[:, base + heads + h:base + heads + h + 1] - g_rows[h]
                gamma = jnp.where(incl, jnp.exp(jnp.where(incl, diff, 0.0)), 0.0)
                a_ref[c, h] = jnp.where(strict, _mm_nt(k * bg[:, base + h:base + h + 1], k) * gamma, 0.0)
            return carry

        lax.fori_loop(0, tl // CHUNK, chunk, 0)


def _dn_prep(proj, small, cw, gate_consts, heads, dh):
    nb, seq_len, pw = proj.shape
    tl = TL_MIX
    nblk = seq_len // tl
    hw = heads * dh
    blk_spec = lambda wd: pl.BlockSpec((None, tl, wd), lambda b, i: (b, i, 0))
    prev, nxt = _halo_specs(False, tl, 3 * hw, seq_len)
    a_spec = pl.BlockSpec((None, tl // CHUNK, heads, CHUNK, CHUNK), lambda b, i: (b, i, 0, 0, 0))
    a_shape = jax.ShapeDtypeStruct((nb, seq_len // CHUNK, heads, CHUNK, CHUNK), F32)
    return pl.pallas_call(
        functools.partial(_dn_prep_kernel, tl, heads, dh),
        grid=(nb, nblk),
        in_specs=[blk_spec(pw), prev, nxt, blk_spec(small.shape[2]), _const_spec(cw.shape),
                  _const_spec(gate_consts.shape)],
        out_specs=[blk_spec(3 * hw), blk_spec(LANES), a_spec, a_spec],
        out_shape=[jax.ShapeDtypeStruct((nb, seq_len, 3 * hw), F32),
                   jax.ShapeDtypeStruct((nb, seq_len, LANES), F32), a_shape, a_shape],
        scratch_shapes=[pltpu.VMEM((tl + 2 * CONV_HALO, 3 * hw), F32)],
        compiler_params=_params("parallel", "parallel"),
        name="dn_prep",
    )(proj, proj, proj, small, cw, gate_consts)


def _dn_solve_kernel(a_ref, t_ref):
    shape = t_ref.shape[1:]
    col = lax.broadcasted_iota(jnp.int32, shape, 0)
    t_ref[0] = jnp.where(col == 0, 1.0, 0.0)

    def row(t, carry):
        def term(s, acc):
            return acc - a_ref[t, pl.ds(s, 1), :] * t_ref[s]
        t_ref[t] = lax.fori_loop(0, t, term, jnp.where(col == t, 1.0, 0.0))
        return carry

    lax.fori_loop(1, CHUNK, row, 0)


def _dn_solve(a_t):
    n = a_t.shape[2]
    spec = pl.BlockSpec((CHUNK, CHUNK, SOLVE_LANES), lambda i: (0, 0, i))
    return pl.pallas_call(
        _dn_solve_kernel,
        grid=(n // SOLVE_LANES,),
        in_specs=[spec],
        out_specs=spec,
        out_shape=jax.ShapeDtypeStruct(a_t.shape, F32),
        compiler_params=_params("parallel"),
        name="dn_solve",
    )(a_t)


def _dn_main_kernel(rev, final, tl, heads, dh, *refs):
    if final:
        (qkv_ref, bg_ref, t_ref, of_ref, gate_ref, ng_ref, o_ref, s_ref) = refs
    else:
        (qkv_ref, bg_ref, t_ref, o_ref, s_ref) = refs
    i = pl.program_id(1)
    hw = heads * dh

    @pl.when(i == 0)
    def _():
        s_ref[...] = jnp.zeros_like(s_ref)

    incl = _tri_mask(rev, strict=False)
    base = 2 * heads if rev else 0
    nchunk = tl // CHUNK

    def chunk(ci, carry):
        c = (nchunk - 1 - ci) if rev else ci
        r0 = pl.multiple_of(c * CHUNK, CHUNK)
        rows = pl.ds(r0, CHUNK)
        bg = bg_ref[rows, :]
        gc, g_rows, g_end = _dn_chunk_terms(rev, bg, base + heads, heads)
        e_g = jnp.exp(gc)
        e_rest = jnp.exp(g_end - gc)
        e_end = jnp.exp(g_end)
        outs = []
        for h in range(heads):
            gl = base + heads + h
            q = qkv_ref[rows, h * dh:(h + 1) * dh]
            k = qkv_ref[rows, hw + h * dh:hw + (h + 1) * dh]
            v = qkv_ref[rows, 2 * hw + h * dh:2 * hw + (h + 1) * dh]
            bh = bg[:, base + h:base + h + 1]
            diff = gc[:, gl:gl + 1] - g_rows[h]
            gamma = jnp.where(incl, jnp.exp(jnp.where(incl, diff, 0.0)), 0.0)
            t_mat = t_ref[c, h]
            w = _mm(t_mat, k * bh * e_g[:, gl:gl + 1])
            u = _mm(t_mat, v * bh)
            attn = jnp.where(incl, _mm_nt(q, k) * gamma, 0.0)
            s_old = s_ref[h]
            v_new = u - _mm(w, s_old)
            outs.append(_mm(q * e_g[:, gl:gl + 1], s_old) + _mm(attn, v_new))
            s_ref[h] = e_end[:, gl:gl + 1] * s_old + _mm_tn(k * e_rest[:, gl:gl + 1], v_new)
        o = jnp.concatenate(outs, axis=1)
        if final:
            o = o + of_ref[rows, :]
            normed = [_rms(o[:, h * dh:(h + 1) * dh], ng_ref[...]) for h in range(heads)]
            o = jnp.concatenate(normed, axis=1) * _silu(gate_ref[rows, :])
        o_ref[rows, :] = o
        return carry

    lax.fori_loop(0, nchunk, chunk, 0)


def _dn_dir(rev, qkv, bg, t_all, of, proj, ng, heads, dh):
    nb, seq_len, _ = qkv.shape
    tl = TL_MIX
    nblk = seq_len // tl
    hw = heads * dh
    pos = _seq_pos(rev)
    blk_spec = lambda wd: pl.BlockSpec((None, tl, wd), lambda b, i: (b, pos(i, nblk), 0))
    t_spec = pl.BlockSpec((None, tl // CHUNK, heads, CHUNK, CHUNK), lambda b, i: (b, pos(i, nblk), 0, 0, 0))
    final = of is not None
    gate_spec = pl.BlockSpec((None, tl, hw), lambda b, i: (b, pos(i, nblk), 3))
    ins = [qkv, bg, t_all] + ([of, proj, ng] if final else [])
    specs = [blk_spec(3 * hw), blk_spec(LANES), t_spec] + (
        [blk_spec(hw), gate_spec, _const_spec(ng.shape)] if final else [])
    return pl.pallas_call(
        functools.partial(_dn_main_kernel, rev, final, tl, heads, dh),
        grid=(nb, nblk),
        in_specs=specs,
        out_specs=blk_spec(hw),
        out_shape=jax.ShapeDtypeStruct((nb, seq_len, hw), F32),
        scratch_shapes=[pltpu.VMEM((heads, dh, dh), F32)],
        compiler_params=_params("parallel", "arbitrary"),
        name="dn_bwd" if rev else "dn_fwd",
    )(*ins)


def _dn_branch(proj, small, cw, gate_consts, ng, heads, dh):
    nb, seq_len, _ = proj.shape
    qkv, bg, a_f, a_b = _dn_prep(proj, small, cw, gate_consts, heads, dh)
    n_inst = nb * (seq_len // CHUNK) * heads
    a_f = a_f.reshape(n_inst, CHUNK, CHUNK).transpose(1, 2, 0)
    a_b = jnp.flip(a_b.reshape(n_inst, CHUNK, CHUNK), axis=(1, 2)).transpose(1, 2, 0)
    t_both = _dn_solve(jnp.concatenate([a_f, a_b], axis=2))
    shape5 = (nb, seq_len // CHUNK, heads, CHUNK, CHUNK)
    t_f = t_both[:, :, :n_inst].transpose(2, 0, 1).reshape(shape5)
    t_b = jnp.flip(t_both[:, :, n_inst:].transpose(2, 0, 1), axis=(1, 2)).reshape(shape5)
    o_f = _dn_dir(False, qkv, bg, t_f, None, None, None, heads, dh)
    return _dn_dir(True, qkv, bg, t_b, o_f, proj, ng, heads, dh)


def _s5_prep_kernel(seg, lre_ref, lim_ref, ldt_ref, bre_ref, bim_ref, a_ref, pw_ref, bbre_ref, bbim_ref):
    lam_re = lre_ref[...]
    lam_im = lim_ref[...]
    dt = jnp.exp(ldt_ref[...])
    mag = jnp.exp(lam_re * dt)
    a_re = mag * jnp.cos(lam_im * dt)
    a_im = mag * jnp.sin(lam_im * dt)
    den = lam_re * lam_re + lam_im * lam_im
    n_re = a_re - 1.0
    f_re = (n_re * lam_re + a_im * lam_im) / den
    f_im = (a_im * lam_re - n_re * lam_im) / den
    a_ref[0] = a_re
    a_ref[1] = a_im
    for i in range(bre_ref.shape[0]):
        bbre_ref[i] = f_re * bre_ref[i] - f_im * bim_ref[i]
        bbim_ref[i] = f_re * bim_ref[i] + f_im * bre_ref[i]
    for k in range(seg):
        m = jnp.exp(lam_re * dt * float(k + 1))
        pw_ref[0, k] = m * jnp.cos(lam_im * dt * float(k + 1))
        pw_ref[1, k] = m * jnp.sin(lam_im * dt * float(k + 1))


def _s5_prep(lre, lim, ldt, bre, bim, seg):
    npair, gw = lre.shape[0], bre.shape[0]
    return pl.pallas_call(
        functools.partial(_s5_prep_kernel, seg),
        out_shape=[jax.ShapeDtypeStruct((2, npair, LANES), F32),
                   jax.ShapeDtypeStruct((2, seg, npair, LANES), F32),
                   jax.ShapeDtypeStruct((gw, npair, LANES), F32),
                   jax.ShapeDtypeStruct((gw, npair, LANES), F32)],
        name="s5_prep",
    )(lre, lim, ldt, bre, bim)


def _gelu_tanh(x):
    return 0.5 * x * (1.0 + jnp.tanh(math.sqrt(2.0 / math.pi) * (x + 0.044715 * (x * x * x))))


def _s5_kernel(rev, final, tl, *refs):
    if final:
        (main_ref, yf_ref, a_ref, pw_ref, bm_ref, cm_ref, dsk_ref, wglu_ref, bglu_ref,
         o_ref, carry_ref, re_scr, im_scr, st_scr) = refs
    else:
        (main_ref, a_ref, pw_ref, bm_ref, cm_ref, o_ref, carry_ref, re_scr, im_scr, st_scr) = refs
    i = pl.program_id(1)
    width = o_ref.shape[1]
    ntile = bm_ref.shape[0]
    tw = width // ntile
    sw = bm_ref.shape[2] // 2
    npt = sw // LANES
    seg = tl // SUBLANES
    pitch = seg + SEG_PAD

    @pl.when(i == 0)
    def _():
        carry_ref[...] = jnp.zeros_like(carry_ref)

    u = main_ref[:, 0:width]
    for gt in range(ntile):
        bu = _mm(u[:, gt * tw:(gt + 1) * tw], bm_ref[gt])
        for j in range(npt):
            p = gt * npt + j
            for s in range(SUBLANES):
                re_scr[p, s * pitch:s * pitch + seg, :] = bu[s * seg:(s + 1) * seg, j * LANES:(j + 1) * LANES]
                im_scr[p, s * pitch:s * pitch + seg, :] = bu[s * seg:(s + 1) * seg, sw + j * LANES:sw + (j + 1) * LANES]

    order = range(seg - 1, -1, -1) if rev else range(seg)
    npair = ntile * npt

    def local_scan(p, carry):
        ar = jnp.broadcast_to(a_ref[0, pl.ds(p, 1), :], (SUBLANES, LANES))
        ai = jnp.broadcast_to(a_ref[1, pl.ds(p, 1), :], (SUBLANES, LANES))
        sr = jnp.zeros((SUBLANES, LANES), F32)
        si = jnp.zeros((SUBLANES, LANES), F32)
        for k in order:
            br = re_scr[p, pl.ds(k, SUBLANES, stride=pitch), :]
            bi = im_scr[p, pl.ds(k, SUBLANES, stride=pitch), :]
            sr, si = ar * sr - ai * si + br, ar * si + ai * sr + bi
            re_scr[p, pl.ds(k, SUBLANES, stride=pitch), :] = sr
            im_scr[p, pl.ds(k, SUBLANES, stride=pitch), :] = si
        return carry

    lax.fori_loop(0, npair, local_scan, 0)

    last = 0 if rev else seg - 1
    seg_order = range(SUBLANES - 1, -1, -1) if rev else range(SUBLANES)
    for gt in range(ntile):
        for j in range(npt):
            p = gt * npt + j
            pr = pw_ref[0, p]
            pi = pw_ref[1, p]
            fr = pr[last:last + 1, :]
            fi = pi[last:last + 1, :]
            cr = carry_ref[0:1, p * LANES:(p + 1) * LANES]
            ci = carry_ref[1:2, p * LANES:(p + 1) * LANES]
            for s in seg_order:
                rows = slice(s * seg, (s + 1) * seg)
                st_scr[rows, j * LANES:(j + 1) * LANES] = (
                    re_scr[p, s * pitch:s * pitch + seg, :] + pr * cr - pi * ci).astype(st_scr.dtype)
                st_scr[rows, sw + j * LANES:sw + (j + 1) * LANES] = (
                    im_scr[p, s * pitch:s * pitch + seg, :] + pr * ci + pi * cr).astype(st_scr.dtype)
                er = re_scr[p, pl.ds(s * pitch + last, 1), :]
                ei = im_scr[p, pl.ds(s * pitch + last, 1), :]
                cr, ci = fr * cr - fi * ci + er, fr * ci + fi * cr + ei
            carry_ref[0:1, p * LANES:(p + 1) * LANES] = cr
            carry_ref[1:2, p * LANES:(p + 1) * LANES] = ci
        y = jnp.dot(st_scr[...], cm_ref[gt], preferred_element_type=F32)
        o_ref[:, gt * tw:(gt + 1) * tw] = y

    if final:
        y = o_ref[...] + yf_ref[...] + dsk_ref[...] * u
        z = _gelu_tanh(y)
        z = z * _sigmoid(_mm(z, wglu_ref[...]) + bglu_ref[...])
        o_ref[...] = z * _silu(main_ref[:, width:2 * width])


def _s5_dir(rev, proj, yf, a_vec, pw, bm, cm, dsk, wglu, bglu):
    nb, seq_len, two_w = proj.shape
    width = two_w // 2
    tl = TL_MIX
    nblk = seq_len // tl
    pos = _seq_pos(rev)
    blk_spec = lambda wd: pl.BlockSpec((None, tl, wd), lambda b, i: (b, pos(i, nblk), 0))
    final = yf is not None
    seg = tl // SUBLANES
    pitch = seg + SEG_PAD
    npair = a_vec.shape[1]
    sw2 = bm.shape[2]
    ins = [proj] + ([yf] if final else []) + [a_vec, pw, bm, cm] + ([dsk, wglu, bglu] if final else [])
    specs = [blk_spec(two_w)] + ([blk_spec(width)] if final else []) + [
        _const_spec(a_vec.shape), _const_spec(pw.shape), _const_spec(bm.shape), _const_spec(cm.shape)] + (
        [_const_spec(dsk.shape), _const_spec(wglu.shape), _const_spec(bglu.shape)] if final else [])
    return pl.pallas_call(
        functools.partial(_s5_kernel, rev, final, tl),
        grid=(nb, nblk),
        in_specs=specs,
        out_specs=blk_spec(width),
        out_shape=jax.ShapeDtypeStruct((nb, seq_len, width), F32),
        scratch_shapes=[pltpu.VMEM((SUBLANES, npair * LANES), F32),
                        pltpu.VMEM((npair, SUBLANES * pitch, LANES), F32),
                        pltpu.VMEM((npair, SUBLANES * pitch, LANES), F32),
                        pltpu.VMEM((tl, sw2), MXU_DTYPE)],
        compiler_params=_params("parallel", "arbitrary"),
        name="s5_bwd" if rev else "s5_fwd",
    )(*ins)


def _s5_branch(proj, lam_re, lam_im, log_dt, b_re, b_im, c_re, c_im, dsk, wglu, bglu):
    _, groups, states = lam_re.shape
    gw = b_re.shape[3]
    width = groups * gw
    gpt = 2 * LANES // gw
    ntile = groups // gpt
    npair = groups * states // LANES
    seg = TL_MIX // SUBLANES
    eye = jnp.eye(gpt, dtype=F32)
    y = None
    for d, rev in enumerate((False, True)):
        lre = lam_re[d].reshape(npair, LANES)
        lim = lam_im[d].reshape(npair, LANES)
        ldt = jnp.broadcast_to(log_dt[d][:, None], (groups, states)).reshape(npair, LANES)
        bre = b_re[d].transpose(2, 0, 1).reshape(gw, npair, LANES)
        bim = b_im[d].transpose(2, 0, 1).reshape(gw, npair, LANES)
        a_vec, pw, bb_re, bb_im = _s5_prep(lre, lim, ldt, bre, bim, seg)
        pw = pw.transpose(0, 2, 1, 3)
        if rev:
            pw = jnp.flip(pw, axis=2)
        bbr = bb_re.reshape(gw, ntile, gpt, states)
        bbi = bb_im.reshape(gw, ntile, gpt, states)
        bm = jnp.concatenate(
            [jnp.einsum('intp,tu->ntiup', m, eye).reshape(ntile, gpt * gw, gpt * states) for m in (bbr, bbi)],
            axis=2).astype(MXU_DTYPE)
        cr = c_re[d].reshape(ntile, gpt, gw, states)
        ci = c_im[d].reshape(ntile, gpt, gw, states)
        cm = jnp.concatenate(
            [jnp.einsum('ntjp,tu->ntpuj', m, eye).reshape(ntile, gpt * states, gpt * gw) for m in (cr, -ci)],
            axis=1).astype(MXU_DTYPE)
        if rev:
            y = _s5_dir(True, proj, y, a_vec, pw, bm, cm, dsk, wglu, bglu)
        else:
            y = _s5_dir(False, proj, None, a_vec, pw, bm, cm, None, None, None)
    del width
    return y


def _layer(x, p, final_g, final):
    nb, seq_len, d = x.shape
    bw = d // 2
    x2 = x.reshape(nb * seq_len, d)
    bf = lambda a: a.astype(MXU_DTYPE)
    row = lambda a: a.reshape(1, -1)

    lru_blocks = p['lru_w_a'].shape[1]
    gla_heads = GLA_HEADS
    gla_dk = p['gla_w_up'].shape[2] // gla_heads
    gla_dv = bw // gla_heads
    gla_rank = p['gla_w_up'].shape[1]
    dn_heads = p['dn_a_log'].shape[1]
    dn_dh = bw // dn_heads
    widths = (bw, bw, gla_heads * gla_dk, gla_heads * gla_dk, bw, bw, 2 * gla_rank, 3 * bw, bw, 4 * dn_heads, bw, bw)
    offs = [0]
    for wd in widths:
        offs.append(offs[-1] + wd)
    w_in = p['w_in']
    cols = lambda a, b: w_in[:, offs[a]:offs[b]]
    n_gate = 4 * dn_heads
    n_small = n_gate + 2 * gla_rank
    w_small = jnp.concatenate([cols(9, 10), cols(6, 7), jnp.zeros((d, LANES - n_small), w_in.dtype)], axis=1)
    w_cat = bf(jnp.concatenate([cols(0, 2), cols(2, 6), cols(7, 9), cols(10, 12), w_small], axis=1))
    out_w = (2 * bw, offs[6] - offs[2], 4 * bw, 2 * bw, LANES)
    lru_p, gla_p, dn_p, s5_p, small = _in_proj(x2, row(p['norm_g']), w_cat, out_w)
    seq = lambda a: a.reshape(nb, seq_len, a.shape[1])
    lru_p, gla_p, dn_p, s5_p, small = seq(lru_p), seq(gla_p), seq(dn_p), seq(s5_p), seq(small)

    lanes_per_blk = bw // lru_blocks
    y_lru = None
    for dct, rev in enumerate((False, True)):
        wg = bf(jnp.concatenate([p['lru_w_a'][dct], p['lru_w_x'][dct]], axis=2))
        bg = jnp.concatenate([p['lru_b_a'][dct].reshape(lru_blocks, 1, lanes_per_blk),
                              p['lru_b_x'][dct].reshape(lru_blocks, 1, lanes_per_blk)], axis=2)
        y_lru = _lru_dir(rev, lru_p, y_lru, p['lru_conv_w'], row(p['lru_conv_b']), wg, bg, row(p['lru_lambda'][dct]))

    y_gla = None
    for dct, rev in enumerate((False, True)):
        lo = n_gate + dct * gla_rank
        wup = jnp.zeros((LANES, gla_heads * gla_dk), F32).at[lo:lo + gla_rank].set(p['gla_w_up'][dct])
        y_gla = _gla_dir(rev, gla_p, small, y_gla, bf(wup), row(p['gla_b_up'][dct]), row(p['gla_norm_g']),
                         gla_heads, gla_dk, gla_dv)

    alpha_lanes = (jnp.arange(2)[:, None] * 2 * dn_heads + dn_heads + jnp.arange(dn_heads)[None, :]).reshape(-1)
    beta_lanes = alpha_lanes - dn_heads
    gate_consts = jnp.zeros((SUBLANES, LANES), F32)
    gate_consts = gate_consts.at[0, alpha_lanes].set(p['dn_a_log'].reshape(-1))
    gate_consts = gate_consts.at[1, alpha_lanes].set(p['dn_dt_bias'].reshape(-1))
    gate_consts = gate_consts.at[2, beta_lanes].set(1.0).at[3, alpha_lanes].set(1.0)
    y_dn = _dn_branch(dn_p, small, p['dn_conv_w'], gate_consts, row(p['dn_norm_g']), dn_heads, dn_dh)

    y_s5 = _s5_branch(s5_p, p['s5_lambda_re'], p['s5_lambda_im'], p['s5_log_dt'], p['s5_b_re'], p['s5_b_im'],
                      p['s5_c_re'], p['s5_c_im'], row(p['s5_d']), bf(p['s5_w_glu']), row(p['s5_b_glu']))

    flat = lambda a: a.reshape(nb * seq_len, bw)
    out = _out_proj(x2, [flat(y_lru), flat(y_gla), flat(y_dn), flat(y_s5)], row(p['norm_g']),
                    bf(p['w_merge_gate']), p['b_merge_gate'].reshape(4, 1, d), bf(p['w_branch']), bf(p['w_out']),
                    row(final_g), final)
    return out.reshape(nb, seq_len, d)


_LAYER_PARAM_NAMES = (
    'norm_g', 'w_in', 'lru_conv_w', 'lru_conv_b', 'lru_w_a', 'lru_b_a', 'lru_w_x', 'lru_b_x', 'lru_lambda',
    'gla_w_up', 'gla_b_up', 'gla_norm_g', 'dn_conv_w', 'dn_a_log', 'dn_dt_bias', 'dn_norm_g',
    's5_lambda_re', 's5_lambda_im', 's5_log_dt', 's5_b_re', 's5_b_im', 's5_c_re', 's5_c_im', 's5_d',
    's5_w_glu', 's5_b_glu', 'w_branch', 'w_merge_gate', 'b_merge_gate', 'w_out')


def kernel(x_prompt, x_sample, norm_g, w_in, lru_conv_w, lru_conv_b, lru_w_a, lru_b_a, lru_w_x, lru_b_x, lru_lambda, gla_w_up, gla_b_up, gla_norm_g, dn_conv_w, dn_a_log, dn_dt_bias, dn_norm_g, s5_lambda_re, s5_lambda_im, s5_log_dt, s5_b_re, s5_b_im, s5_c_re, s5_c_im, s5_d, s5_w_glu, s5_b_glu, w_branch, w_merge_gate, b_merge_gate, w_out, final_norm_g):
    stacked = (norm_g, w_in, lru_conv_w, lru_conv_b, lru_w_a, lru_b_a, lru_w_x, lru_b_x, lru_lambda,
               gla_w_up, gla_b_up, gla_norm_g, dn_conv_w, dn_a_log, dn_dt_bias, dn_norm_g,
               s5_lambda_re, s5_lambda_im, s5_log_dt, s5_b_re, s5_b_im, s5_c_re, s5_c_im, s5_d,
               s5_w_glu, s5_b_glu, w_branch, w_merge_gate, b_merge_gate, w_out)
    depth = norm_g.shape[0]
    n_prompt = x_prompt.shape[0]
    x = jnp.concatenate([x_prompt, x_sample], axis=0)
    for layer in range(depth):
        p = {name: arr[layer] for name, arr in zip(_LAYER_PARAM_NAMES, stacked)}
        x = _layer(x, p, final_norm_g, layer == depth - 1)
    return x[:n_prompt], x[n_prompt:]
```

```python
import functools
import math

import jax
import jax.numpy as jnp
from jax import lax
from jax.experimental import pallas as pl
from jax.experimental.pallas import tpu as pltpu

F32 = jnp.float32
MXU_DTYPE = jnp.bfloat16
HI = lax.Precision.HIGHEST

NORM_EPS = 1e-6
CONV_W = 4
CONV_HALO = 8
LRU_C = 8.0
GLA_TAU = 16.0
GLA_HEADS = 4
CHUNK = 64
LANES = 128
SUBLANES = 8
SEG_PAD = 4
VMEM_LIMIT = 56 * 1024 * 1024

TM_PROJ = 256
TL_MIX = 256
SOLVE_LANES = 256
S5_SCAN_UNROLL = 4


def _mm(a, b):
    return jnp.dot(a.astype(MXU_DTYPE), b.astype(MXU_DTYPE), preferred_element_type=F32)


def _mm_nt(a, b):
    return lax.dot_general(a.astype(MXU_DTYPE), b.astype(MXU_DTYPE), (((1,), (1,)), ((), ())),
                           preferred_element_type=F32)


def _mm_tn(a, b):
    return lax.dot_general(a.astype(MXU_DTYPE), b.astype(MXU_DTYPE), (((0,), (0,)), ((), ())),
                           preferred_element_type=F32)


def _mm_hi(a, b):
    return jnp.dot(a, b, precision=HI, preferred_element_type=F32)


def _sigmoid(x):
    return 1.0 / (1.0 + jnp.exp(-x))


def _silu(x):
    return x * _sigmoid(x)


def _softplus(x):
    return jnp.maximum(x, 0.0) + jnp.log1p(jnp.exp(-jnp.abs(x)))


def _rms(x, g):
    return x * lax.rsqrt(jnp.mean(x * x, axis=-1, keepdims=True) + NORM_EPS) * g


def _params(*sem):
    return pltpu.CompilerParams(dimension_semantics=sem, vmem_limit_bytes=VMEM_LIMIT)


def _const_spec(shape):
    nd = len(shape)
    return pl.BlockSpec(shape, lambda *_: (0,) * nd)


def _seq_pos(rev):
    if rev:
        return lambda i, n: n - 1 - i
    return lambda i, n: i


def _tri_mask(rev, strict):
    t = lax.broadcasted_iota(jnp.int32, (CHUNK, CHUNK), 0)
    s = lax.broadcasted_iota(jnp.int32, (CHUNK, CHUNK), 1)
    if rev:
        return (s > t) if strict else (s >= t)
    return (s < t) if strict else (s <= t)


def _block_cum(rev, tl):
    t = jnp.arange(tl)[:, None]
    s = jnp.arange(tl)[None, :]
    same = (t // CHUNK) == (s // CHUNK)
    return (same & ((s >= t) if rev else (s <= t))).astype(F32)


def _chunk_order(rev, nchunk):
    return range(nchunk - 1, -1, -1) if rev else range(nchunk)


def _in_proj_kernel(widths, x_ref, g_ref, w_ref, *o_refs):
    xn = _rms(x_ref[...], g_ref[...]).astype(MXU_DTYPE)
    off = 0
    for o_ref, wd in zip(o_refs, widths):
        o_ref[...] = jnp.dot(xn, w_ref[:, off:off + wd], preferred_element_type=F32)
        off += wd


def _in_proj(x2, g, w_cat, widths):
    t, d = x2.shape
    n = w_cat.shape[1]
    return pl.pallas_call(
        functools.partial(_in_proj_kernel, widths),
        grid=(t // TM_PROJ,),
        in_specs=[pl.BlockSpec((TM_PROJ, d), lambda i: (i, 0)), _const_spec((1, d)), _const_spec((d, n))],
        out_specs=[pl.BlockSpec((TM_PROJ, wd), lambda i: (i, 0)) for wd in widths],
        out_shape=[jax.ShapeDtypeStruct((t, wd), F32) for wd in widths],
        compiler_params=_params("parallel"),
        name="in_proj",
    )(x2, g, w_cat)


def _out_proj_kernel(final, gla_heads, dn_heads, x_ref, ylru_ref, gof_ref, gob_ref, ggate_ref, dof_ref, dob_ref,
                     dgate_ref, ys5_ref, g_ref, gng_ref, dng_ref, wmg_ref, bmg_ref, wbr_ref, wout_ref, fg_ref,
                     o_ref):
    def finish(of_ref, ob_ref, gate_ref, ng_ref, heads):
        o = of_ref[...] + ob_ref[...]
        dv = o.shape[1] // heads
        normed = [_rms(o[:, h * dv:(h + 1) * dv], ng_ref[...]) for h in range(heads)]
        return jnp.concatenate(normed, axis=1) * _silu(gate_ref[...])

    x = x_ref[...]
    xn = _rms(x, g_ref[...]).astype(MXU_DTYPE)
    ys = (ylru_ref[...], finish(gof_ref, gob_ref, ggate_ref, gng_ref, gla_heads),
          finish(dof_ref, dob_ref, dgate_ref, dng_ref, dn_heads), ys5_ref[...])
    merged = None
    for n, y in enumerate(ys):
        gate = _sigmoid(jnp.dot(xn, wmg_ref[n], preferred_element_type=F32) + bmg_ref[n])
        term = gate * _mm(y, wbr_ref[n])
        merged = term if merged is None else merged + term
    out = x + _mm(merged, wout_ref[...])
    if final:
        out = _rms(out, fg_ref[...])
    o_ref[...] = out


def _out_proj(x2, y_lru, gla, dn, y_s5, g, wmg, bmg, wbr, wout, fg, final):
    t, d = x2.shape
    bw = y_lru.shape[1]
    row = lambda wd, col=0: pl.BlockSpec((TM_PROJ, wd), lambda i: (i, col))
    g_of, g_ob, g_proj, g_col, g_ng, g_heads = gla
    d_of, d_ob, d_proj, d_col, d_ng, d_heads = dn
    return pl.pallas_call(
        functools.partial(_out_proj_kernel, final, g_heads, d_heads),
        grid=(t // TM_PROJ,),
        in_specs=[row(d), row(bw), row(bw), row(bw), row(bw, g_col), row(bw), row(bw), row(bw, d_col), row(bw),
                  _const_spec((1, d)), _const_spec(g_ng.shape), _const_spec(d_ng.shape),
                  _const_spec(wmg.shape), _const_spec(bmg.shape), _const_spec(wbr.shape),
                  _const_spec(wout.shape), _const_spec((1, d))],
        out_specs=row(d),
        out_shape=jax.ShapeDtypeStruct((t, d), F32),
        compiler_params=_params("parallel"),
        name="out_proj",
    )(x2, y_lru, g_of, g_ob, g_proj, d_of, d_ob, d_proj, y_s5, g, g_ng, d_ng, wmg, bmg, wbr, wout, fg)


def _halo_specs(rev, tl, width, seq_len):
    pos = _seq_pos(rev)
    per = tl // CONV_HALO
    last = seq_len // CONV_HALO - 1
    nblk = seq_len // tl
    prev = pl.BlockSpec((None, CONV_HALO, width),
                        lambda b, i: (b, jnp.maximum(pos(i, nblk) * per - 1, 0), 0))
    nxt = pl.BlockSpec((None, CONV_HALO, width),
                       lambda b, i: (b, jnp.minimum((pos(i, nblk) + 1) * per, last), 0))
    return prev, nxt


def _centred_conv(ext_ref, u, prev, nxt, at_start, at_end, w_ref):
    tl = u.shape[0]
    ext_ref[0:CONV_HALO, :] = jnp.where(at_start, 0.0, prev)
    ext_ref[CONV_HALO:CONV_HALO + tl, :] = u
    ext_ref[CONV_HALO + tl:, :] = jnp.where(at_end, 0.0, nxt)
    left = CONV_W // 2
    acc = None
    for j in range(CONV_W):
        term = ext_ref[pl.ds(CONV_HALO + j - left, tl), :] * w_ref[j:j + 1, :]
        acc = term if acc is None else acc + term
    return acc


def _lru_kernel(rev, final, tl, *refs):
    if final:
        (main_ref, prev_ref, next_ref, hf_ref, cw_ref, cb_ref, wg_ref, bg_ref, lam_ref,
         o_ref, carry_ref, ext_ref, a_scr, d_scr) = refs
    else:
        (main_ref, prev_ref, next_ref, cw_ref, cb_ref, wg_ref, bg_ref, lam_ref,
         o_ref, carry_ref, ext_ref, a_scr, d_scr) = refs
    i = pl.program_id(1)
    nblk = pl.num_programs(1)
    blk = _seq_pos(rev)(i, nblk)
    width = o_ref.shape[1]
    nslab = width // LANES
    seg = tl // SUBLANES
    pitch = seg + SEG_PAD

    @pl.when(i == 0)
    def _():
        carry_ref[...] = jnp.zeros_like(carry_ref)

    u = main_ref[:, 0:width]
    xc = _centred_conv(ext_ref, u, prev_ref[...], next_ref[...], blk == 0, blk == nblk - 1, cw_ref) + cb_ref[...]
    c_all = -LRU_C * _softplus(-lam_ref[...])
    for h in range(nslab):
        sl = slice(h * LANES, (h + 1) * LANES)
        xh = xc[:, sl]
        pre = _mm(xh, wg_ref[h]) + bg_ref[h]
        r = _sigmoid(pre[:, 0:LANES])
        ig = _sigmoid(pre[:, LANES:])
        a = jnp.exp(c_all[:, sl] * r)
        drive = jnp.sqrt(1.0 - a * a) * (ig * xh)
        for s in range(SUBLANES):
            a_scr[h, s * pitch:s * pitch + seg, :] = a[s * seg:(s + 1) * seg, :]
            d_scr[h, s * pitch:s * pitch + seg, :] = drive[s * seg:(s + 1) * seg, :]

    order = range(seg - 1, -1, -1) if rev else range(seg)
    for h in range(nslab):
        hloc = jnp.zeros((SUBLANES, LANES), F32)
        prod = jnp.ones((SUBLANES, LANES), F32)
        for k in order:
            ak = a_scr[h, pl.ds(k, SUBLANES, stride=pitch), :]
            dk = d_scr[h, pl.ds(k, SUBLANES, stride=pitch), :]
            hloc = ak * hloc + dk
            prod = prod * ak
            d_scr[h, pl.ds(k, SUBLANES, stride=pitch), :] = hloc
            a_scr[h, pl.ds(k, SUBLANES, stride=pitch), :] = prod

    last = 0 if rev else seg - 1
    seg_order = range(SUBLANES - 1, -1, -1) if rev else range(SUBLANES)
    for h in range(nslab):
        sl = slice(h * LANES, (h + 1) * LANES)
        c = carry_ref[0:1, sl]
        for s in seg_order:
            rows = slice(s * seg, (s + 1) * seg)
            hs = d_scr[h, s * pitch:s * pitch + seg, :] + a_scr[h, s * pitch:s * pitch + seg, :] * c
            if final:
                gate = main_ref[rows, width + h * LANES:width + (h + 1) * LANES]
                o_ref[rows, sl] = (hf_ref[rows, sl] + hs) * _silu(gate)
            else:
                o_ref[rows, sl] = hs
            c = a_scr[h, pl.ds(s * pitch + last, 1), :] * c + d_scr[h, pl.ds(s * pitch + last, 1), :]
        carry_ref[0:1, sl] = c


def _lru_dir(rev, proj, hf, cw, cb, wg, bg, lam):
    nb, seq_len, two_w = proj.shape
    width = two_w // 2
    tl = TL_MIX
    nblk = seq_len // tl
    pos = _seq_pos(rev)
    blk_spec = lambda wd: pl.BlockSpec((None, tl, wd), lambda b, i: (b, pos(i, nblk), 0))
    prev, nxt = _halo_specs(rev, tl, width, seq_len)
    final = hf is not None
    pitch = tl // SUBLANES + SEG_PAD
    ins = [proj, proj, proj] + ([hf] if final else []) + [cw, cb, wg, bg, lam]
    specs = [blk_spec(two_w), prev, nxt] + ([blk_spec(width)] if final else []) + [
        _const_spec(cw.shape), _const_spec(cb.shape), _const_spec(wg.shape), _const_spec(bg.shape),
        _const_spec(lam.shape)]
    return pl.pallas_call(
        functools.partial(_lru_kernel, rev, final, tl),
        grid=(nb, nblk),
        in_specs=specs,
        out_specs=blk_spec(width),
        out_shape=jax.ShapeDtypeStruct((nb, seq_len, width), F32),
        scratch_shapes=[pltpu.VMEM((SUBLANES, width), F32),
                        pltpu.VMEM((tl + 2 * CONV_HALO, width), F32),
                        pltpu.VMEM((width // LANES, SUBLANES * pitch, LANES), F32),
                        pltpu.VMEM((width // LANES, SUBLANES * pitch, LANES), F32)],
        compiler_params=_params("parallel", "arbitrary"),
        name="lru_bwd" if rev else "lru_fwd",
    )(*ins)


def _gla_kernel(tl, heads, dk, dv, mf_ref, mb_ref, sf_ref, sb_ref, wup_ref, bup_ref, cumf_ref, cumb_ref,
                of_ref, ob_ref, st_ref):
    i = pl.program_id(1)
    hk = heads * dk
    nchunk = tl // CHUNK

    @pl.when(i == 0)
    def _():
        st_ref[...] = jnp.zeros_like(st_ref)

    dirs = ((False, mf_ref, sf_ref, cumf_ref, of_ref), (True, mb_ref, sb_ref, cumb_ref, ob_ref))
    for d, (rev, m_ref, s_ref, cum_ref, o_ref) in enumerate(dirs):
        incl = _tri_mask(rev, strict=False)
        end = 0 if rev else CHUNK - 1
        pre = _mm(s_ref[...], wup_ref[d]) + bup_ref[d]
        la = (jnp.minimum(pre, 0.0) - jnp.log1p(jnp.exp(-jnp.abs(pre)))) * (1.0 / GLA_TAU)
        b = _mm_hi(cum_ref[...], la)
        q_dec = m_ref[:, 0:hk] * (dk ** -0.5) * jnp.exp(b)
        k = m_ref[:, hk:2 * hk]
        k_inv = k * jnp.exp(-b)
        states = [st_ref[d, h] for h in range(heads)]
        for c in _chunk_order(rev, nchunk):
            rows = slice(c * CHUNK, (c + 1) * CHUNK)
            b_end = b[c * CHUNK + end:c * CHUNK + end + 1, :]
            k_end = k[rows, :] * jnp.exp(b_end - b[rows, :])
            dec = jnp.exp(b_end)
            for h in range(heads):
                ks = slice(h * dk, (h + 1) * dk)
                v = m_ref[rows, 2 * hk + h * dv:2 * hk + (h + 1) * dv]
                scores = jnp.where(incl, _mm_nt(q_dec[rows, ks], k_inv[rows, ks]), 0.0)
                o_ref[rows, h * dv:(h + 1) * dv] = _mm(scores, v) + _mm_nt(q_dec[rows, ks], states[h])
                states[h] = states[h] * dec[:, ks] + _mm_tn(v, k_end[:, ks])
        for h in range(heads):
            st_ref[d, h] = states[h]


def _gla_both(proj, small, wup, bup, heads, dk, dv):
    nb, seq_len, pw = proj.shape
    tl = TL_MIX
    nblk = seq_len // tl
    fwd = lambda wd: pl.BlockSpec((None, tl, wd), lambda b, i: (b, i, 0))
    bwd = lambda wd: pl.BlockSpec((None, tl, wd), lambda b, i: (b, nblk - 1 - i, 0))
    hv = heads * dv
    sw = small.shape[2]
    cum_f, cum_b = _block_cum(False, tl), _block_cum(True, tl)
    out = jax.ShapeDtypeStruct((nb, seq_len, hv), F32)
    return pl.pallas_call(
        functools.partial(_gla_kernel, tl, heads, dk, dv),
        grid=(nb, nblk),
        in_specs=[fwd(pw), bwd(pw), fwd(sw), bwd(sw), _const_spec(wup.shape), _const_spec(bup.shape),
                  _const_spec(cum_f.shape), _const_spec(cum_b.shape)],
        out_specs=[fwd(hv), bwd(hv)],
        out_shape=[out, out],
        scratch_shapes=[pltpu.VMEM((2, heads, dv, dk), F32)],
        compiler_params=_params("parallel", "arbitrary"),
        name="gla",
    )(proj, proj, small, small, wup, bup, cum_f, cum_b)


def _dn_gates(small, gc_ref):
    beta = _sigmoid(small)
    g = -jnp.exp(gc_ref[0:1, :]) * _softplus(small + gc_ref[1:2, :])
    return jnp.where(gc_ref[2:3, :] > 0.5, beta, jnp.where(gc_ref[3:4, :] > 0.5, g, 0.0))


def _dn_gamma(rev, gc, gct, c, lane):
    incl = _tri_mask(rev, strict=False)
    diff = gc[c * CHUNK:(c + 1) * CHUNK, lane:lane + 1] - gct[lane:lane + 1, c * CHUNK:(c + 1) * CHUNK]
    return jnp.where(incl, jnp.exp(jnp.where(incl, diff, 0.0)), 0.0)


def _dn_prep_kernel(tl, heads, dh, main_ref, prev_ref, next_ref, small_ref, cw_ref, gcst_ref, cumf_ref, cumb_ref,
                    qkv_ref, bg_ref, gc_ref, af_ref, ab_ref, ext_ref):
    i = pl.program_id(1)
    nblk = pl.num_programs(1)
    hw = heads * dh
    x = main_ref[:, 0:3 * hw]
    y = _silu(_centred_conv(ext_ref, x, prev_ref[...], next_ref[...], i == 0, i == nblk - 1, cw_ref))
    for h in range(heads):
        for part, scale in ((0, dh ** -0.5), (1, 1.0)):
            sl = slice(part * hw + h * dh, part * hw + (h + 1) * dh)
            z = y[:, sl]
            qkv_ref[:, sl] = z * lax.rsqrt(jnp.sum(z * z, axis=-1, keepdims=True) + NORM_EPS) * scale
    qkv_ref[:, 2 * hw:] = y[:, 2 * hw:]
    bg = _dn_gates(small_ref[...], gcst_ref)
    bg_ref[...] = bg
    lane = lax.broadcasted_iota(jnp.int32, bg.shape, 1)
    gc = jnp.where(lane < 2 * heads, _mm_hi(cumf_ref[...], bg), _mm_hi(cumb_ref[...], bg))
    gc_ref[...] = gc
    gct = gc.T
    for rev, a_ref in ((False, af_ref), (True, ab_ref)):
        strict = _tri_mask(rev, strict=True)
        base = 2 * heads if rev else 0
        for c in range(tl // CHUNK):
            rows = slice(c * CHUNK, (c + 1) * CHUNK)
            for h in range(heads):
                k = qkv_ref[rows, hw + h * dh:hw + (h + 1) * dh]
                gamma = _dn_gamma(rev, gc, gct, c, base + heads + h)
                a_ref[c, h] = jnp.where(strict, _mm_nt(k * bg[rows, base + h:base + h + 1], k) * gamma, 0.0)


def _dn_prep(proj, small, cw, gate_consts, heads, dh):
    nb, seq_len, pw = proj.shape
    tl = TL_MIX
    nblk = seq_len // tl
    hw = heads * dh
    blk_spec = lambda wd: pl.BlockSpec((None, tl, wd), lambda b, i: (b, i, 0))
    prev, nxt = _halo_specs(False, tl, 3 * hw, seq_len)
    a_spec = pl.BlockSpec((None, tl // CHUNK, heads, CHUNK, CHUNK), lambda b, i: (b, i, 0, 0, 0))
    a_shape = jax.ShapeDtypeStruct((nb, seq_len // CHUNK, heads, CHUNK, CHUNK), F32)
    lanes_shape = jax.ShapeDtypeStruct((nb, seq_len, LANES), F32)
    cum_f, cum_b = _block_cum(False, tl), _block_cum(True, tl)
    return pl.pallas_call(
        functools.partial(_dn_prep_kernel, tl, heads, dh),
        grid=(nb, nblk),
        in_specs=[blk_spec(pw), prev, nxt, blk_spec(small.shape[2]), _const_spec(cw.shape),
                  _const_spec(gate_consts.shape), _const_spec(cum_f.shape), _const_spec(cum_b.shape)],
        out_specs=[blk_spec(3 * hw), blk_spec(LANES), blk_spec(LANES), a_spec, a_spec],
        out_shape=[jax.ShapeDtypeStruct((nb, seq_len, 3 * hw), F32), lanes_shape, lanes_shape, a_shape, a_shape],
        scratch_shapes=[pltpu.VMEM((tl + 2 * CONV_HALO, 3 * hw), F32)],
        compiler_params=_params("parallel", "parallel"),
        name="dn_prep",
    )(proj, proj, proj, small, cw, gate_consts, cum_f, cum_b)


def _dn_solve_kernel(a_ref, t_ref):
    shape = t_ref.shape[1:]
    col = lax.broadcasted_iota(jnp.int32, shape, 0)
    t_ref[0] = jnp.where(col == 0, 1.0, 0.0)

    def row(t, carry):
        def term(s, acc):
            return acc - a_ref[t, pl.ds(s, 1), :] * t_ref[s]
        t_ref[t] = lax.fori_loop(0, t, term, jnp.where(col == t, 1.0, 0.0))
        return carry

    lax.fori_loop(1, CHUNK, row, 0)


def _dn_solve(a_t):
    n = a_t.shape[2]
    spec = pl.BlockSpec((CHUNK, CHUNK, SOLVE_LANES), lambda i: (0, 0, i))
    return pl.pallas_call(
        _dn_solve_kernel,
        grid=(n // SOLVE_LANES,),
        in_specs=[spec],
        out_specs=spec,
        out_shape=jax.ShapeDtypeStruct(a_t.shape, F32),
        compiler_params=_params("parallel"),
        name="dn_solve",
    )(a_t)


def _dn_main_kernel(tl, heads, dh, qf_ref, bgf_ref, gcf_ref, tf_ref, qb_ref, bgb_ref, gcb_ref, tb_ref,
                    of_ref, ob_ref, s_ref):
    i = pl.program_id(1)
    hw = heads * dh
    nchunk = tl // CHUNK

    @pl.when(i == 0)
    def _():
        s_ref[...] = jnp.zeros_like(s_ref)

    dirs = ((False, qf_ref, bgf_ref, gcf_ref, tf_ref, of_ref), (True, qb_ref, bgb_ref, gcb_ref, tb_ref, ob_ref))
    for d, (rev, qkv_ref, bg_ref, gc_ref, t_ref, o_ref) in enumerate(dirs):
        incl = _tri_mask(rev, strict=False)
        base = 2 * heads * d
        end = 0 if rev else CHUNK - 1
        bg = bg_ref[...]
        gc = gc_ref[...]
        gct = gc.T
        e_g = jnp.exp(gc)
        states = [s_ref[d, h] for h in range(heads)]
        for c in _chunk_order(rev, nchunk):
            rows = slice(c * CHUNK, (c + 1) * CHUNK)
            g_end = gc[c * CHUNK + end:c * CHUNK + end + 1, :]
            e_rest = jnp.exp(g_end - gc[rows, :])
            e_end = jnp.exp(g_end)
            for h in range(heads):
                gl = base + heads + h
                q = qkv_ref[rows, h * dh:(h + 1) * dh]
                k = qkv_ref[rows, hw + h * dh:hw + (h + 1) * dh]
                v = qkv_ref[rows, 2 * hw + h * dh:2 * hw + (h + 1) * dh]
                bh = bg[rows, base + h:base + h + 1]
                eg = e_g[rows, gl:gl + 1]
                gamma = _dn_gamma(rev, gc, gct, c, gl)
                t_mat = t_ref[c, h]
                w = _mm(t_mat, k * bh * eg)
                u = _mm(t_mat, v * bh)
                attn = jnp.where(incl, _mm_nt(q, k) * gamma, 0.0)
                ws_qs = _mm(jnp.concatenate([w, q * eg], axis=0), states[h])
                v_new = u - ws_qs[0:CHUNK, :]
                o_ref[rows, h * dh:(h + 1) * dh] = ws_qs[CHUNK:, :] + _mm(attn, v_new)
                states[h] = e_end[:, gl:gl + 1] * states[h] + _mm_tn(k * e_rest[:, gl:gl + 1], v_new)
        for h in range(heads):
            s_ref[d, h] = states[h]


def _dn_main(qkv, bg, gc, t_f, t_b, heads, dh):
    nb, seq_len, _ = qkv.shape
    tl = TL_MIX
    nblk = seq_len // tl
    hw = heads * dh
    fwd = lambda wd: pl.BlockSpec((None, tl, wd), lambda b, i: (b, i, 0))
    bwd = lambda wd: pl.BlockSpec((None, tl, wd), lambda b, i: (b, nblk - 1 - i, 0))
    t_blk = (None, tl // CHUNK, heads, CHUNK, CHUNK)
    t_fwd = pl.BlockSpec(t_blk, lambda b, i: (b, i, 0, 0, 0))
    t_bwd = pl.BlockSpec(t_blk, lambda b, i: (b, nblk - 1 - i, 0, 0, 0))
    out = jax.ShapeDtypeStruct((nb, seq_len, hw), F32)
    return pl.pallas_call(
        functools.partial(_dn_main_kernel, tl, heads, dh),
        grid=(nb, nblk),
        in_specs=[fwd(3 * hw), fwd(LANES), fwd(LANES), t_fwd, bwd(3 * hw), bwd(LANES), bwd(LANES), t_bwd],
        out_specs=[fwd(hw), bwd(hw)],
        out_shape=[out, out],
        scratch_shapes=[pltpu.VMEM((2, heads, dh, dh), F32)],
        compiler_params=_params("parallel", "arbitrary"),
        name="dn_main",
    )(qkv, bg, gc, t_f, qkv, bg, gc, t_b)


def _dn_branch(proj, small, cw, gate_consts, heads, dh):
    nb, seq_len, _ = proj.shape
    qkv, bg, gc, a_f, a_b = _dn_prep(proj, small, cw, gate_consts, heads, dh)
    n_inst = nb * (seq_len // CHUNK) * heads
    a_f = a_f.reshape(n_inst, CHUNK, CHUNK).transpose(1, 2, 0)
    a_b = jnp.flip(a_b.reshape(n_inst, CHUNK, CHUNK), axis=(1, 2)).transpose(1, 2, 0)
    t_both = _dn_solve(jnp.concatenate([a_f, a_b], axis=2))
    shape5 = (nb, seq_len // CHUNK, heads, CHUNK, CHUNK)
    t_f = t_both[:, :, :n_inst].transpose(2, 0, 1).reshape(shape5)
    t_b = jnp.flip(t_both[:, :, n_inst:].transpose(2, 0, 1), axis=(1, 2)).reshape(shape5)
    return _dn_main(qkv, bg, gc, t_f, t_b, heads, dh)


def _s5_prep_kernel(seg, lre_ref, lim_ref, ldt_ref, bre_ref, bim_ref, a_ref, pw_ref, bbre_ref, bbim_ref):
    lam_re = lre_ref[...]
    lam_im = lim_ref[...]
    dt = jnp.exp(ldt_ref[...])
    mag = jnp.exp(lam_re * dt)
    a_re = mag * jnp.cos(lam_im * dt)
    a_im = mag * jnp.sin(lam_im * dt)
    den = lam_re * lam_re + lam_im * lam_im
    n_re = a_re - 1.0
    f_re = (n_re * lam_re + a_im * lam_im) / den
    f_im = (a_im * lam_re - n_re * lam_im) / den
    a_ref[0] = a_re
    a_ref[1] = a_im
    for i in range(bre_ref.shape[0]):
        bbre_ref[i] = f_re * bre_ref[i] - f_im * bim_ref[i]
        bbim_ref[i] = f_re * bim_ref[i] + f_im * bre_ref[i]
    for k in range(seg):
        m = jnp.exp(lam_re * dt * float(k + 1))
        pw_ref[0, k] = m * jnp.cos(lam_im * dt * float(k + 1))
        pw_ref[1, k] = m * jnp.sin(lam_im * dt * float(k + 1))


def _s5_prep(lre, lim, ldt, bre, bim, seg):
    npair, gw = lre.shape[0], bre.shape[0]
    return pl.pallas_call(
        functools.partial(_s5_prep_kernel, seg),
        out_shape=[jax.ShapeDtypeStruct((2, npair, LANES), F32),
                   jax.ShapeDtypeStruct((2, seg, npair, LANES), F32),
                   jax.ShapeDtypeStruct((gw, npair, LANES), F32),
                   jax.ShapeDtypeStruct((gw, npair, LANES), F32)],
        name="s5_prep",
    )(lre, lim, ldt, bre, bim)


def _gelu_tanh(x):
    return 0.5 * x * (1.0 + jnp.tanh(math.sqrt(2.0 / math.pi) * (x + 0.044715 * (x * x * x))))


def _s5_kernel(rev, final, tl, *refs):
    if final:
        (main_ref, yf_ref, a_ref, pw_ref, bm_ref, cm_ref, dsk_ref, wglu_ref, bglu_ref,
         o_ref, carry_ref, re_scr, im_scr, st_scr) = refs
    else:
        (main_ref, a_ref, pw_ref, bm_ref, cm_ref, o_ref, carry_ref, re_scr, im_scr, st_scr) = refs
    i = pl.program_id(1)
    width = o_ref.shape[1]
    ntile = bm_ref.shape[0]
    tw = width // ntile
    sw = bm_ref.shape[2] // 2
    npt = sw // LANES
    seg = tl // SUBLANES
    pitch = seg + SEG_PAD

    @pl.when(i == 0)
    def _():
        carry_ref[...] = jnp.zeros_like(carry_ref)

    u = main_ref[:, 0:width]
    for gt in range(ntile):
        bu = _mm(u[:, gt * tw:(gt + 1) * tw], bm_ref[gt])
        for j in range(npt):
            p = gt * npt + j
            for s in range(SUBLANES):
                re_scr[p, s * pitch:s * pitch + seg, :] = bu[s * seg:(s + 1) * seg, j * LANES:(j + 1) * LANES]
                im_scr[p, s * pitch:s * pitch + seg, :] = bu[s * seg:(s + 1) * seg, sw + j * LANES:sw + (j + 1) * LANES]

    order = range(seg - 1, -1, -1) if rev else range(seg)
    npair = ntile * npt

    def local_scan(it, carry):
        ps = [it * S5_SCAN_UNROLL + j for j in range(S5_SCAN_UNROLL)]
        ar = [jnp.broadcast_to(a_ref[0, pl.ds(p, 1), :], (SUBLANES, LANES)) for p in ps]
        ai = [jnp.broadcast_to(a_ref[1, pl.ds(p, 1), :], (SUBLANES, LANES)) for p in ps]
        sr = [jnp.zeros((SUBLANES, LANES), F32) for _ in ps]
        si = [jnp.zeros((SUBLANES, LANES), F32) for _ in ps]
        for k in order:
            for j, p in enumerate(ps):
                br = re_scr[p, pl.ds(k, SUBLANES, stride=pitch), :]
                bi = im_scr[p, pl.ds(k, SUBLANES, stride=pitch), :]
                sr[j], si[j] = ar[j] * sr[j] - ai[j] * si[j] + br, ar[j] * si[j] + ai[j] * sr[j] + bi
                re_scr[p, pl.ds(k, SUBLANES, stride=pitch), :] = sr[j]
                im_scr[p, pl.ds(k, SUBLANES, stride=pitch), :] = si[j]
        return carry

    lax.fori_loop(0, npair // S5_SCAN_UNROLL, local_scan, 0)

    last = 0 if rev else seg - 1
    seg_order = range(SUBLANES - 1, -1, -1) if rev else range(SUBLANES)
    for gt in range(ntile):
        for j in range(npt):
            p = gt * npt + j
            pr = pw_ref[0, p]
            pi = pw_ref[1, p]
            fr = pr[last:last + 1, :]
            fi = pi[last:last + 1, :]
            cr = carry_ref[0:1, p * LANES:(p + 1) * LANES]
            ci = carry_ref[1:2, p * LANES:(p + 1) * LANES]
            for s in seg_order:
                rows = slice(s * seg, (s + 1) * seg)
                st_scr[rows, j * LANES:(j + 1) * LANES] = (
                    re_scr[p, s * pitch:s * pitch + seg, :] + pr * cr - pi * ci).astype(st_scr.dtype)
                st_scr[rows, sw + j * LANES:sw + (j + 1) * LANES] = (
                    im_scr[p, s * pitch:s * pitch + seg, :] + pr * ci + pi * cr).astype(st_scr.dtype)
                er = re_scr[p, pl.ds(s * pitch + last, 1), :]
                ei = im_scr[p, pl.ds(s * pitch + last, 1), :]
                cr, ci = fr * cr - fi * ci + er, fr * ci + fi * cr + ei
            carry_ref[0:1, p * LANES:(p + 1) * LANES] = cr
            carry_ref[1:2, p * LANES:(p + 1) * LANES] = ci
        y = jnp.dot(st_scr[...], cm_ref[gt], preferred_element_type=F32)
        o_ref[:, gt * tw:(gt + 1) * tw] = y

    if final:
        y = o_ref[...] + yf_ref[...] + dsk_ref[...] * u
        z = _gelu_tanh(y)
        z = z * _sigmoid(_mm(z, wglu_ref[...]) + bglu_ref[...])
        o_ref[...] = z * _silu(main_ref[:, width:2 * width])


def _s5_dir(rev, proj, yf, a_vec, pw, bm, cm, dsk, wglu, bglu):
    nb, seq_len, two_w = proj.shape
    width = two_w // 2
    tl = TL_MIX
    nblk = seq_len // tl
    pos = _seq_pos(rev)
    blk_spec = lambda wd: pl.BlockSpec((None, tl, wd), lambda b, i: (b, pos(i, nblk), 0))
    final = yf is not None
    seg = tl // SUBLANES
    pitch = seg + SEG_PAD
    npair = a_vec.shape[1]
    sw2 = bm.shape[2]
    ins = [proj] + ([yf] if final else []) + [a_vec, pw, bm, cm] + ([dsk, wglu, bglu] if final else [])
    specs = [blk_spec(two_w)] + ([blk_spec(width)] if final else []) + [
        _const_spec(a_vec.shape), _const_spec(pw.shape), _const_spec(bm.shape), _const_spec(cm.shape)] + (
        [_const_spec(dsk.shape), _const_spec(wglu.shape), _const_spec(bglu.shape)] if final else [])
    return pl.pallas_call(
        functools.partial(_s5_kernel, rev, final, tl),
        grid=(nb, nblk),
        in_specs=specs,
        out_specs=blk_spec(width),
        out_shape=jax.ShapeDtypeStruct((nb, seq_len, width), F32),
        scratch_shapes=[pltpu.VMEM((SUBLANES, npair * LANES), F32),
                        pltpu.VMEM((npair, SUBLANES * pitch, LANES), F32),
                        pltpu.VMEM((npair, SUBLANES * pitch, LANES), F32),
                        pltpu.VMEM((tl, sw2), MXU_DTYPE)],
        compiler_params=_params("parallel", "arbitrary"),
        name="s5_bwd" if rev else "s5_fwd",
    )(*ins)


def _s5_branch(proj, lam_re, lam_im, log_dt, b_re, b_im, c_re, c_im, dsk, wglu, bglu):
    _, groups, states = lam_re.shape
    gw = b_re.shape[3]
    gpt = 2 * LANES // gw
    ntile = groups // gpt
    npair = groups * states // LANES
    seg = TL_MIX // SUBLANES
    eye = jnp.eye(gpt, dtype=F32)
    y = None
    for d, rev in enumerate((False, True)):
        lre = lam_re[d].reshape(npair, LANES)
        lim = lam_im[d].reshape(npair, LANES)
        ldt = jnp.broadcast_to(log_dt[d][:, None], (groups, states)).reshape(npair, LANES)
        bre = b_re[d].transpose(2, 0, 1).reshape(gw, npair, LANES)
        bim = b_im[d].transpose(2, 0, 1).reshape(gw, npair, LANES)
        a_vec, pw, bb_re, bb_im = _s5_prep(lre, lim, ldt, bre, bim, seg)
        pw = pw.transpose(0, 2, 1, 3)
        if rev:
            pw = jnp.flip(pw, axis=2)
        bbr = bb_re.reshape(gw, ntile, gpt, states)
        bbi = bb_im.reshape(gw, ntile, gpt, states)
        bm = jnp.concatenate(
            [jnp.einsum('intp,tu->ntiup', m, eye).reshape(ntile, gpt * gw, gpt * states) for m in (bbr, bbi)],
            axis=2).astype(MXU_DTYPE)
        cr = c_re[d].reshape(ntile, gpt, gw, states)
        ci = c_im[d].reshape(ntile, gpt, gw, states)
        cm = jnp.concatenate(
            [jnp.einsum('ntjp,tu->ntpuj', m, eye).reshape(ntile, gpt * states, gpt * gw) for m in (cr, -ci)],
            axis=1).astype(MXU_DTYPE)
        if rev:
            y = _s5_dir(True, proj, y, a_vec, pw, bm, cm, dsk, wglu, bglu)
        else:
            y = _s5_dir(False, proj, None, a_vec, pw, bm, cm, None, None, None)
    return y


def _layer(x, p, final_g, final):
    nb, seq_len, d = x.shape
    bw = d // 2
    x2 = x.reshape(nb * seq_len, d)
    bf = lambda a: a.astype(MXU_DTYPE)
    row = lambda a: a.reshape(1, -1)

    lru_blocks = p['lru_w_a'].shape[1]
    gla_heads = GLA_HEADS
    gla_dk = p['gla_w_up'].shape[2] // gla_heads
    gla_dv = bw // gla_heads
    gla_rank = p['gla_w_up'].shape[1]
    dn_heads = p['dn_a_log'].shape[1]
    dn_dh = bw // dn_heads
    widths = (bw, bw, gla_heads * gla_dk, gla_heads * gla_dk, bw, bw, 2 * gla_rank, 3 * bw, bw, 4 * dn_heads, bw, bw)
    offs = [0]
    for wd in widths:
        offs.append(offs[-1] + wd)
    w_in = p['w_in']
    cols = lambda a, b: w_in[:, offs[a]:offs[b]]
    n_gate = 4 * dn_heads
    n_small = n_gate + 2 * gla_rank
    w_small = jnp.concatenate([cols(9, 10), cols(6, 7), jnp.zeros((d, LANES - n_small), w_in.dtype)], axis=1)
    w_cat = bf(jnp.concatenate([cols(0, 2), cols(2, 6), cols(7, 9), cols(10, 12), w_small], axis=1))
    out_w = (2 * bw, offs[6] - offs[2], 4 * bw, 2 * bw, LANES)
    lru_p, gla_p, dn_p, s5_p, small = _in_proj(x2, row(p['norm_g']), w_cat, out_w)
    seq = lambda a: a.reshape(nb, seq_len, a.shape[1])
    flat = lambda a: a.reshape(nb * seq_len, a.shape[2])
    lru_s, gla_s, dn_s, s5_s, small_s = seq(lru_p), seq(gla_p), seq(dn_p), seq(s5_p), seq(small)

    lanes_per_blk = bw // lru_blocks
    y_lru = None
    for dct, rev in enumerate((False, True)):
        wg = bf(jnp.concatenate([p['lru_w_a'][dct], p['lru_w_x'][dct]], axis=2))
        bg = jnp.concatenate([p['lru_b_a'][dct].reshape(lru_blocks, 1, lanes_per_blk),
                              p['lru_b_x'][dct].reshape(lru_blocks, 1, lanes_per_blk)], axis=2)
        y_lru = _lru_dir(rev, lru_s, y_lru, p['lru_conv_w'], row(p['lru_conv_b']), wg, bg, row(p['lru_lambda'][dct]))

    wup = jnp.zeros((2, LANES, gla_heads * gla_dk), F32)
    for dct in range(2):
        lo = n_gate + dct * gla_rank
        wup = wup.at[dct, lo:lo + gla_rank].set(p['gla_w_up'][dct])
    gla_f, gla_b = _gla_both(gla_s, small_s, bf(wup), p['gla_b_up'].reshape(2, 1, -1), gla_heads, gla_dk, gla_dv)

    alpha_lanes = (jnp.arange(2)[:, None] * 2 * dn_heads + dn_heads + jnp.arange(dn_heads)[None, :]).reshape(-1)
    beta_lanes = alpha_lanes - dn_heads
    gate_consts = jnp.zeros((SUBLANES, LANES), F32)
    gate_consts = gate_consts.at[0, alpha_lanes].set(p['dn_a_log'].reshape(-1))
    gate_consts = gate_consts.at[1, alpha_lanes].set(p['dn_dt_bias'].reshape(-1))
    gate_consts = gate_consts.at[2, beta_lanes].set(1.0).at[3, alpha_lanes].set(1.0)
    dn_f, dn_b = _dn_branch(dn_s, small_s, p['dn_conv_w'], gate_consts, dn_heads, dn_dh)

    y_s5 = _s5_branch(s5_s, p['s5_lambda_re'], p['s5_lambda_im'], p['s5_log_dt'], p['s5_b_re'], p['s5_b_im'],
                      p['s5_c_re'], p['s5_c_im'], row(p['s5_d']), bf(p['s5_w_glu']), row(p['s5_b_glu']))

    gla_gate_col = (offs[5] - offs[2]) // bw
    dn_gate_col = (offs[8] - offs[7]) // bw
    out = _out_proj(x2, flat(y_lru),
                    (flat(gla_f), flat(gla_b), gla_p, gla_gate_col, row(p['gla_norm_g']), gla_heads),
                    (flat(dn_f), flat(dn_b), dn_p, dn_gate_col, row(p['dn_norm_g']), dn_heads),
                    flat(y_s5), row(p['norm_g']),
                    bf(p['w_merge_gate']), p['b_merge_gate'].reshape(4, 1, d), bf(p['w_branch']), bf(p['w_out']),
                    row(final_g), final)
    return out.reshape(nb, seq_len, d)


_LAYER_PARAM_NAMES = (
    'norm_g', 'w_in', 'lru_conv_w', 'lru_conv_b', 'lru_w_a', 'lru_b_a', 'lru_w_x', 'lru_b_x', 'lru_lambda',
    'gla_w_up', 'gla_b_up', 'gla_norm_g', 'dn_conv_w', 'dn_a_log', 'dn_dt_bias', 'dn_norm_g',
    's5_lambda_re', 's5_lambda_im', 's5_log_dt', 's5_b_re', 's5_b_im', 's5_c_re', 's5_c_im', 's5_d',
    's5_w_glu', 's5_b_glu', 'w_branch', 'w_merge_gate', 'b_merge_gate', 'w_out')


def kernel(x_prompt, x_sample, norm_g, w_in, lru_conv_w, lru_conv_b, lru_w_a, lru_b_a, lru_w_x, lru_b_x, lru_lambda, gla_w_up, gla_b_up, gla_norm_g, dn_conv_w, dn_a_log, dn_dt_bias, dn_norm_g, s5_lambda_re, s5_lambda_im, s5_log_dt, s5_b_re, s5_b_im, s5_c_re, s5_c_im, s5_d, s5_w_glu, s5_b_glu, w_branch, w_merge_gate, b_merge_gate, w_out, final_norm_g):
    stacked = (norm_g, w_in, lru_conv_w, lru_conv_b, lru_w_a, lru_b_a, lru_w_x, lru_b_x, lru_lambda,
               gla_w_up, gla_b_up, gla_norm_g, dn_conv_w, dn_a_log, dn_dt_bias, dn_norm_g,
               s5_lambda_re, s5_lambda_im, s5_log_dt, s5_b_re, s5_b_im, s5_c_re, s5_c_im, s5_d,
               s5_w_glu, s5_b_glu, w_branch, w_merge_gate, b_merge_gate, w_out)
    depth = norm_g.shape[0]
    n_prompt = x_prompt.shape[0]
    x = jnp.concatenate([x_prompt, x_sample], axis=0)
    for layer in range(depth):
        p = {name: arr[layer] for name, arr in zip(_LAYER_PARAM_NAMES, stacked)}
        x = _layer(x, p, final_norm_g, layer == depth - 1)
    return x[:n_prompt], x[n_prompt:]
```

```python
import functools
import math

import jax
import jax.numpy as jnp
from jax import lax
from jax.experimental import pallas as pl
from jax.experimental.pallas import tpu as pltpu

F32 = jnp.float32
MXU_DTYPE = jnp.bfloat16
HI = lax.Precision.HIGHEST

NORM_EPS = 1e-6
CONV_W = 4
CONV_HALO = 8
LRU_C = 8.0
GLA_TAU = 16.0
GLA_HEADS = 4
CHUNK = 64
LANES = 128
SUBLANES = 8
SEG_PAD = 4
VMEM_LIMIT = 56 * 1024 * 1024

TM_PROJ = 256
TL_MIX = 256
SOLVE_LANES = 256
S5_SCAN_UNROLL = 4


def _mm(a, b):
    return jnp.dot(a.astype(MXU_DTYPE), b.astype(MXU_DTYPE), preferred_element_type=F32)


def _mm_nt(a, b):
    return lax.dot_general(a.astype(MXU_DTYPE), b.astype(MXU_DTYPE), (((1,), (1,)), ((), ())),
                           preferred_element_type=F32)


def _mm_tn(a, b):
    return lax.dot_general(a.astype(MXU_DTYPE), b.astype(MXU_DTYPE), (((0,), (0,)), ((), ())),
                           preferred_element_type=F32)


def _cum_dot(cum3, x):
    hi = x.astype(jnp.bfloat16)
    r1 = x - hi.astype(F32)
    mid = r1.astype(jnp.bfloat16)
    lo = (r1 - mid.astype(F32)).astype(jnp.bfloat16)
    return jnp.dot(cum3, jnp.concatenate([hi, mid, lo], axis=0), preferred_element_type=F32)


def _sigmoid(x):
    return 1.0 / (1.0 + jnp.exp(-x))


def _silu(x):
    return x * _sigmoid(x)


def _softplus(x):
    return jnp.maximum(x, 0.0) + jnp.log1p(jnp.exp(-jnp.abs(x)))


def _rms(x, g):
    return x * lax.rsqrt(jnp.mean(x * x, axis=-1, keepdims=True) + NORM_EPS) * g


def _params(*sem):
    return pltpu.CompilerParams(dimension_semantics=sem, vmem_limit_bytes=VMEM_LIMIT)


def _const_spec(shape):
    nd = len(shape)
    return pl.BlockSpec(shape, lambda *_: (0,) * nd)


def _seq_pos(rev):
    if rev:
        return lambda i, n: n - 1 - i
    return lambda i, n: i


def _tri_mask(rev, strict):
    t = lax.broadcasted_iota(jnp.int32, (CHUNK, CHUNK), 0)
    s = lax.broadcasted_iota(jnp.int32, (CHUNK, CHUNK), 1)
    if rev:
        return (s > t) if strict else (s >= t)
    return (s < t) if strict else (s <= t)


def _log2(n):
    assert n > 0 and n & (n - 1) == 0, n
    return n.bit_length() - 1


def _block_mask(nrow, ncol, row_blk, col_blk):
    r = jnp.right_shift(lax.broadcasted_iota(jnp.int32, (nrow, ncol), 0), _log2(row_blk))
    c = jnp.right_shift(lax.broadcasted_iota(jnp.int32, (nrow, ncol), 1), _log2(col_blk))
    return r == c


def _tri_mask_heads(rev, strict, heads):
    t = lax.broadcasted_iota(jnp.int32, (CHUNK, heads * CHUNK), 0)
    s = jnp.bitwise_and(lax.broadcasted_iota(jnp.int32, (CHUNK, heads * CHUNK), 1), CHUNK - 1)
    if rev:
        return (s > t) if strict else (s >= t)
    return (s < t) if strict else (s <= t)


def _block_cum(rev, tl):
    t = jnp.arange(tl)[:, None]
    s = jnp.arange(tl)[None, :]
    same = (t // CHUNK) == (s // CHUNK)
    cum = (same & ((s >= t) if rev else (s <= t))).astype(jnp.bfloat16)
    return jnp.concatenate([cum, cum, cum], axis=1)


def _chunk_order(rev, nchunk):
    return range(nchunk - 1, -1, -1) if rev else range(nchunk)


def _in_proj_kernel(widths, x_ref, g_ref, w_ref, *o_refs):
    xn = _rms(x_ref[...], g_ref[...]).astype(MXU_DTYPE)
    off = 0
    for o_ref, wd in zip(o_refs, widths):
        o_ref[...] = jnp.dot(xn, w_ref[:, off:off + wd], preferred_element_type=F32)
        off += wd


def _in_proj(x2, g, w_cat, widths):
    t, d = x2.shape
    n = w_cat.shape[1]
    return pl.pallas_call(
        functools.partial(_in_proj_kernel, widths),
        grid=(t // TM_PROJ,),
        in_specs=[pl.BlockSpec((TM_PROJ, d), lambda i: (i, 0)), _const_spec((1, d)), _const_spec((d, n))],
        out_specs=[pl.BlockSpec((TM_PROJ, wd), lambda i: (i, 0)) for wd in widths],
        out_shape=[jax.ShapeDtypeStruct((t, wd), F32) for wd in widths],
        compiler_params=_params("parallel"),
        name="in_proj",
    )(x2, g, w_cat)


def _out_proj_kernel(final, gla_heads, dn_heads, x_ref, ylru_ref, gof_ref, gob_ref, ggate_ref, dof_ref, dob_ref,
                     dgate_ref, ys5_ref, g_ref, gng_ref, dng_ref, wmg_ref, bmg_ref, wbr_ref, wout_ref, fg_ref,
                     o_ref):
    def finish(of_ref, ob_ref, gate_ref, ng_ref, heads):
        o = of_ref[...] + ob_ref[...]
        dv = o.shape[1] // heads
        normed = [_rms(o[:, h * dv:(h + 1) * dv], ng_ref[...]) for h in range(heads)]
        return jnp.concatenate(normed, axis=1) * _silu(gate_ref[...])

    x = x_ref[...]
    xn = _rms(x, g_ref[...]).astype(MXU_DTYPE)
    ys = (ylru_ref[...], finish(gof_ref, gob_ref, ggate_ref, gng_ref, gla_heads),
          finish(dof_ref, dob_ref, dgate_ref, dng_ref, dn_heads), ys5_ref[...])
    merged = None
    for n, y in enumerate(ys):
        gate = _sigmoid(jnp.dot(xn, wmg_ref[n], preferred_element_type=F32) + bmg_ref[n])
        term = gate * _mm(y, wbr_ref[n])
        merged = term if merged is None else merged + term
    out = x + _mm(merged, wout_ref[...])
    if final:
        out = _rms(out, fg_ref[...])
    o_ref[...] = out


def _out_proj(x2, y_lru, gla, dn, y_s5, g, wmg, bmg, wbr, wout, fg, final):
    t, d = x2.shape
    bw = y_lru.shape[1]
    row = lambda wd, col=0: pl.BlockSpec((TM_PROJ, wd), lambda i: (i, col))
    g_of, g_ob, g_proj, g_col, g_ng, g_heads = gla
    d_of, d_ob, d_proj, d_col, d_ng, d_heads = dn
    return pl.pallas_call(
        functools.partial(_out_proj_kernel, final, g_heads, d_heads),
        grid=(t // TM_PROJ,),
        in_specs=[row(d), row(bw), row(bw), row(bw), row(bw, g_col), row(bw), row(bw), row(bw, d_col), row(bw),
                  _const_spec((1, d)), _const_spec(g_ng.shape), _const_spec(d_ng.shape),
                  _const_spec(wmg.shape), _const_spec(bmg.shape), _const_spec(wbr.shape),
                  _const_spec(wout.shape), _const_spec((1, d))],
        out_specs=row(d),
        out_shape=jax.ShapeDtypeStruct((t, d), F32),
        compiler_params=_params("parallel"),
        name="out_proj",
    )(x2, y_lru, g_of, g_ob, g_proj, d_of, d_ob, d_proj, y_s5, g, g_ng, d_ng, wmg, bmg, wbr, wout, fg)


def _halo_specs(rev, tl, width, seq_len):
    pos = _seq_pos(rev)
    per = tl // CONV_HALO
    last = seq_len // CONV_HALO - 1
    nblk = seq_len // tl
    prev = pl.BlockSpec((None, CONV_HALO, width),
                        lambda b, i: (b, jnp.maximum(pos(i, nblk) * per - 1, 0), 0))
    nxt = pl.BlockSpec((None, CONV_HALO, width),
                       lambda b, i: (b, jnp.minimum((pos(i, nblk) + 1) * per, last), 0))
    return prev, nxt


def _centred_conv(ext_ref, u, prev, nxt, at_start, at_end, w_ref):
    tl = u.shape[0]
    ext_ref[0:CONV_HALO, :] = jnp.where(at_start, 0.0, prev)
    ext_ref[CONV_HALO:CONV_HALO + tl, :] = u
    ext_ref[CONV_HALO + tl:, :] = jnp.where(at_end, 0.0, nxt)
    left = CONV_W // 2
    acc = None
    for j in range(CONV_W):
        term = ext_ref[pl.ds(CONV_HALO + j - left, tl), :] * w_ref[j:j + 1, :]
        acc = term if acc is None else acc + term
    return acc


def _lru_kernel(rev, final, tl, *refs):
    if final:
        (main_ref, prev_ref, next_ref, hf_ref, cw_ref, cb_ref, wg_ref, bg_ref, lam_ref,
         o_ref, carry_ref, ext_ref, a_scr, d_scr) = refs
    else:
        (main_ref, prev_ref, next_ref, cw_ref, cb_ref, wg_ref, bg_ref, lam_ref,
         o_ref, carry_ref, ext_ref, a_scr, d_scr) = refs
    i = pl.program_id(1)
    nblk = pl.num_programs(1)
    blk = _seq_pos(rev)(i, nblk)
    width = o_ref.shape[1]
    nslab = width // LANES
    seg = tl // SUBLANES
    pitch = seg + SEG_PAD

    @pl.when(i == 0)
    def _():
        carry_ref[...] = jnp.zeros_like(carry_ref)

    u = main_ref[:, 0:width]
    xc = _centred_conv(ext_ref, u, prev_ref[...], next_ref[...], blk == 0, blk == nblk - 1, cw_ref) + cb_ref[...]
    c_all = -LRU_C * _softplus(-lam_ref[...])
    for h in range(nslab):
        sl = slice(h * LANES, (h + 1) * LANES)
        xh = xc[:, sl]
        pre = _mm(xh, wg_ref[h]) + bg_ref[h]
        r = _sigmoid(pre[:, 0:LANES])
        ig = _sigmoid(pre[:, LANES:])
        a = jnp.exp(c_all[:, sl] * r)
        drive = jnp.sqrt(1.0 - a * a) * (ig * xh)
        for s in range(SUBLANES):
            a_scr[h, s * pitch:s * pitch + seg, :] = a[s * seg:(s + 1) * seg, :]
            d_scr[h, s * pitch:s * pitch + seg, :] = drive[s * seg:(s + 1) * seg, :]

    order = range(seg - 1, -1, -1) if rev else range(seg)
    for h in range(nslab):
        hloc = jnp.zeros((SUBLANES, LANES), F32)
        prod = jnp.ones((SUBLANES, LANES), F32)
        for k in order:
            ak = a_scr[h, pl.ds(k, SUBLANES, stride=pitch), :]
            dk = d_scr[h, pl.ds(k, SUBLANES, stride=pitch), :]
            hloc = ak * hloc + dk
            prod = prod * ak
            d_scr[h, pl.ds(k, SUBLANES, stride=pitch), :] = hloc
            a_scr[h, pl.ds(k, SUBLANES, stride=pitch), :] = prod

    last = 0 if rev else seg - 1
    seg_order = range(SUBLANES - 1, -1, -1) if rev else range(SUBLANES)
    for h in range(nslab):
        sl = slice(h * LANES, (h + 1) * LANES)
        c = carry_ref[0:1, sl]
        for s in seg_order:
            rows = slice(s * seg, (s + 1) * seg)
            hs = d_scr[h, s * pitch:s * pitch + seg, :] + a_scr[h, s * pitch:s * pitch + seg, :] * c
            if final:
                gate = main_ref[rows, width + h * LANES:width + (h + 1) * LANES]
                o_ref[rows, sl] = (hf_ref[rows, sl] + hs) * _silu(gate)
            else:
                o_ref[rows, sl] = hs
            c = a_scr[h, pl.ds(s * pitch + last, 1), :] * c + d_scr[h, pl.ds(s * pitch + last, 1), :]
        carry_ref[0:1, sl] = c


def _lru_dir(rev, proj, hf, cw, cb, wg, bg, lam):
    nb, seq_len, two_w = proj.shape
    width = two_w // 2
    tl = TL_MIX
    nblk = seq_len // tl
    pos = _seq_pos(rev)
    blk_spec = lambda wd: pl.BlockSpec((None, tl, wd), lambda b, i: (b, pos(i, nblk), 0))
    prev, nxt = _halo_specs(rev, tl, width, seq_len)
    final = hf is not None
    pitch = tl // SUBLANES + SEG_PAD
    ins = [proj, proj, proj] + ([hf] if final else []) + [cw, cb, wg, bg, lam]
    specs = [blk_spec(two_w), prev, nxt] + ([blk_spec(width)] if final else []) + [
        _const_spec(cw.shape), _const_spec(cb.shape), _const_spec(wg.shape), _const_spec(bg.shape),
        _const_spec(lam.shape)]
    return pl.pallas_call(
        functools.partial(_lru_kernel, rev, final, tl),
        grid=(nb, nblk),
        in_specs=specs,
        out_specs=blk_spec(width),
        out_shape=jax.ShapeDtypeStruct((nb, seq_len, width), F32),
        scratch_shapes=[pltpu.VMEM((SUBLANES, width), F32),
                        pltpu.VMEM((tl + 2 * CONV_HALO, width), F32),
                        pltpu.VMEM((width // LANES, SUBLANES * pitch, LANES), F32),
                        pltpu.VMEM((width // LANES, SUBLANES * pitch, LANES), F32)],
        compiler_params=_params("parallel", "arbitrary"),
        name="lru_bwd" if rev else "lru_fwd",
    )(*ins)


def _gla_kernel(tl, heads, dk, dv, mf_ref, mb_ref, sf_ref, sb_ref, wup_ref, bup_ref, cumf_ref, cumb_ref,
                of_ref, ob_ref, st_ref):
    i = pl.program_id(1)
    hk = heads * dk
    nchunk = tl // CHUNK

    @pl.when(i == 0)
    def _():
        st_ref[...] = jnp.zeros_like(st_ref)

    hv = heads * dv
    kk_blocks = _block_mask(heads * CHUNK, hk, CHUNK, dk).astype(MXU_DTYPE)
    kv_blocks = _block_mask(heads * CHUNK, hv, CHUNK, dv).astype(MXU_DTYPE)
    st_blocks = _block_mask(hv, hk, dv, dk).astype(F32)
    dirs = ((False, mf_ref, sf_ref, cumf_ref, of_ref), (True, mb_ref, sb_ref, cumb_ref, ob_ref))
    for d, (rev, m_ref, s_ref, cum_ref, o_ref) in enumerate(dirs):
        incl = _tri_mask_heads(rev, False, heads)
        end = 0 if rev else CHUNK - 1
        pre = _mm(s_ref[...], wup_ref[d]) + bup_ref[d]
        la = (jnp.minimum(pre, 0.0) - jnp.log1p(jnp.exp(-jnp.abs(pre)))) * (1.0 / GLA_TAU)
        b = _cum_dot(cum_ref[...], la)
        q_dec = (m_ref[:, 0:hk] * (dk ** -0.5) * jnp.exp(b)).astype(MXU_DTYPE)
        k = m_ref[:, hk:2 * hk]
        k_inv = (k * jnp.exp(-b)).astype(MXU_DTYPE)
        state = st_ref[d]
        for c in _chunk_order(rev, nchunk):
            rows = slice(c * CHUNK, (c + 1) * CHUNK)
            b_end = b[c * CHUNK + end:c * CHUNK + end + 1, :]
            k_end = k[rows, :] * jnp.exp(b_end - b[rows, :])
            v = m_ref[rows, 2 * hk:2 * hk + hv].astype(MXU_DTYPE)
            k_exp = jnp.concatenate([k_inv[rows, :]] * heads, axis=0) * kk_blocks
            v_exp = jnp.concatenate([v] * heads, axis=0) * kv_blocks
            scores = jnp.where(incl, _mm_nt(q_dec[rows, :], k_exp), 0.0)
            o_ref[rows, :] = _mm(scores, v_exp) + _mm_nt(q_dec[rows, :], state)
            state = state * jnp.exp(b_end) + _mm_tn(v, k_end) * st_blocks
        st_ref[d] = state


def _gla_both(proj, small, wup, bup, heads, dk, dv):
    nb, seq_len, pw = proj.shape
    tl = TL_MIX
    nblk = seq_len // tl
    fwd = lambda wd: pl.BlockSpec((None, tl, wd), lambda b, i: (b, i, 0))
    bwd = lambda wd: pl.BlockSpec((None, tl, wd), lambda b, i: (b, nblk - 1 - i, 0))
    hv = heads * dv
    sw = small.shape[2]
    cum_f, cum_b = _block_cum(False, tl), _block_cum(True, tl)
    out = jax.ShapeDtypeStruct((nb, seq_len, hv), F32)
    return pl.pallas_call(
        functools.partial(_gla_kernel, tl, heads, dk, dv),
        grid=(nb, nblk),
        in_specs=[fwd(pw), bwd(pw), fwd(sw), bwd(sw), _const_spec(wup.shape), _const_spec(bup.shape),
                  _const_spec(cum_f.shape), _const_spec(cum_b.shape)],
        out_specs=[fwd(hv), bwd(hv)],
        out_shape=[out, out],
        scratch_shapes=[pltpu.VMEM((2, heads * dv, heads * dk), F32)],
        compiler_params=_params("parallel", "arbitrary"),
        name="gla",
    )(proj, proj, small, small, wup, bup, cum_f, cum_b)


def _dn_gates(small, gc_ref):
    beta = _sigmoid(small)
    g = -jnp.exp(gc_ref[0:1, :]) * _softplus(small + gc_ref[1:2, :])
    return jnp.where(gc_ref[2:3, :] > 0.5, beta, jnp.where(gc_ref[3:4, :] > 0.5, g, 0.0))


def _dn_forms(gx, d, heads):
    assert 2 * heads == SUBLANES and 2 * CHUNK == LANES
    base = 2 * heads * d
    rows8 = gx.T[base:base + SUBLANES, :]
    rows8_rot = pltpu.roll(rows8, CHUNK, axis=1)
    rep = lambda lane: jnp.broadcast_to(gx[:, lane:lane + 1], (gx.shape[0], LANES))
    return rows8, rows8_rot, [rep(base + h) for h in range(heads)], [rep(base + heads + h) for h in range(heads)]


def _head_rows(rows8, rows8_rot, r0, c, heads):
    nchunk = rows8.shape[1] // CHUNK
    low_half = lax.broadcasted_iota(jnp.int32, (1, LANES), 1) < CHUNK
    out = []
    for j in range(heads // 2):
        halves = []
        for h in (2 * j, 2 * j + 1):
            src, slot = (rows8, c) if c % 2 == h % 2 else (rows8_rot, (c + 1) % nchunk)
            halves.append(src[r0 + h:r0 + h + 1, LANES * (slot // 2):LANES * (slot // 2 + 1)])
        out.append(jnp.where(low_half, halves[0], halves[1]))
    return jnp.concatenate(out, axis=1)


def _head_cols(rep, c, heads):
    low_half = lax.broadcasted_iota(jnp.int32, (1, LANES), 1) < CHUNK
    rows = slice(c * CHUNK, (c + 1) * CHUNK)
    return jnp.concatenate([jnp.where(low_half, rep[2 * j][rows, :], rep[2 * j + 1][rows, :])
                            for j in range(heads // 2)], axis=1)


def _dn_gamma(rev, forms, c, heads):
    rows8, rows8_rot, _, g_rep = forms
    incl = _tri_mask_heads(rev, False, heads)
    diff = _head_cols(g_rep, c, heads) - _head_rows(rows8, rows8_rot, heads, c, heads)
    return jnp.where(incl, jnp.exp(jnp.where(incl, diff, 0.0)), 0.0)


def _head_expand(x, blocks, heads):
    return jnp.concatenate([x.astype(MXU_DTYPE)] * heads, axis=0) * blocks


def _dn_prep_kernel(tl, heads, dh, main_ref, prev_ref, next_ref, small_ref, cw_ref, gcst_ref, cumf_ref, cumb_ref,
                    qkv_ref, gx_ref, af_ref, ab_ref, ext_ref):
    i = pl.program_id(1)
    nblk = pl.num_programs(1)
    hw = heads * dh
    x = main_ref[:, 0:3 * hw]
    y = _silu(_centred_conv(ext_ref, x, prev_ref[...], next_ref[...], i == 0, i == nblk - 1, cw_ref))
    for h in range(heads):
        for part, scale in ((0, dh ** -0.5), (1, 1.0)):
            sl = slice(part * hw + h * dh, part * hw + (h + 1) * dh)
            z = y[:, sl]
            qkv_ref[:, sl] = z * lax.rsqrt(jnp.sum(z * z, axis=-1, keepdims=True) + NORM_EPS) * scale
    qkv_ref[:, 2 * hw:] = y[:, 2 * hw:]
    bg = _dn_gates(small_ref[...], gcst_ref)
    lane = lax.broadcasted_iota(jnp.int32, bg.shape, 1)
    run = jnp.where(lane < 2 * heads, _cum_dot(cumf_ref[...], bg), _cum_dot(cumb_ref[...], bg))
    gx = jnp.where(gcst_ref[3:4, :] > 0.5, run, bg)
    gx_ref[...] = gx
    blocks = _block_mask(heads * CHUNK, hw, CHUNK, dh).astype(MXU_DTYPE)
    for d, (rev, a_ref) in enumerate(((False, af_ref), (True, ab_ref))):
        strict = _tri_mask_heads(rev, True, heads)
        forms = _dn_forms(gx, d, heads)
        for c in range(tl // CHUNK):
            k = qkv_ref[c * CHUNK:(c + 1) * CHUNK, hw:2 * hw]
            kk = _mm_nt(k, _head_expand(k, blocks, heads))
            a_ref[c] = jnp.where(strict, kk * _head_cols(forms[2], c, heads) * _dn_gamma(rev, forms, c, heads), 0.0)


def _dn_prep(proj, small, cw, gate_consts, heads, dh):
    nb, seq_len, pw = proj.shape
    tl = TL_MIX
    nblk = seq_len // tl
    hw = heads * dh
    blk_spec = lambda wd: pl.BlockSpec((None, tl, wd), lambda b, i: (b, i, 0))
    prev, nxt = _halo_specs(False, tl, 3 * hw, seq_len)
    a_spec = pl.BlockSpec((None, tl // CHUNK, CHUNK, heads * CHUNK), lambda b, i: (b, i, 0, 0))
    a_shape = jax.ShapeDtypeStruct((nb, seq_len // CHUNK, CHUNK, heads * CHUNK), F32)
    lanes_shape = jax.ShapeDtypeStruct((nb, seq_len, LANES), F32)
    cum_f, cum_b = _block_cum(False, tl), _block_cum(True, tl)
    return pl.pallas_call(
        functools.partial(_dn_prep_kernel, tl, heads, dh),
        grid=(nb, nblk),
        in_specs=[blk_spec(pw), prev, nxt, blk_spec(small.shape[2]), _const_spec(cw.shape),
                  _const_spec(gate_consts.shape), _const_spec(cum_f.shape), _const_spec(cum_b.shape)],
        out_specs=[blk_spec(3 * hw), blk_spec(LANES), a_spec, a_spec],
        out_shape=[jax.ShapeDtypeStruct((nb, seq_len, 3 * hw), F32), lanes_shape, a_shape, a_shape],
        scratch_shapes=[pltpu.VMEM((tl + 2 * CONV_HALO, 3 * hw), F32)],
        compiler_params=_params("parallel", "parallel"),
        name="dn_prep",
    )(proj, proj, proj, small, cw, gate_consts, cum_f, cum_b)


def _dn_solve_kernel(upper, a_ref, t_ref):
    half = CHUNK // 2
    lanes = t_ref.shape[2]
    col = lax.broadcasted_iota(jnp.int32, (half, lanes), 0)

    def row(idx, carry):
        t = (CHUNK - 1 - idx) if upper else idx

        def term_lo(s, acc):
            return acc - a_ref[t, pl.ds(s, 1), :] * t_ref[s, 0:half, :]

        def term_hi(s, acc):
            return acc - a_ref[t, pl.ds(s, 1), :] * t_ref[s, half:, :]

        if upper:
            lo_range, hi_range = (t + 1, jnp.maximum(t + 1, half)), (t + 1, CHUNK)
        else:
            lo_range, hi_range = (0, t), (half, jnp.maximum(t, half))
        t_ref[t, 0:half, :] = lax.fori_loop(*lo_range, term_lo, jnp.where(col == t, 1.0, 0.0))
        t_ref[t, half:, :] = lax.fori_loop(*hi_range, term_hi, jnp.where(col + half == t, 1.0, 0.0))
        return carry

    lax.fori_loop(0, CHUNK, row, 0)


def _dn_solve(a_t, upper):
    n = a_t.shape[2]
    lanes = min(SOLVE_LANES, n)
    spec = pl.BlockSpec((CHUNK, CHUNK, lanes), lambda i: (0, 0, i))
    return pl.pallas_call(
        functools.partial(_dn_solve_kernel, upper),
        grid=(n // lanes,),
        in_specs=[spec],
        out_specs=spec,
        out_shape=jax.ShapeDtypeStruct(a_t.shape, F32),
        compiler_params=_params("parallel"),
        name="dn_solve_bwd" if upper else "dn_solve_fwd",
    )(a_t)


def _dn_main_kernel(tl, heads, dh, qf_ref, gxf_ref, tf_ref, qb_ref, gxb_ref, tb_ref, of_ref, ob_ref, s_ref):
    i = pl.program_id(1)
    hw = heads * dh
    nchunk = tl // CHUNK
    npair = heads // 2
    pw = 2 * dh

    @pl.when(i == 0)
    def _():
        s_ref[...] = jnp.zeros_like(s_ref)

    blocks = _block_mask(heads * CHUNK, hw, CHUNK, dh).astype(MXU_DTYPE)
    t_blocks = _block_mask(heads * CHUNK, heads * CHUNK, CHUNK, CHUNK).astype(MXU_DTYPE)
    pair_blocks = _block_mask(pw, pw, dh, dh).astype(F32)
    stack = lambda x: jnp.concatenate([x[:, h * dh:(h + 1) * dh] for h in range(heads)], axis=0)
    dirs = ((False, qf_ref, gxf_ref, tf_ref, of_ref), (True, qb_ref, gxb_ref, tb_ref, ob_ref))
    for d, (rev, qkv_ref, gx_ref, t_ref, o_ref) in enumerate(dirs):
        incl = _tri_mask_heads(rev, False, heads)
        end = 0 if rev else CHUNK - 1
        forms = _dn_forms(gx_ref[...], d, heads)
        b_nat = jnp.concatenate(forms[2], axis=1)
        g_nat = jnp.concatenate(forms[3], axis=1)
        eg_nat = jnp.exp(g_nat)
        states = [s_ref[d, j] for j in range(npair)]
        for c in _chunk_order(rev, nchunk):
            rows = slice(c * CHUNK, (c + 1) * CHUNK)
            q = qkv_ref[rows, 0:hw]
            k = qkv_ref[rows, hw:2 * hw]
            v = qkv_ref[rows, 2 * hw:3 * hw]
            eg = eg_nat[rows, :]
            attn = jnp.where(incl, _mm_nt(q, _head_expand(k, blocks, heads)) * _dn_gamma(rev, forms, c, heads), 0.0)
            beta = b_nat[rows, :]
            t_bd = _head_expand(t_ref[c], t_blocks, heads)
            wu = _mm(t_bd, jnp.concatenate([stack(k * beta * eg), stack(v * beta)], axis=1))
            w = jnp.concatenate([wu[h * CHUNK:(h + 1) * CHUNK, 0:dh] for h in range(heads)], axis=1)
            u = jnp.concatenate([wu[h * CHUNK:(h + 1) * CHUNK, dh:] for h in range(heads)], axis=1)
            lhs = jnp.concatenate([w, q * eg], axis=0)
            xs = [_mm(lhs[:, j * pw:(j + 1) * pw], states[j]) for j in range(npair)]
            v_new = u - jnp.concatenate([x[0:CHUNK, :] for x in xs], axis=1)
            o_ref[rows, :] = (jnp.concatenate([x[CHUNK:, :] for x in xs], axis=1)
                              + _mm(attn, _head_expand(v_new, blocks, heads)))
            g_end = g_nat[c * CHUNK + end:c * CHUNK + end + 1, :]
            k_end = k * jnp.exp(g_end - g_nat[rows, :])
            e_end = jnp.exp(g_end)
            for j in range(npair):
                ps = slice(j * pw, (j + 1) * pw)
                states[j] = states[j] * e_end[:, ps] + _mm_tn(k_end[:, ps], v_new[:, ps]) * pair_blocks
        for j in range(npair):
            s_ref[d, j] = states[j]


def _dn_main(qkv, gx, t_f, t_b, heads, dh):
    nb, seq_len, _ = qkv.shape
    tl = TL_MIX
    nblk = seq_len // tl
    hw = heads * dh
    fwd = lambda wd: pl.BlockSpec((None, tl, wd), lambda b, i: (b, i, 0))
    bwd = lambda wd: pl.BlockSpec((None, tl, wd), lambda b, i: (b, nblk - 1 - i, 0))
    t_blk = (None, tl // CHUNK, CHUNK, heads * CHUNK)
    t_fwd = pl.BlockSpec(t_blk, lambda b, i: (b, i, 0, 0))
    t_bwd = pl.BlockSpec(t_blk, lambda b, i: (b, nblk - 1 - i, 0, 0))
    out = jax.ShapeDtypeStruct((nb, seq_len, hw), F32)
    return pl.pallas_call(
        functools.partial(_dn_main_kernel, tl, heads, dh),
        grid=(nb, nblk),
        in_specs=[fwd(3 * hw), fwd(LANES), t_fwd, bwd(3 * hw), bwd(LANES), t_bwd],
        out_specs=[fwd(hw), bwd(hw)],
        out_shape=[out, out],
        scratch_shapes=[pltpu.VMEM((2, heads // 2, 2 * dh, 2 * dh), F32)],
        compiler_params=_params("parallel", "arbitrary"),
        name="dn_main",
    )(qkv, gx, t_f, qkv, gx, t_b)


def _dn_branch(proj, small, cw, gate_consts, heads, dh):
    nb, seq_len, _ = proj.shape
    nc = seq_len // CHUNK
    qkv, gx, a_f, a_b = _dn_prep(proj, small, cw, gate_consts, heads, dh)
    n_inst = nb * nc * heads
    to_lanes = lambda a: a.reshape(nb, nc, CHUNK, heads, CHUNK).transpose(2, 4, 0, 1, 3).reshape(CHUNK, CHUNK, n_inst)
    from_lanes = lambda t: (t.reshape(CHUNK, CHUNK, nb, nc, heads).transpose(2, 3, 0, 4, 1)
                            .reshape(nb, nc, CHUNK, heads * CHUNK).astype(MXU_DTYPE))
    t_f = from_lanes(_dn_solve(to_lanes(a_f), upper=False))
    t_b = from_lanes(_dn_solve(to_lanes(a_b), upper=True))
    return _dn_main(qkv, gx, t_f, t_b, heads, dh)


def _s5_prep_kernel(seg, lre_ref, lim_ref, ldt_ref, bre_ref, bim_ref, a_ref, pw_ref, bbre_ref, bbim_ref):
    lam_re = lre_ref[...]
    lam_im = lim_ref[...]
    dt = jnp.exp(ldt_ref[...])
    mag = jnp.exp(lam_re * dt)
    a_re = mag * jnp.cos(lam_im * dt)
    a_im = mag * jnp.sin(lam_im * dt)
    den = lam_re * lam_re + lam_im * lam_im
    n_re = a_re - 1.0
    f_re = (n_re * lam_re + a_im * lam_im) / den
    f_im = (a_im * lam_re - n_re * lam_im) / den
    a_ref[0] = a_re
    a_ref[1] = a_im
    for i in range(bre_ref.shape[0]):
        bbre_ref[i] = f_re * bre_ref[i] - f_im * bim_ref[i]
        bbim_ref[i] = f_re * bim_ref[i] + f_im * bre_ref[i]
    for k in range(seg):
        m = jnp.exp(lam_re * dt * float(k + 1))
        pw_ref[0, k] = m * jnp.cos(lam_im * dt * float(k + 1))
        pw_ref[1, k] = m * jnp.sin(lam_im * dt * float(k + 1))


def _s5_prep(lre, lim, ldt, bre, bim, seg):
    npair, gw = lre.shape[0], bre.shape[0]
    return pl.pallas_call(
        functools.partial(_s5_prep_kernel, seg),
        out_shape=[jax.ShapeDtypeStruct((2, npair, LANES), F32),
                   jax.ShapeDtypeStruct((2, seg, npair, LANES), F32),
                   jax.ShapeDtypeStruct((gw, npair, LANES), F32),
                   jax.ShapeDtypeStruct((gw, npair, LANES), F32)],
        name="s5_prep",
    )(lre, lim, ldt, bre, bim)


def _gelu_tanh(x):
    return 0.5 * x * (1.0 + jnp.tanh(math.sqrt(2.0 / math.pi) * (x + 0.044715 * (x * x * x))))


def _s5_kernel(rev, final, tl, *refs):
    if final:
        (main_ref, yf_ref, a_ref, pw_ref, bm_ref, cm_ref, dsk_ref, wglu_ref, bglu_ref,
         o_ref, carry_ref, re_scr, im_scr, st_scr) = refs
    else:
        (main_ref, a_ref, pw_ref, bm_ref, cm_ref, o_ref, carry_ref, re_scr, im_scr, st_scr) = refs
    i = pl.program_id(1)
    width = o_ref.shape[1]
    ntile = bm_ref.shape[0]
    tw = width // ntile
    sw = bm_ref.shape[2] // 2
    npt = sw // LANES
    seg = tl // SUBLANES
    pitch = seg + SEG_PAD

    @pl.when(i == 0)
    def _():
        carry_ref[...] = jnp.zeros_like(carry_ref)

    u = main_ref[:, 0:width]
    for gt in range(ntile):
        bu = _mm(u[:, gt * tw:(gt + 1) * tw], bm_ref[gt])
        for j in range(npt):
            p = gt * npt + j
            for s in range(SUBLANES):
                re_scr[p, s * pitch:s * pitch + seg, :] = bu[s * seg:(s + 1) * seg, j * LANES:(j + 1) * LANES]
                im_scr[p, s * pitch:s * pitch + seg, :] = bu[s * seg:(s + 1) * seg, sw + j * LANES:sw + (j + 1) * LANES]

    order = range(seg - 1, -1, -1) if rev else range(seg)
    npair = ntile * npt

    def local_scan(it, carry):
        ps = [it * S5_SCAN_UNROLL + j for j in range(S5_SCAN_UNROLL)]
        ar = [jnp.broadcast_to(a_ref[0, pl.ds(p, 1), :], (SUBLANES, LANES)) for p in ps]
        ai = [jnp.broadcast_to(a_ref[1, pl.ds(p, 1), :], (SUBLANES, LANES)) for p in ps]
        sr = [jnp.zeros((SUBLANES, LANES), F32) for _ in ps]
        si = [jnp.zeros((SUBLANES, LANES), F32) for _ in ps]
        for k in order:
            for j, p in enumerate(ps):
                br = re_scr[p, pl.ds(k, SUBLANES, stride=pitch), :]
                bi = im_scr[p, pl.ds(k, SUBLANES, stride=pitch), :]
                sr[j], si[j] = ar[j] * sr[j] - ai[j] * si[j] + br, ar[j] * si[j] + ai[j] * sr[j] + bi
                re_scr[p, pl.ds(k, SUBLANES, stride=pitch), :] = sr[j]
                im_scr[p, pl.ds(k, SUBLANES, stride=pitch), :] = si[j]
        return carry

    lax.fori_loop(0, npair // S5_SCAN_UNROLL, local_scan, 0)

    last = 0 if rev else seg - 1
    seg_order = range(SUBLANES - 1, -1, -1) if rev else range(SUBLANES)
    for gt in range(ntile):
        for j in range(npt):
            p = gt * npt + j
            pr = pw_ref[0, p]
            pi = pw_ref[1, p]
            fr = pr[last:last + 1, :]
            fi = pi[last:last + 1, :]
            cr = carry_ref[0:1, p * LANES:(p + 1) * LANES]
            ci = carry_ref[1:2, p * LANES:(p + 1) * LANES]
            for s in seg_order:
                rows = slice(s * seg, (s + 1) * seg)
                st_scr[rows, j * LANES:(j + 1) * LANES] = (
                    re_scr[p, s * pitch:s * pitch + seg, :] + pr * cr - pi * ci).astype(st_scr.dtype)
                st_scr[rows, sw + j * LANES:sw + (j + 1) * LANES] = (
                    im_scr[p, s * pitch:s * pitch + seg, :] + pr * ci + pi * cr).astype(st_scr.dtype)
                er = re_scr[p, pl.ds(s * pitch + last, 1), :]
                ei = im_scr[p, pl.ds(s * pitch + last, 1), :]
                cr, ci = fr * cr - fi * ci + er, fr * ci + fi * cr + ei
            carry_ref[0:1, p * LANES:(p + 1) * LANES] = cr
            carry_ref[1:2, p * LANES:(p + 1) * LANES] = ci
        y = jnp.dot(st_scr[...], cm_ref[gt], preferred_element_type=F32)
        o_ref[:, gt * tw:(gt + 1) * tw] = y

    if final:
        y = o_ref[...] + yf_ref[...] + dsk_ref[...] * u
        z = _gelu_tanh(y)
        z = z * _sigmoid(_mm(z, wglu_ref[...]) + bglu_ref[...])
        o_ref[...] = z * _silu(main_ref[:, width:2 * width])


def _s5_dir(rev, proj, yf, a_vec, pw, bm, cm, dsk, wglu, bglu):
    nb, seq_len, two_w = proj.shape
    width = two_w // 2
    tl = TL_MIX
    nblk = seq_len // tl
    pos = _seq_pos(rev)
    blk_spec = lambda wd: pl.BlockSpec((None, tl, wd), lambda b, i: (b, pos(i, nblk), 0))
    final = yf is not None
    seg = tl // SUBLANES
    pitch = seg + SEG_PAD
    npair = a_vec.shape[1]
    sw2 = bm.shape[2]
    ins = [proj] + ([yf] if final else []) + [a_vec, pw, bm, cm] + ([dsk, wglu, bglu] if final else [])
    specs = [blk_spec(two_w)] + ([blk_spec(width)] if final else []) + [
        _const_spec(a_vec.shape), _const_spec(pw.shape), _const_spec(bm.shape), _const_spec(cm.shape)] + (
        [_const_spec(dsk.shape), _const_spec(wglu.shape), _const_spec(bglu.shape)] if final else [])
    return pl.pallas_call(
        functools.partial(_s5_kernel, rev, final, tl),
        grid=(nb, nblk),
        in_specs=specs,
        out_specs=blk_spec(width),
        out_shape=jax.ShapeDtypeStruct((nb, seq_len, width), F32),
        scratch_shapes=[pltpu.VMEM((SUBLANES, npair * LANES), F32),
                        pltpu.VMEM((npair, SUBLANES * pitch, LANES), F32),
                        pltpu.VMEM((npair, SUBLANES * pitch, LANES), F32),
                        pltpu.VMEM((tl, sw2), MXU_DTYPE)],
        compiler_params=_params("parallel", "arbitrary"),
        name="s5_bwd" if rev else "s5_fwd",
    )(*ins)


def _s5_branch(proj, lam_re, lam_im, log_dt, b_re, b_im, c_re, c_im, dsk, wglu, bglu):
    _, groups, states = lam_re.shape
    gw = b_re.shape[3]
    gpt = 2 * LANES // gw
    ntile = groups // gpt
    npair = groups * states // LANES
    seg = TL_MIX // SUBLANES
    eye = jnp.eye(gpt, dtype=F32)
    y = None
    for d, rev in enumerate((False, True)):
        lre = lam_re[d].reshape(npair, LANES)
        lim = lam_im[d].reshape(npair, LANES)
        ldt = jnp.broadcast_to(log_dt[d][:, None], (groups, states)).reshape(npair, LANES)
        bre = b_re[d].transpose(2, 0, 1).reshape(gw, npair, LANES)
        bim = b_im[d].transpose(2, 0, 1).reshape(gw, npair, LANES)
        a_vec, pw, bb_re, bb_im = _s5_prep(lre, lim, ldt, bre, bim, seg)
        pw = pw.transpose(0, 2, 1, 3)
        if rev:
            pw = jnp.flip(pw, axis=2)
        bbr = bb_re.reshape(gw, ntile, gpt, states)
        bbi = bb_im.reshape(gw, ntile, gpt, states)
        bm = jnp.concatenate(
            [jnp.einsum('intp,tu->ntiup', m, eye).reshape(ntile, gpt * gw, gpt * states) for m in (bbr, bbi)],
            axis=2).astype(MXU_DTYPE)
        cr = c_re[d].reshape(ntile, gpt, gw, states)
        ci = c_im[d].reshape(ntile, gpt, gw, states)
        cm = jnp.concatenate(
            [jnp.einsum('ntjp,tu->ntpuj', m, eye).reshape(ntile, gpt * states, gpt * gw) for m in (cr, -ci)],
            axis=1).astype(MXU_DTYPE)
        if rev:
            y = _s5_dir(True, proj, y, a_vec, pw, bm, cm, dsk, wglu, bglu)
        else:
            y = _s5_dir(False, proj, None, a_vec, pw, bm, cm, None, None, None)
    return y


def _layer(x, p, final_g, final):
    nb, seq_len, d = x.shape
    bw = d // 2
    x2 = x.reshape(nb * seq_len, d)
    bf = lambda a: a.astype(MXU_DTYPE)
    row = lambda a: a.reshape(1, -1)

    lru_blocks = p['lru_w_a'].shape[1]
    gla_heads = GLA_HEADS
    gla_dk = p['gla_w_up'].shape[2] // gla_heads
    gla_dv = bw // gla_heads
    gla_rank = p['gla_w_up'].shape[1]
    dn_heads = p['dn_a_log'].shape[1]
    dn_dh = bw // dn_heads
    widths = (bw, bw, gla_heads * gla_dk, gla_heads * gla_dk, bw, bw, 2 * gla_rank, 3 * bw, bw, 4 * dn_heads, bw, bw)
    offs = [0]
    for wd in widths:
        offs.append(offs[-1] + wd)
    w_in = p['w_in']
    cols = lambda a, b: w_in[:, offs[a]:offs[b]]
    n_gate = 4 * dn_heads
    n_small = n_gate + 2 * gla_rank
    w_small = jnp.concatenate([cols(9, 10), cols(6, 7), jnp.zeros((d, LANES - n_small), w_in.dtype)], axis=1)
    w_cat = bf(jnp.concatenate([cols(0, 2), cols(2, 6), cols(7, 9), cols(10, 12), w_small], axis=1))
    out_w = (2 * bw, offs[6] - offs[2], 4 * bw, 2 * bw, LANES)
    lru_p, gla_p, dn_p, s5_p, small = _in_proj(x2, row(p['norm_g']), w_cat, out_w)
    seq = lambda a: a.reshape(nb, seq_len, a.shape[1])
    flat = lambda a: a.reshape(nb * seq_len, a.shape[2])
    lru_s, gla_s, dn_s, s5_s, small_s = seq(lru_p), seq(gla_p), seq(dn_p), seq(s5_p), seq(small)

    lanes_per_blk = bw // lru_blocks
    y_lru = None
    for dct, rev in enumerate((False, True)):
        wg = bf(jnp.concatenate([p['lru_w_a'][dct], p['lru_w_x'][dct]], axis=2))
        bg = jnp.concatenate([p['lru_b_a'][dct].reshape(lru_blocks, 1, lanes_per_blk),
                              p['lru_b_x'][dct].reshape(lru_blocks, 1, lanes_per_blk)], axis=2)
        y_lru = _lru_dir(rev, lru_s, y_lru, p['lru_conv_w'], row(p['lru_conv_b']), wg, bg, row(p['lru_lambda'][dct]))

    wup = jnp.zeros((2, LANES, gla_heads * gla_dk), F32)
    for dct in range(2):
        lo = n_gate + dct * gla_rank
        wup = wup.at[dct, lo:lo + gla_rank].set(p['gla_w_up'][dct])
    gla_f, gla_b = _gla_both(gla_s, small_s, bf(wup), p['gla_b_up'].reshape(2, 1, -1), gla_heads, gla_dk, gla_dv)

    alpha_lanes = (jnp.arange(2)[:, None] * 2 * dn_heads + dn_heads + jnp.arange(dn_heads)[None, :]).reshape(-1)
    beta_lanes = alpha_lanes - dn_heads
    gate_consts = jnp.zeros((SUBLANES, LANES), F32)
    gate_consts = gate_consts.at[0, alpha_lanes].set(p['dn_a_log'].reshape(-1))
    gate_consts = gate_consts.at[1, alpha_lanes].set(p['dn_dt_bias'].reshape(-1))
    gate_consts = gate_consts.at[2, beta_lanes].set(1.0).at[3, alpha_lanes].set(1.0)
    dn_f, dn_b = _dn_branch(dn_s, small_s, p['dn_conv_w'], gate_consts, dn_heads, dn_dh)

    y_s5 = _s5_branch(s5_s, p['s5_lambda_re'], p['s5_lambda_im'], p['s5_log_dt'], p['s5_b_re'], p['s5_b_im'],
                      p['s5_c_re'], p['s5_c_im'], row(p['s5_d']), bf(p['s5_w_glu']), row(p['s5_b_glu']))

    gla_gate_col = (offs[5] - offs[2]) // bw
    dn_gate_col = (offs[8] - offs[7]) // bw
    out = _out_proj(x2, flat(y_lru),
                    (flat(gla_f), flat(gla_b), gla_p, gla_gate_col, row(p['gla_norm_g']), gla_heads),
                    (flat(dn_f), flat(dn_b), dn_p, dn_gate_col, row(p['dn_norm_g']), dn_heads),
                    flat(y_s5), row(p['norm_g']),
                    bf(p['w_merge_gate']), p['b_merge_gate'].reshape(4, 1, d), bf(p['w_branch']), bf(p['w_out']),
                    row(final_g), final)
    return out.reshape(nb, seq_len, d)


_LAYER_PARAM_NAMES = (
    'norm_g', 'w_in', 'lru_conv_w', 'lru_conv_b', 'lru_w_a', 'lru_b_a', 'lru_w_x', 'lru_b_x', 'lru_lambda',
    'gla_w_up', 'gla_b_up', 'gla_norm_g', 'dn_conv_w', 'dn_a_log', 'dn_dt_bias', 'dn_norm_g',
    's5_lambda_re', 's5_lambda_im', 's5_log_dt', 's5_b_re', 's5_b_im', 's5_c_re', 's5_c_im', 's5_d',
    's5_w_glu', 's5_b_glu', 'w_branch', 'w_merge_gate', 'b_merge_gate', 'w_out')


def kernel(x_prompt, x_sample, norm_g, w_in, lru_conv_w, lru_conv_b, lru_w_a, lru_b_a, lru_w_x, lru_b_x, lru_lambda, gla_w_up, gla_b_up, gla_norm_g, dn_conv_w, dn_a_log, dn_dt_bias, dn_norm_g, s5_lambda_re, s5_lambda_im, s5_log_dt, s5_b_re, s5_b_im, s5_c_re, s5_c_im, s5_d, s5_w_glu, s5_b_glu, w_branch, w_merge_gate, b_merge_gate, w_out, final_norm_g):
    stacked = (norm_g, w_in, lru_conv_w, lru_conv_b, lru_w_a, lru_b_a, lru_w_x, lru_b_x, lru_lambda,
               gla_w_up, gla_b_up, gla_norm_g, dn_conv_w, dn_a_log, dn_dt_bias, dn_norm_g,
               s5_lambda_re, s5_lambda_im, s5_log_dt, s5_b_re, s5_b_im, s5_c_re, s5_c_im, s5_d,
               s5_w_glu, s5_b_glu, w_branch, w_merge_gate, b_merge_gate, w_out)
    depth = norm_g.shape[0]
    n_prompt = x_prompt.shape[0]
    x = jnp.concatenate([x_prompt, x_sample], axis=0)
    for layer in range(depth):
        p = {name: arr[layer] for name, arr in zip(_LAYER_PARAM_NAMES, stacked)}
        x = _layer(x, p, final_norm_g, layer == depth - 1)
    return x[:n_prompt], x[n_prompt:]
```

```python
import functools
import math

import jax
import jax.numpy as jnp
from jax import lax
from jax.experimental import pallas as pl
from jax.experimental.pallas import tpu as pltpu

F32 = jnp.float32
MXU_DTYPE = jnp.bfloat16
HI = lax.Precision.HIGHEST

NORM_EPS = 1e-6
CONV_W = 4
CONV_HALO = 8
LRU_C = 8.0
GLA_TAU = 16.0
GLA_HEADS = 4
CHUNK = 64
LANES = 128
SUBLANES = 8
SEG_PAD = 4
VMEM_LIMIT = 56 * 1024 * 1024

TM_PROJ = 256
TL_MIX = 256
SOLVE_LANES = 512


def _mm(a, b):
    return jnp.dot(a.astype(MXU_DTYPE), b.astype(MXU_DTYPE), preferred_element_type=F32)


def _mm_nt(a, b):
    return lax.dot_general(a.astype(MXU_DTYPE), b.astype(MXU_DTYPE), (((1,), (1,)), ((), ())),
                           preferred_element_type=F32)


def _mm_tn(a, b):
    return lax.dot_general(a.astype(MXU_DTYPE), b.astype(MXU_DTYPE), (((0,), (0,)), ((), ())),
                           preferred_element_type=F32)


def _cum_dot(cum3, x):
    hi = x.astype(jnp.bfloat16)
    r1 = x - hi.astype(F32)
    mid = r1.astype(jnp.bfloat16)
    lo = (r1 - mid.astype(F32)).astype(jnp.bfloat16)
    return jnp.dot(cum3, jnp.concatenate([hi, mid, lo], axis=0), preferred_element_type=F32)


def _sigmoid(x):
    return 1.0 / (1.0 + jnp.exp(-x))


def _silu(x):
    return x * _sigmoid(x)


def _softplus(x):
    return jnp.maximum(x, 0.0) + jnp.log1p(jnp.exp(-jnp.abs(x)))


def _rms(x, g):
    return x * lax.rsqrt(jnp.mean(x * x, axis=-1, keepdims=True) + NORM_EPS) * g


def _params(*sem):
    return pltpu.CompilerParams(dimension_semantics=sem, vmem_limit_bytes=VMEM_LIMIT)


def _const_spec(shape):
    nd = len(shape)
    return pl.BlockSpec(shape, lambda *_: (0,) * nd)


def _seq_pos(rev):
    if rev:
        return lambda i, n: n - 1 - i
    return lambda i, n: i


def _log2(n):
    assert n > 0 and n & (n - 1) == 0, n
    return n.bit_length() - 1


def _block_mask(nrow, ncol, row_blk, col_blk):
    r = jnp.right_shift(lax.broadcasted_iota(jnp.int32, (nrow, ncol), 0), _log2(row_blk))
    c = jnp.right_shift(lax.broadcasted_iota(jnp.int32, (nrow, ncol), 1), _log2(col_blk))
    return r == c


def _tri_mask_heads(rev, strict, heads):
    t = lax.broadcasted_iota(jnp.int32, (CHUNK, heads * CHUNK), 0)
    s = jnp.bitwise_and(lax.broadcasted_iota(jnp.int32, (CHUNK, heads * CHUNK), 1), CHUNK - 1)
    if rev:
        return (s > t) if strict else (s >= t)
    return (s < t) if strict else (s <= t)


def _block_cum(rev, tl):
    t = jnp.arange(tl)[:, None]
    s = jnp.arange(tl)[None, :]
    same = (t // CHUNK) == (s // CHUNK)
    cum = (same & ((s >= t) if rev else (s <= t))).astype(jnp.bfloat16)
    return jnp.concatenate([cum, cum, cum], axis=1)


def _chunk_order(rev, nchunk):
    return range(nchunk - 1, -1, -1) if rev else range(nchunk)


def _row_parts_specs(parts, width):
    specs, starts, step0 = [], [], 0
    for _, row0, rows in parts:
        steps, blk0 = rows // TM_PROJ, row0 // TM_PROJ
        specs.append(pl.BlockSpec(
            (TM_PROJ, width), lambda i, s=step0, n=steps, b=blk0: (b + jnp.clip(i - s, 0, n - 1), 0)))
        starts.append(step0)
        step0 += steps
    return specs, starts


def _whole(arr):
    return (arr, 0, arr.shape[0])


def _row_parts_value(refs, starts):
    i = pl.program_id(0)
    x = refs[0][...]
    for ref, s in zip(refs[1:], starts[1:]):
        x = jnp.where(i >= s, ref[...], x)
    return x


def _in_proj_kernel(widths, starts, *refs):
    nparts = len(starts)
    g_ref, w_ref = refs[nparts:nparts + 2]
    o_refs = refs[nparts + 2:]
    xn = _rms(_row_parts_value(refs[:nparts], starts), g_ref[...]).astype(MXU_DTYPE)
    off = 0
    for o_ref, wd in zip(o_refs, widths):
        o_ref[...] = jnp.dot(xn, w_ref[:, off:off + wd], preferred_element_type=F32)
        off += wd


def _in_proj(x_parts, g, w_cat, widths):
    d = x_parts[0].shape[1]
    t = sum(a.shape[0] for a in x_parts)
    n = w_cat.shape[1]
    x_specs, starts = _row_parts_specs([_whole(a) for a in x_parts], d)
    return pl.pallas_call(
        functools.partial(_in_proj_kernel, widths, tuple(starts)),
        grid=(t // TM_PROJ,),
        in_specs=x_specs + [_const_spec((1, d)), _const_spec((d, n))],
        out_specs=[pl.BlockSpec((TM_PROJ, wd), lambda i: (i, 0)) for wd in widths],
        out_shape=[jax.ShapeDtypeStruct((t, wd), F32) for wd in widths],
        compiler_params=_params("parallel"),
        name="in_proj",
    )(*x_parts, g, w_cat)


def _out_proj_kernel(final, gla_heads, dn_heads, starts, *refs):
    nparts = len(starts)
    (ylru_ref, gof_ref, gob_ref, ggate_ref, dof_ref, dob_ref, dgate_ref, ys5_ref, g_ref, gng_ref, dng_ref,
     wmg_ref, bmg_ref, wbr_ref, wout_ref, fg_ref, o_ref) = refs[nparts:]

    def finish(of_ref, ob_ref, gate_ref, ng_ref, heads):
        o = of_ref[...] + ob_ref[...]
        dv = o.shape[1] // heads
        normed = [_rms(o[:, h * dv:(h + 1) * dv], ng_ref[...]) for h in range(heads)]
        return jnp.concatenate(normed, axis=1) * _silu(gate_ref[...])

    x = _row_parts_value(refs[:nparts], starts)
    xn = _rms(x, g_ref[...]).astype(MXU_DTYPE)
    ys = (ylru_ref[...], finish(gof_ref, gob_ref, ggate_ref, gng_ref, gla_heads),
          finish(dof_ref, dob_ref, dgate_ref, dng_ref, dn_heads), ys5_ref[...])
    merged = None
    for n, y in enumerate(ys):
        gate = _sigmoid(jnp.dot(xn, wmg_ref[n], preferred_element_type=F32) + bmg_ref[n])
        term = gate * _mm(y, wbr_ref[n])
        merged = term if merged is None else merged + term
    out = x + _mm(merged, wout_ref[...])
    if final:
        out = _rms(out, fg_ref[...])
    o_ref[...] = out


def _out_proj(x_parts, row0, y_lru, gla, dn, y_s5, g, wmg, bmg, wbr, wout, fg, final):
    d = x_parts[0][0].shape[1]
    t = sum(rows for _, _, rows in x_parts)
    bw = y_lru.shape[1]
    blk0 = row0 // TM_PROJ
    row = lambda wd, col=0: pl.BlockSpec((TM_PROJ, wd), lambda i: (blk0 + i, col))
    g_of, g_ob, g_proj, g_col, g_ng, g_heads = gla
    d_of, d_ob, d_proj, d_col, d_ng, d_heads = dn
    x_specs, starts = _row_parts_specs(x_parts, d)
    return pl.pallas_call(
        functools.partial(_out_proj_kernel, final, g_heads, d_heads, tuple(starts)),
        grid=(t // TM_PROJ,),
        in_specs=x_specs + [row(bw), row(bw), row(bw), row(bw, g_col), row(bw), row(bw), row(bw, d_col), row(bw),
                            _const_spec((1, d)), _const_spec(g_ng.shape), _const_spec(d_ng.shape),
                            _const_spec(wmg.shape), _const_spec(bmg.shape), _const_spec(wbr.shape),
                            _const_spec(wout.shape), _const_spec((1, d))],
        out_specs=pl.BlockSpec((TM_PROJ, d), lambda i: (i, 0)),
        out_shape=jax.ShapeDtypeStruct((t, d), F32),
        compiler_params=_params("parallel"),
        name="out_proj",
    )(*[a for a, _, _ in x_parts], y_lru, g_of, g_ob, g_proj, d_of, d_ob, d_proj, y_s5, g, g_ng, d_ng,
      wmg, bmg, wbr, wout, fg)


def _halo_specs(rev, tl, width, seq_len):
    pos = _seq_pos(rev)
    per = tl // CONV_HALO
    last = seq_len // CONV_HALO - 1
    nblk = seq_len // tl
    prev = pl.BlockSpec((None, CONV_HALO, width),
                        lambda b, i: (b, jnp.maximum(pos(i, nblk) * per - 1, 0), 0))
    nxt = pl.BlockSpec((None, CONV_HALO, width),
                       lambda b, i: (b, jnp.minimum((pos(i, nblk) + 1) * per, last), 0))
    return prev, nxt


def _centred_conv(ext_ref, u, prev, nxt, at_start, at_end, w_ref):
    tl = u.shape[0]
    ext_ref[0:CONV_HALO, :] = jnp.where(at_start, 0.0, prev)
    ext_ref[CONV_HALO:CONV_HALO + tl, :] = u
    ext_ref[CONV_HALO + tl:, :] = jnp.where(at_end, 0.0, nxt)
    left = CONV_W // 2
    acc = None
    for j in range(CONV_W):
        term = ext_ref[pl.ds(CONV_HALO + j - left, tl), :] * w_ref[j:j + 1, :]
        acc = term if acc is None else acc + term
    return acc


def _lru_kernel(rev, final, tl, *refs):
    if final:
        (main_ref, prev_ref, next_ref, hf_ref, cw_ref, cb_ref, wg_ref, bg_ref, lam_ref,
         o_ref, carry_ref, ext_ref, a_scr, d_scr) = refs
    else:
        (main_ref, prev_ref, next_ref, cw_ref, cb_ref, wg_ref, bg_ref, lam_ref,
         o_ref, carry_ref, ext_ref, a_scr, d_scr) = refs
    i = pl.program_id(1)
    nblk = pl.num_programs(1)
    blk = _seq_pos(rev)(i, nblk)
    width = o_ref.shape[1]
    nslab = width // LANES
    seg = tl // SUBLANES
    pitch = seg + SEG_PAD

    @pl.when(i == 0)
    def _():
        carry_ref[...] = jnp.zeros_like(carry_ref)

    u = main_ref[:, 0:width]
    xc = _centred_conv(ext_ref, u, prev_ref[...], next_ref[...], blk == 0, blk == nblk - 1, cw_ref) + cb_ref[...]
    c_all = -LRU_C * _softplus(-lam_ref[...])
    for h in range(nslab):
        sl = slice(h * LANES, (h + 1) * LANES)
        xh = xc[:, sl]
        pre = _mm(xh, wg_ref[h]) + bg_ref[h]
        r = _sigmoid(pre[:, 0:LANES])
        ig = _sigmoid(pre[:, LANES:])
        a = jnp.exp(c_all[:, sl] * r)
        drive = jnp.sqrt(1.0 - a * a) * (ig * xh)
        for s in range(SUBLANES):
            a_scr[h, s * pitch:s * pitch + seg, :] = a[s * seg:(s + 1) * seg, :]
            d_scr[h, s * pitch:s * pitch + seg, :] = drive[s * seg:(s + 1) * seg, :]

    order = range(seg - 1, -1, -1) if rev else range(seg)
    for h in range(nslab):
        hloc = jnp.zeros((SUBLANES, LANES), F32)
        prod = jnp.ones((SUBLANES, LANES), F32)
        for k in order:
            ak = a_scr[h, pl.ds(k, SUBLANES, stride=pitch), :]
            dk = d_scr[h, pl.ds(k, SUBLANES, stride=pitch), :]
            hloc = ak * hloc + dk
            prod = prod * ak
            d_scr[h, pl.ds(k, SUBLANES, stride=pitch), :] = hloc
            a_scr[h, pl.ds(k, SUBLANES, stride=pitch), :] = prod

    last = 0 if rev else seg - 1
    seg_order = range(SUBLANES - 1, -1, -1) if rev else range(SUBLANES)
    for h in range(nslab):
        sl = slice(h * LANES, (h + 1) * LANES)
        c = carry_ref[0:1, sl]
        for s in seg_order:
            rows = slice(s * seg, (s + 1) * seg)
            hs = d_scr[h, s * pitch:s * pitch + seg, :] + a_scr[h, s * pitch:s * pitch + seg, :] * c
            if final:
                gate = main_ref[rows, width + h * LANES:width + (h + 1) * LANES]
                o_ref[rows, sl] = (hf_ref[rows, sl] + hs) * _silu(gate)
            else:
                o_ref[rows, sl] = hs
            c = a_scr[h, pl.ds(s * pitch + last, 1), :] * c + d_scr[h, pl.ds(s * pitch + last, 1), :]
        carry_ref[0:1, sl] = c


def _lru_dir(rev, proj, hf, cw, cb, wg, bg, lam):
    nb, seq_len, two_w = proj.shape
    width = two_w // 2
    tl = TL_MIX
    nblk = seq_len // tl
    pos = _seq_pos(rev)
    blk_spec = lambda wd: pl.BlockSpec((None, tl, wd), lambda b, i: (b, pos(i, nblk), 0))
    prev, nxt = _halo_specs(rev, tl, width, seq_len)
    final = hf is not None
    pitch = tl // SUBLANES + SEG_PAD
    ins = [proj, proj, proj] + ([hf] if final else []) + [cw, cb, wg, bg, lam]
    specs = [blk_spec(two_w), prev, nxt] + ([blk_spec(width)] if final else []) + [
        _const_spec(cw.shape), _const_spec(cb.shape), _const_spec(wg.shape), _const_spec(bg.shape),
        _const_spec(lam.shape)]
    return pl.pallas_call(
        functools.partial(_lru_kernel, rev, final, tl),
        grid=(nb, nblk),
        in_specs=specs,
        out_specs=blk_spec(width),
        out_shape=jax.ShapeDtypeStruct((nb, seq_len, width), F32),
        scratch_shapes=[pltpu.VMEM((SUBLANES, width), F32),
                        pltpu.VMEM((tl + 2 * CONV_HALO, width), F32),
                        pltpu.VMEM((width // LANES, SUBLANES * pitch, LANES), F32),
                        pltpu.VMEM((width // LANES, SUBLANES * pitch, LANES), F32)],
        compiler_params=_params("parallel", "arbitrary"),
        name="lru_bwd" if rev else "lru_fwd",
    )(*ins)


def _gla_kernel(tl, heads, dk, dv, mf_ref, mb_ref, sf_ref, sb_ref, wup_ref, bup_ref, cumf_ref, cumb_ref,
                of_ref, ob_ref, st_ref):
    i = pl.program_id(1)
    hk = heads * dk
    nchunk = tl // CHUNK

    @pl.when(i == 0)
    def _():
        st_ref[...] = jnp.zeros_like(st_ref)

    hv = heads * dv
    kk_blocks = _block_mask(heads * CHUNK, hk, CHUNK, dk).astype(MXU_DTYPE)
    kv_blocks = _block_mask(heads * CHUNK, hv, CHUNK, dv).astype(MXU_DTYPE)
    st_blocks = _block_mask(hv, hk, dv, dk).astype(F32)
    dirs = ((False, mf_ref, sf_ref, cumf_ref, of_ref), (True, mb_ref, sb_ref, cumb_ref, ob_ref))
    for d, (rev, m_ref, s_ref, cum_ref, o_ref) in enumerate(dirs):
        incl = _tri_mask_heads(rev, False, heads)
        end = 0 if rev else CHUNK - 1
        pre = _mm(s_ref[...], wup_ref[d]) + bup_ref[d]
        la = (jnp.minimum(pre, 0.0) - jnp.log1p(jnp.exp(-jnp.abs(pre)))) * (1.0 / GLA_TAU)
        b = _cum_dot(cum_ref[...], la)
        q_dec = (m_ref[:, 0:hk] * (dk ** -0.5) * jnp.exp(b)).astype(MXU_DTYPE)
        k = m_ref[:, hk:2 * hk]
        k_inv = (k * jnp.exp(-b)).astype(MXU_DTYPE)
        state = st_ref[d]
        for c in _chunk_order(rev, nchunk):
            rows = slice(c * CHUNK, (c + 1) * CHUNK)
            b_end = b[c * CHUNK + end:c * CHUNK + end + 1, :]
            k_end = k[rows, :] * jnp.exp(b_end - b[rows, :])
            v = m_ref[rows, 2 * hk:2 * hk + hv].astype(MXU_DTYPE)
            k_exp = jnp.concatenate([k_inv[rows, :]] * heads, axis=0) * kk_blocks
            v_exp = jnp.concatenate([v] * heads, axis=0) * kv_blocks
            scores = jnp.where(incl, _mm_nt(q_dec[rows, :], k_exp), 0.0)
            o_ref[rows, :] = _mm(scores, v_exp) + _mm_nt(q_dec[rows, :], state)
            state = state * jnp.exp(b_end) + _mm_tn(v, k_end) * st_blocks
        st_ref[d] = state


def _gla_both(proj, small, wup, bup, heads, dk, dv):
    nb, seq_len, pw = proj.shape
    tl = TL_MIX
    nblk = seq_len // tl
    fwd = lambda wd: pl.BlockSpec((None, tl, wd), lambda b, i: (b, i, 0))
    bwd = lambda wd: pl.BlockSpec((None, tl, wd), lambda b, i: (b, nblk - 1 - i, 0))
    hv = heads * dv
    sw = small.shape[2]
    cum_f, cum_b = _block_cum(False, tl), _block_cum(True, tl)
    out = jax.ShapeDtypeStruct((nb, seq_len, hv), F32)
    return pl.pallas_call(
        functools.partial(_gla_kernel, tl, heads, dk, dv),
        grid=(nb, nblk),
        in_specs=[fwd(pw), bwd(pw), fwd(sw), bwd(sw), _const_spec(wup.shape), _const_spec(bup.shape),
                  _const_spec(cum_f.shape), _const_spec(cum_b.shape)],
        out_specs=[fwd(hv), bwd(hv)],
        out_shape=[out, out],
        scratch_shapes=[pltpu.VMEM((2, heads * dv, heads * dk), F32)],
        compiler_params=_params("parallel", "arbitrary"),
        name="gla",
    )(proj, proj, small, small, wup, bup, cum_f, cum_b)


def _dn_gates(small, gc_ref):
    beta = _sigmoid(small)
    g = -jnp.exp(gc_ref[0:1, :]) * _softplus(small + gc_ref[1:2, :])
    return jnp.where(gc_ref[2:3, :] > 0.5, beta, jnp.where(gc_ref[3:4, :] > 0.5, g, 0.0))


def _dn_forms(gx, d, heads):
    assert 2 * heads == SUBLANES and 2 * CHUNK == LANES
    base = 2 * heads * d
    rows8 = gx.T[base:base + SUBLANES, :]
    rows8_rot = pltpu.roll(rows8, CHUNK, axis=1)
    rep = lambda lane: jnp.broadcast_to(gx[:, lane:lane + 1], (gx.shape[0], LANES))
    return rows8, rows8_rot, [rep(base + h) for h in range(heads)], [rep(base + heads + h) for h in range(heads)]


def _head_rows(rows8, rows8_rot, r0, c, heads):
    nchunk = rows8.shape[1] // CHUNK
    low_half = lax.broadcasted_iota(jnp.int32, (1, LANES), 1) < CHUNK
    out = []
    for j in range(heads // 2):
        halves = []
        for h in (2 * j, 2 * j + 1):
            src, slot = (rows8, c) if c % 2 == h % 2 else (rows8_rot, (c + 1) % nchunk)
            halves.append(src[r0 + h:r0 + h + 1, LANES * (slot // 2):LANES * (slot // 2 + 1)])
        out.append(jnp.where(low_half, halves[0], halves[1]))
    return jnp.concatenate(out, axis=1)


def _head_cols(rep, c, heads):
    low_half = lax.broadcasted_iota(jnp.int32, (1, LANES), 1) < CHUNK
    rows = slice(c * CHUNK, (c + 1) * CHUNK)
    return jnp.concatenate([jnp.where(low_half, rep[2 * j][rows, :], rep[2 * j + 1][rows, :])
                            for j in range(heads // 2)], axis=1)


def _dn_gamma(rev, forms, c, heads):
    rows8, rows8_rot, _, g_rep = forms
    incl = _tri_mask_heads(rev, False, heads)
    diff = _head_cols(g_rep, c, heads) - _head_rows(rows8, rows8_rot, heads, c, heads)
    return jnp.where(incl, jnp.exp(jnp.where(incl, diff, 0.0)), 0.0)


def _head_expand(x, blocks, heads):
    return jnp.concatenate([x.astype(MXU_DTYPE)] * heads, axis=0) * blocks


def _dn_prep_kernel(tl, heads, dh, main_ref, prev_ref, next_ref, small_ref, cw_ref, gcst_ref, cumf_ref, cumb_ref,
                    qkv_ref, gx_ref, af_ref, ab_ref, ext_ref):
    i = pl.program_id(1)
    nblk = pl.num_programs(1)
    hw = heads * dh
    x = main_ref[:, 0:3 * hw]
    y = _silu(_centred_conv(ext_ref, x, prev_ref[...], next_ref[...], i == 0, i == nblk - 1, cw_ref))
    for h in range(heads):
        for part, scale in ((0, dh ** -0.5), (1, 1.0)):
            sl = slice(part * hw + h * dh, part * hw + (h + 1) * dh)
            z = y[:, sl]
            qkv_ref[:, sl] = z * lax.rsqrt(jnp.sum(z * z, axis=-1, keepdims=True) + NORM_EPS) * scale
    qkv_ref[:, 2 * hw:] = y[:, 2 * hw:]
    bg = _dn_gates(small_ref[...], gcst_ref)
    lane = lax.broadcasted_iota(jnp.int32, bg.shape, 1)
    run = jnp.where(lane < 2 * heads, _cum_dot(cumf_ref[...], bg), _cum_dot(cumb_ref[...], bg))
    gx = jnp.where(gcst_ref[3:4, :] > 0.5, run, bg)
    gx_ref[...] = gx
    blocks = _block_mask(heads * CHUNK, hw, CHUNK, dh).astype(MXU_DTYPE)
    for d, (rev, a_ref) in enumerate(((False, af_ref), (True, ab_ref))):
        strict = _tri_mask_heads(rev, True, heads)
        forms = _dn_forms(gx, d, heads)
        for c in range(tl // CHUNK):
            k = qkv_ref[c * CHUNK:(c + 1) * CHUNK, hw:2 * hw]
            kk = _mm_nt(k, _head_expand(k, blocks, heads))
            a_ref[c] = jnp.where(strict, kk * _head_cols(forms[2], c, heads) * _dn_gamma(rev, forms, c, heads), 0.0)


def _dn_prep(proj, small, cw, gate_consts, heads, dh):
    nb, seq_len, pw = proj.shape
    tl = TL_MIX
    nblk = seq_len // tl
    hw = heads * dh
    blk_spec = lambda wd: pl.BlockSpec((None, tl, wd), lambda b, i: (b, i, 0))
    prev, nxt = _halo_specs(False, tl, 3 * hw, seq_len)
    a_spec = pl.BlockSpec((None, tl // CHUNK, CHUNK, heads * CHUNK), lambda b, i: (b, i, 0, 0))
    a_shape = jax.ShapeDtypeStruct((nb, seq_len // CHUNK, CHUNK, heads * CHUNK), F32)
    lanes_shape = jax.ShapeDtypeStruct((nb, seq_len, LANES), F32)
    cum_f, cum_b = _block_cum(False, tl), _block_cum(True, tl)
    return pl.pallas_call(
        functools.partial(_dn_prep_kernel, tl, heads, dh),
        grid=(nb, nblk),
        in_specs=[blk_spec(pw), prev, nxt, blk_spec(small.shape[2]), _const_spec(cw.shape),
                  _const_spec(gate_consts.shape), _const_spec(cum_f.shape), _const_spec(cum_b.shape)],
        out_specs=[blk_spec(3 * hw), blk_spec(LANES), a_spec, a_spec],
        out_shape=[jax.ShapeDtypeStruct((nb, seq_len, 3 * hw), F32), lanes_shape, a_shape, a_shape],
        scratch_shapes=[pltpu.VMEM((tl + 2 * CONV_HALO, 3 * hw), F32)],
        compiler_params=_params("parallel", "parallel"),
        name="dn_prep",
    )(proj, proj, proj, small, cw, gate_consts, cum_f, cum_b)


def _dn_solve_kernel(upper, a_ref, t_ref):
    half = CHUNK // 2
    lanes = t_ref.shape[2]
    col = lax.broadcasted_iota(jnp.int32, (half, lanes), 0)

    def row(idx, carry):
        t = (CHUNK - 1 - idx) if upper else idx

        def term_lo(s, acc):
            return acc - a_ref[t, pl.ds(s, 1), :] * t_ref[s, 0:half, :]

        def term_hi(s, acc):
            return acc - a_ref[t, pl.ds(s, 1), :] * t_ref[s, half:, :]

        if upper:
            lo_range, hi_range = (t + 1, jnp.maximum(t + 1, half)), (t + 1, CHUNK)
        else:
            lo_range, hi_range = (0, t), (half, jnp.maximum(t, half))
        t_ref[t, 0:half, :] = lax.fori_loop(*lo_range, term_lo, jnp.where(col == t, 1.0, 0.0))
        t_ref[t, half:, :] = lax.fori_loop(*hi_range, term_hi, jnp.where(col + half == t, 1.0, 0.0))
        return carry

    lax.fori_loop(0, CHUNK, row, 0)


def _dn_solve(a_t, upper):
    n = a_t.shape[2]
    lanes = min(SOLVE_LANES, n)
    spec = pl.BlockSpec((CHUNK, CHUNK, lanes), lambda i: (0, 0, i))
    return pl.pallas_call(
        functools.partial(_dn_solve_kernel, upper),
        grid=(n // lanes,),
        in_specs=[spec],
        out_specs=spec,
        out_shape=jax.ShapeDtypeStruct(a_t.shape, F32),
        compiler_params=_params("parallel"),
        name="dn_solve_bwd" if upper else "dn_solve_fwd",
    )(a_t)


def _dn_main_kernel(tl, heads, dh, qf_ref, gxf_ref, tf_ref, qb_ref, gxb_ref, tb_ref, of_ref, ob_ref, s_ref):
    i = pl.program_id(1)
    hw = heads * dh
    nchunk = tl // CHUNK
    npair = heads // 2
    pw = 2 * dh

    @pl.when(i == 0)
    def _():
        s_ref[...] = jnp.zeros_like(s_ref)

    blocks = _block_mask(heads * CHUNK, hw, CHUNK, dh).astype(MXU_DTYPE)
    t_blocks = _block_mask(heads * CHUNK, heads * CHUNK, CHUNK, CHUNK).astype(MXU_DTYPE)
    pair_blocks = _block_mask(pw, pw, dh, dh).astype(F32)
    stack = lambda x: jnp.concatenate([x[:, h * dh:(h + 1) * dh] for h in range(heads)], axis=0)
    dirs = ((False, qf_ref, gxf_ref, tf_ref, of_ref), (True, qb_ref, gxb_ref, tb_ref, ob_ref))
    for d, (rev, qkv_ref, gx_ref, t_ref, o_ref) in enumerate(dirs):
        incl = _tri_mask_heads(rev, False, heads)
        end = 0 if rev else CHUNK - 1
        forms = _dn_forms(gx_ref[...], d, heads)
        b_nat = jnp.concatenate(forms[2], axis=1)
        g_nat = jnp.concatenate(forms[3], axis=1)
        eg_nat = jnp.exp(g_nat)
        states = [s_ref[d, j] for j in range(npair)]
        for c in _chunk_order(rev, nchunk):
            rows = slice(c * CHUNK, (c + 1) * CHUNK)
            q = qkv_ref[rows, 0:hw]
            k = qkv_ref[rows, hw:2 * hw]
            v = qkv_ref[rows, 2 * hw:3 * hw]
            eg = eg_nat[rows, :]
            attn = jnp.where(incl, _mm_nt(q, _head_expand(k, blocks, heads)) * _dn_gamma(rev, forms, c, heads), 0.0)
            beta = b_nat[rows, :]
            t_bd = _head_expand(t_ref[c], t_blocks, heads)
            wu = _mm(t_bd, jnp.concatenate([stack(k * beta * eg), stack(v * beta)], axis=1))
            w = jnp.concatenate([wu[h * CHUNK:(h + 1) * CHUNK, 0:dh] for h in range(heads)], axis=1)
            u = jnp.concatenate([wu[h * CHUNK:(h + 1) * CHUNK, dh:] for h in range(heads)], axis=1)
            lhs = jnp.concatenate([w, q * eg], axis=0)
            xs = [_mm(lhs[:, j * pw:(j + 1) * pw], states[j]) for j in range(npair)]
            v_new = u - jnp.concatenate([x[0:CHUNK, :] for x in xs], axis=1)
            o_ref[rows, :] = (jnp.concatenate([x[CHUNK:, :] for x in xs], axis=1)
                              + _mm(attn, _head_expand(v_new, blocks, heads)))
            g_end = g_nat[c * CHUNK + end:c * CHUNK + end + 1, :]
            k_end = k * jnp.exp(g_end - g_nat[rows, :])
            e_end = jnp.exp(g_end)
            for j in range(npair):
                ps = slice(j * pw, (j + 1) * pw)
                states[j] = states[j] * e_end[:, ps] + _mm_tn(k_end[:, ps], v_new[:, ps]) * pair_blocks
        for j in range(npair):
            s_ref[d, j] = states[j]


def _dn_main(qkv, gx, t_f, t_b, heads, dh):
    nb, seq_len, _ = qkv.shape
    tl = TL_MIX
    nblk = seq_len // tl
    hw = heads * dh
    fwd = lambda wd: pl.BlockSpec((None, tl, wd), lambda b, i: (b, i, 0))
    bwd = lambda wd: pl.BlockSpec((None, tl, wd), lambda b, i: (b, nblk - 1 - i, 0))
    t_blk = (None, tl // CHUNK, CHUNK, heads * CHUNK)
    t_fwd = pl.BlockSpec(t_blk, lambda b, i: (b, i, 0, 0))
    t_bwd = pl.BlockSpec(t_blk, lambda b, i: (b, nblk - 1 - i, 0, 0))
    out = jax.ShapeDtypeStruct((nb, seq_len, hw), F32)
    return pl.pallas_call(
        functools.partial(_dn_main_kernel, tl, heads, dh),
        grid=(nb, nblk),
        in_specs=[fwd(3 * hw), fwd(LANES), t_fwd, bwd(3 * hw), bwd(LANES), t_bwd],
        out_specs=[fwd(hw), bwd(hw)],
        out_shape=[out, out],
        scratch_shapes=[pltpu.VMEM((2, heads // 2, 2 * dh, 2 * dh), F32)],
        compiler_params=_params("parallel", "arbitrary"),
        name="dn_main",
    )(qkv, gx, t_f, qkv, gx, t_b)


def _dn_branch(proj, small, cw, gate_consts, heads, dh):
    nb, seq_len, _ = proj.shape
    nc = seq_len // CHUNK
    qkv, gx, a_f, a_b = _dn_prep(proj, small, cw, gate_consts, heads, dh)
    n_inst = nb * nc * heads
    to_lanes = lambda a: a.reshape(nb, nc, CHUNK, heads, CHUNK).transpose(2, 4, 0, 1, 3).reshape(CHUNK, CHUNK, n_inst)
    from_lanes = lambda t: (t.reshape(CHUNK, CHUNK, nb, nc, heads).transpose(2, 3, 0, 4, 1)
                            .reshape(nb, nc, CHUNK, heads * CHUNK).astype(MXU_DTYPE))
    t_f = from_lanes(_dn_solve(to_lanes(a_f), upper=False))
    t_b = from_lanes(_dn_solve(to_lanes(a_b), upper=True))
    return _dn_main(qkv, gx, t_f, t_b, heads, dh)


S5_CHUNK = 16


def _cpow(lam_re, lam_im, dt, e):
    mag = jnp.exp(lam_re * dt * e)
    ang = lam_im * dt * e
    return mag * jnp.cos(ang), mag * jnp.sin(ang)


def _s5_wt_kernel(gw, lre_ref, lim_ref, ldt_ref, bre_ref, bim_ref, ctre_ref, ctim_ref, cre_ref, cim_ref,
                  k_ref, zwre_ref, zwim_ref, cwre_ref, cwimn_ref):
    backward = pl.program_id(0) == 1
    lam_re, lam_im, dt = lre_ref[...], lim_ref[...], jnp.exp(ldt_ref[...])
    a_re, a_im = _cpow(lam_re, lam_im, dt, 1.0)
    den = lam_re * lam_re + lam_im * lam_im
    n_re = a_re - 1.0
    f_re = (n_re * lam_re + a_im * lam_im) / den
    f_im = (a_im * lam_re - n_re * lam_im) / den
    bb_re = f_re * bre_ref[...] - f_im * bim_ref[...]
    bb_im = f_re * bim_ref[...] + f_im * bre_ref[...]
    pos = jnp.right_shift(lax.broadcasted_iota(jnp.int32, lam_re.shape, 1), _log2(gw)).astype(F32)
    p_re, p_im = _cpow(lam_re, lam_im, dt, pos)
    e_re = p_re * bb_re - p_im * bb_im
    e_im = p_re * bb_im + p_im * bb_re
    k_ref[...] = (jnp.dot(cre_ref[...], e_re, precision=HI, preferred_element_type=F32)
                  - jnp.dot(cim_ref[...], e_im, precision=HI, preferred_element_type=F32))
    z_re, z_im = _cpow(lam_re, lam_im, dt, jnp.where(backward, pos, (S5_CHUNK - 1.0) - pos))
    zwre_ref[...] = z_re * bb_re - z_im * bb_im
    zwim_ref[...] = z_re * bb_im + z_im * bb_re
    c_re, c_im = _cpow(lam_re, lam_im, dt, jnp.where(backward, S5_CHUNK - pos, pos + 1.0))
    cwre_ref[...] = ctre_ref[...] * c_re - ctim_ref[...] * c_im
    cwimn_ref[...] = -(ctre_ref[...] * c_im + ctim_ref[...] * c_re)


def _s5_weights(lam_re, lam_im, log_dt, b_re, b_im, c_re, c_im):
    _, groups, states, gw = b_re.shape
    wl = S5_CHUNK * gw
    rep = lambda a: jnp.broadcast_to(a[..., None], (2, groups, states, wl))
    tile = lambda a: jnp.tile(a, (1, 1, 1, S5_CHUNK))
    blk = lambda r, c: pl.BlockSpec((None, None, r, c), lambda d, g: (d, g, 0, 0))
    ct = lambda a: tile(a.transpose(0, 1, 3, 2))
    big = jax.ShapeDtypeStruct((2, groups, states, wl), F32)
    return pl.pallas_call(
        functools.partial(_s5_wt_kernel, gw),
        grid=(2, groups),
        in_specs=[blk(states, wl)] * 7 + [blk(gw, states)] * 2,
        out_specs=[blk(gw, wl)] + [blk(states, wl)] * 4,
        out_shape=[jax.ShapeDtypeStruct((2, groups, gw, wl), F32), big, big, big, big],
        compiler_params=_params("parallel", "parallel"),
        name="s5_wt",
    )(rep(lam_re), rep(lam_im), rep(jnp.broadcast_to(log_dt[..., None], lam_re.shape)), tile(b_re), tile(b_im),
      ct(c_re), ct(c_im), c_re, c_im)


def _s5_pow_kernel(seg, lre_ref, lim_ref, ldt_ref, pw_ref):
    a_re, a_im = _cpow(lre_ref[...], lim_ref[...], jnp.exp(ldt_ref[...]), float(S5_CHUNK))
    x_re, x_im = a_re, a_im
    for k in range(seg):
        pw_ref[0, k] = x_re
        pw_ref[1, k] = x_im
        x_re, x_im = x_re * a_re - x_im * a_im, x_re * a_im + x_im * a_re


def _s5_pow(lre, lim, ldt, seg):
    return pl.pallas_call(
        functools.partial(_s5_pow_kernel, seg),
        out_shape=jax.ShapeDtypeStruct((2, seg) + lre.shape, F32),
        name="s5_pow",
    )(lre, lim, ldt)


def _gelu_tanh(x):
    return 0.5 * x * (1.0 + jnp.tanh(math.sqrt(2.0 / math.pi) * (x + 0.044715 * (x * x * x))))


def _s5_mix_kernel(nr, u_ref, tf_ref, tb_ref, zw_ref, cw_ref, pw_ref, y_ref, pad_re, pad_im, f_re, f_im, b_re, b_im):
    seg = nr // SUBLANES
    pitch = seg + SEG_PAD
    half = LANES // 2
    fwd_lanes = lax.broadcasted_iota(jnp.int32, (1, LANES), 1) < half
    w1 = jnp.concatenate([(tf_ref[...] + tb_ref[...]).astype(MXU_DTYPE), zw_ref[...]], axis=1)
    yz = jnp.dot(u_ref[...], w1, preferred_element_type=F32)
    wl = tf_ref.shape[1]

    for pad, col in ((pad_re, wl), (pad_im, wl + LANES)):
        pad[0:SUBLANES, :] = jnp.zeros((SUBLANES, LANES), F32)
        pad[SUBLANES:SUBLANES + nr, :] = yz[:, col:col + LANES]
        pad[SUBLANES + nr:, :] = jnp.zeros((SUBLANES, LANES), F32)
    for s in range(SUBLANES):
        dst = slice(s * pitch, s * pitch + seg)
        for pad, fw, bw in ((pad_re, f_re, b_re), (pad_im, f_im, b_im)):
            fw[dst, :] = pad[pl.ds(SUBLANES - 1 + s * seg, seg), :]
            bw[dst, :] = pad[pl.ds(SUBLANES + 1 + s * seg, seg), :]

    a_re = jnp.broadcast_to(pw_ref[0, 0:1, :], (SUBLANES, LANES))
    a_im = jnp.broadcast_to(pw_ref[1, 0:1, :], (SUBLANES, LANES))
    for re_scr, im_scr, order in ((f_re, f_im, range(seg)), (b_re, b_im, range(seg - 1, -1, -1))):
        sr = jnp.zeros((SUBLANES, LANES), F32)
        si = jnp.zeros((SUBLANES, LANES), F32)
        for k in order:
            sr, si = (a_re * sr - a_im * si + re_scr[pl.ds(k, SUBLANES, stride=pitch), :],
                      a_re * si + a_im * sr + im_scr[pl.ds(k, SUBLANES, stride=pitch), :])
            re_scr[pl.ds(k, SUBLANES, stride=pitch), :] = sr
            im_scr[pl.ds(k, SUBLANES, stride=pitch), :] = si

    pr, pi = pw_ref[2], pw_ref[3]
    sg_re = pw_ref[0, seg - 1:seg, :]
    sg_im = pw_ref[1, seg - 1:seg, :]
    zero = jnp.zeros((1, LANES), F32)
    cf, cb = [(zero, zero)], [(zero, zero)]
    for s in range(SUBLANES - 1):
        er, ei = f_re[pl.ds(s * pitch + seg - 1, 1), :], f_im[pl.ds(s * pitch + seg - 1, 1), :]
        cr, ci = cf[-1]
        cf.append((sg_re * cr - sg_im * ci + er, sg_re * ci + sg_im * cr + ei))
        t = SUBLANES - 1 - s
        er, ei = b_re[pl.ds(t * pitch, 1), :], b_im[pl.ds(t * pitch, 1), :]
        cr, ci = cb[-1]
        cb.append((sg_re * cr - sg_im * ci + er, sg_re * ci + sg_im * cr + ei))
    cb = cb[::-1]
    parts_re, parts_im = [], []
    for s in range(SUBLANES):
        src = slice(s * pitch, s * pitch + seg)
        cr = jnp.where(fwd_lanes, cf[s][0], cb[s][0])
        ci = jnp.where(fwd_lanes, cf[s][1], cb[s][1])
        parts_re.append(jnp.where(fwd_lanes, f_re[src, :], b_re[src, :]) + pr * cr - pi * ci)
        parts_im.append(jnp.where(fwd_lanes, f_im[src, :], b_im[src, :]) + pr * ci + pi * cr)
    states = jnp.concatenate([jnp.concatenate(parts_re, axis=0), jnp.concatenate(parts_im, axis=0)], axis=1)
    y_ref[...] = yz[:, 0:wl] + _mm(states, cw_ref[...])


def _s5_mix(u_g, toep_f, toep_b, zw, cw, pw):
    groups, nb, nr, wl = u_g.shape
    seg = nr // SUBLANES
    pitch = seg + SEG_PAD
    per_g = lambda r, c: pl.BlockSpec((None, r, c), lambda g, b: (g, 0, 0))
    rows = pl.BlockSpec((None, None, nr, wl), lambda g, b: (g, b, 0, 0))
    scan = pltpu.VMEM((SUBLANES * pitch, LANES), F32)
    pad = pltpu.VMEM((nr + 2 * SUBLANES, LANES), F32)
    return pl.pallas_call(
        functools.partial(_s5_mix_kernel, nr),
        grid=(groups, nb),
        in_specs=[rows, per_g(wl, wl), per_g(wl, wl), per_g(wl, 2 * LANES), per_g(2 * LANES, wl),
                  pl.BlockSpec((None, 4, seg, LANES), lambda g, b: (g, 0, 0, 0))],
        out_specs=rows,
        out_shape=jax.ShapeDtypeStruct((groups, nb, nr, wl), F32),
        scratch_shapes=[pad, pad, scan, scan, scan, scan],
        compiler_params=_params("parallel", "parallel"),
        name="s5_mix",
    )(u_g, toep_f, toep_b, zw, cw, pw)


def _s5_finish_kernel(width, y_ref, main_ref, dsk_ref, wglu_ref, bglu_ref, o_ref):
    y = y_ref[...] + dsk_ref[...] * main_ref[:, 0:width]
    z = _gelu_tanh(y)
    z = z * _sigmoid(_mm(z, wglu_ref[...]) + bglu_ref[...])
    o_ref[...] = z * _silu(main_ref[:, width:])


def _s5_finish(y2, proj2, dsk, wglu, bglu):
    t, width = y2.shape
    row = lambda wd: pl.BlockSpec((TM_PROJ, wd), lambda i: (i, 0))
    return pl.pallas_call(
        functools.partial(_s5_finish_kernel, width),
        grid=(t // TM_PROJ,),
        in_specs=[row(width), row(2 * width), _const_spec(dsk.shape), _const_spec(wglu.shape),
                  _const_spec(bglu.shape)],
        out_specs=row(width),
        out_shape=jax.ShapeDtypeStruct((t, width), F32),
        compiler_params=_params("parallel"),
        name="s5_finish",
    )(y2, proj2, dsk, wglu, bglu)


def _s5_toeplitz_branch(proj, lam_re, lam_im, log_dt, b_re, b_im, c_re, c_im, dsk, wglu, bglu):
    nb, seq_len, two_w = proj.shape
    width = two_w // 2
    _, groups, states, gw = b_re.shape
    nr = seq_len // S5_CHUNK
    seg = nr // SUBLANES
    wl = S5_CHUNK * gw
    kk, zw_re, zw_im, cw_re, cw_imn = _s5_weights(lam_re, lam_im, log_dt, b_re, b_im, c_re, c_im)

    lag = jnp.arange(S5_CHUNK)[None, :] - jnp.arange(S5_CHUNK)[:, None]
    kt = kk.reshape(2, groups, gw, S5_CHUNK, gw).transpose(0, 1, 3, 4, 2)

    def toeplitz(d, lag_d):
        t = jnp.where((lag_d >= 0)[None, :, :, None, None], kt[d][:, jnp.maximum(lag_d, 0)], 0.0)
        return t.transpose(0, 1, 3, 2, 4).reshape(groups, wl, wl)

    toep_f, toep_b = toeplitz(0, lag), toeplitz(1, -lag)
    by_rows = lambda a: a.transpose(0, 2, 1)
    zw = jnp.concatenate([by_rows(zw_re[0]), by_rows(zw_re[1]), by_rows(zw_im[0]), by_rows(zw_im[1])],
                         axis=2).astype(MXU_DTYPE)
    cw = jnp.concatenate([cw_re[0], cw_re[1], cw_imn[0], cw_imn[1]], axis=1).astype(MXU_DTYPE)

    flat = lambda a: a.reshape(2, groups * states // LANES, LANES)
    pw = _s5_pow(flat(lam_re), flat(lam_im), flat(jnp.broadcast_to(log_dt[..., None], lam_re.shape)), seg)
    pw = pw.reshape(2, seg, 2, groups, states).transpose(2, 3, 0, 1, 4)
    plain = jnp.concatenate([pw[0], pw[1]], axis=3)
    by_row = jnp.concatenate([pw[0], jnp.flip(pw[1], axis=2)], axis=3)
    pw_all = jnp.concatenate([plain, by_row], axis=1)

    u_g = (proj[:, :, 0:width].reshape(nb, nr, S5_CHUNK, groups, gw).transpose(3, 0, 1, 2, 4)
           .reshape(groups, nb, nr, wl).astype(MXU_DTYPE))
    y_g = _s5_mix(u_g, toep_f, toep_b, zw, cw, pw_all)
    y = y_g.reshape(groups, nb, nr, S5_CHUNK, gw).transpose(1, 2, 3, 0, 4).reshape(nb * seq_len, width)
    return _s5_finish(y, proj.reshape(nb * seq_len, two_w), dsk, wglu, bglu).reshape(nb, seq_len, width)


def _layer(x_parts, group_rows, seq_len, p, final_g, final):
    d = x_parts[0].shape[1]
    nb = sum(a.shape[0] for a in x_parts) // seq_len
    bw = d // 2
    bf = lambda a: a.astype(MXU_DTYPE)
    row = lambda a: a.reshape(1, -1)

    lru_blocks = p['lru_w_a'].shape[1]
    gla_heads = GLA_HEADS
    gla_dk = p['gla_w_up'].shape[2] // gla_heads
    gla_dv = bw // gla_heads
    gla_rank = p['gla_w_up'].shape[1]
    dn_heads = p['dn_a_log'].shape[1]
    dn_dh = bw // dn_heads
    widths = (bw, bw, gla_heads * gla_dk, gla_heads * gla_dk, bw, bw, 2 * gla_rank, 3 * bw, bw, 4 * dn_heads, bw, bw)
    offs = [0]
    for wd in widths:
        offs.append(offs[-1] + wd)
    w_in = p['w_in']
    cols = lambda a, b: w_in[:, offs[a]:offs[b]]
    n_gate = 4 * dn_heads
    n_small = n_gate + 2 * gla_rank
    w_small = jnp.concatenate([cols(9, 10), cols(6, 7), jnp.zeros((d, LANES - n_small), w_in.dtype)], axis=1)
    w_cat = bf(jnp.concatenate([cols(0, 2), cols(2, 6), cols(7, 9), cols(10, 12), w_small], axis=1))
    out_w = (2 * bw, offs[6] - offs[2], 4 * bw, 2 * bw, LANES)
    lru_p, gla_p, dn_p, s5_p, small = _in_proj(x_parts, row(p['norm_g']), w_cat, out_w)
    seq = lambda a: a.reshape(nb, seq_len, a.shape[1])
    flat = lambda a: a.reshape(nb * seq_len, a.shape[2])
    lru_s, gla_s, dn_s, s5_s, small_s = seq(lru_p), seq(gla_p), seq(dn_p), seq(s5_p), seq(small)

    lanes_per_blk = bw // lru_blocks
    y_lru = None
    for dct, rev in enumerate((False, True)):
        wg = bf(jnp.concatenate([p['lru_w_a'][dct], p['lru_w_x'][dct]], axis=2))
        bg = jnp.concatenate([p['lru_b_a'][dct].reshape(lru_blocks, 1, lanes_per_blk),
                              p['lru_b_x'][dct].reshape(lru_blocks, 1, lanes_per_blk)], axis=2)
        y_lru = _lru_dir(rev, lru_s, y_lru, p['lru_conv_w'], row(p['lru_conv_b']), wg, bg, row(p['lru_lambda'][dct]))

    wup = jnp.zeros((2, LANES, gla_heads * gla_dk), F32)
    for dct in range(2):
        lo = n_gate + dct * gla_rank
        wup = wup.at[dct, lo:lo + gla_rank].set(p['gla_w_up'][dct])
    gla_f, gla_b = _gla_both(gla_s, small_s, bf(wup), p['gla_b_up'].reshape(2, 1, -1), gla_heads, gla_dk, gla_dv)

    alpha_lanes = (jnp.arange(2)[:, None] * 2 * dn_heads + dn_heads + jnp.arange(dn_heads)[None, :]).reshape(-1)
    beta_lanes = alpha_lanes - dn_heads
    gate_consts = jnp.zeros((SUBLANES, LANES), F32)
    gate_consts = gate_consts.at[0, alpha_lanes].set(p['dn_a_log'].reshape(-1))
    gate_consts = gate_consts.at[1, alpha_lanes].set(p['dn_dt_bias'].reshape(-1))
    gate_consts = gate_consts.at[2, beta_lanes].set(1.0).at[3, alpha_lanes].set(1.0)
    dn_f, dn_b = _dn_branch(dn_s, small_s, p['dn_conv_w'], gate_consts, dn_heads, dn_dh)

    y_s5 = _s5_toeplitz_branch(s5_s, p['s5_lambda_re'], p['s5_lambda_im'], p['s5_log_dt'], p['s5_b_re'], p['s5_b_im'],
                      p['s5_c_re'], p['s5_c_im'], row(p['s5_d']), bf(p['s5_w_glu']), row(p['s5_b_glu']))

    gla_gate_col = (offs[5] - offs[2]) // bw
    dn_gate_col = (offs[8] - offs[7]) // bw
    rest = (flat(y_lru),
            (flat(gla_f), flat(gla_b), gla_p, gla_gate_col, row(p['gla_norm_g']), gla_heads),
            (flat(dn_f), flat(dn_b), dn_p, dn_gate_col, row(p['dn_norm_g']), dn_heads),
            flat(y_s5), row(p['norm_g']),
            bf(p['w_merge_gate']), p['b_merge_gate'].reshape(4, 1, d), bf(p['w_branch']), bf(p['w_out']),
            row(final_g), final)
    if not final:
        return [_out_proj([_whole(a) for a in x_parts], 0, *rest)]
    outs, row0 = [], 0
    for gi, rows in enumerate(group_rows):
        x_range = (x_parts[0], row0, rows) if len(x_parts) == 1 else _whole(x_parts[gi])
        outs.append(_out_proj([x_range], row0, *rest))
        row0 += rows
    return outs


_LAYER_PARAM_NAMES = (
    'norm_g', 'w_in', 'lru_conv_w', 'lru_conv_b', 'lru_w_a', 'lru_b_a', 'lru_w_x', 'lru_b_x', 'lru_lambda',
    'gla_w_up', 'gla_b_up', 'gla_norm_g', 'dn_conv_w', 'dn_a_log', 'dn_dt_bias', 'dn_norm_g',
    's5_lambda_re', 's5_lambda_im', 's5_log_dt', 's5_b_re', 's5_b_im', 's5_c_re', 's5_c_im', 's5_d',
    's5_w_glu', 's5_b_glu', 'w_branch', 'w_merge_gate', 'b_merge_gate', 'w_out')


def kernel(x_prompt, x_sample, norm_g, w_in, lru_conv_w, lru_conv_b, lru_w_a, lru_b_a, lru_w_x, lru_b_x, lru_lambda, gla_w_up, gla_b_up, gla_norm_g, dn_conv_w, dn_a_log, dn_dt_bias, dn_norm_g, s5_lambda_re, s5_lambda_im, s5_log_dt, s5_b_re, s5_b_im, s5_c_re, s5_c_im, s5_d, s5_w_glu, s5_b_glu, w_branch, w_merge_gate, b_merge_gate, w_out, final_norm_g):
    stacked = (norm_g, w_in, lru_conv_w, lru_conv_b, lru_w_a, lru_b_a, lru_w_x, lru_b_x, lru_lambda,
               gla_w_up, gla_b_up, gla_norm_g, dn_conv_w, dn_a_log, dn_dt_bias, dn_norm_g,
               s5_lambda_re, s5_lambda_im, s5_log_dt, s5_b_re, s5_b_im, s5_c_re, s5_c_im, s5_d,
               s5_w_glu, s5_b_glu, w_branch, w_merge_gate, b_merge_gate, w_out)
    depth = norm_g.shape[0]
    seq_len, d = x_prompt.shape[1:]
    assert x_sample.shape[1] == seq_len
    groups = (x_prompt, x_sample)
    parts = [a.reshape(-1, d) for a in groups]
    group_rows = [a.shape[0] for a in parts]
    for layer in range(depth):
        p = {name: arr[layer] for name, arr in zip(_LAYER_PARAM_NAMES, stacked)}
        parts = _layer(parts, group_rows, seq_len, p, final_norm_g, layer == depth - 1)
    return tuple(y.reshape(a.shape) for y, a in zip(parts, groups))
```

```python
import functools
import math

import jax
import jax.numpy as jnp
from jax import lax
from jax.experimental import pallas as pl
from jax.experimental.pallas import tpu as pltpu

F32 = jnp.float32
MXU_DTYPE = jnp.bfloat16
HI = lax.Precision.HIGHEST

NORM_EPS = 1e-6
CONV_W = 4
CONV_HALO = 8
LRU_C = 8.0
GLA_TAU = 16.0
GLA_HEADS = 4
CHUNK = 64
LANES = 128
SUBLANES = 8
SEG_PAD = 4
VMEM_LIMIT = 56 * 1024 * 1024

TM_PROJ = 256
TL_MIX = 256
SOLVE_CHUNKS = 128


def _mm(a, b):
    return jnp.dot(a.astype(MXU_DTYPE), b.astype(MXU_DTYPE), preferred_element_type=F32)


def _mm_nt(a, b):
    return lax.dot_general(a.astype(MXU_DTYPE), b.astype(MXU_DTYPE), (((1,), (1,)), ((), ())),
                           preferred_element_type=F32)


def _mm_tn(a, b):
    return lax.dot_general(a.astype(MXU_DTYPE), b.astype(MXU_DTYPE), (((0,), (0,)), ((), ())),
                           preferred_element_type=F32)


def _cum_dot(cum3, x):
    hi = x.astype(jnp.bfloat16)
    r1 = x - hi.astype(F32)
    mid = r1.astype(jnp.bfloat16)
    lo = (r1 - mid.astype(F32)).astype(jnp.bfloat16)
    return jnp.dot(cum3, jnp.concatenate([hi, mid, lo], axis=0), preferred_element_type=F32)


def _sigmoid(x):
    return 1.0 / (1.0 + jnp.exp(-x))


def _silu(x):
    return x * _sigmoid(x)


def _softplus(x):
    return jnp.maximum(x, 0.0) + jnp.log1p(jnp.exp(-jnp.abs(x)))


def _rms(x, g):
    return x * lax.rsqrt(jnp.mean(x * x, axis=-1, keepdims=True) + NORM_EPS) * g


def _params(*sem):
    return pltpu.CompilerParams(dimension_semantics=sem, vmem_limit_bytes=VMEM_LIMIT)


def _const_spec(shape):
    nd = len(shape)
    return pl.BlockSpec(shape, lambda *_: (0,) * nd)


def _seq_pos(rev):
    if rev:
        return lambda i, n: n - 1 - i
    return lambda i, n: i


def _log2(n):
    assert n > 0 and n & (n - 1) == 0, n
    return n.bit_length() - 1


def _block_mask(nrow, ncol, row_blk, col_blk):
    r = jnp.right_shift(lax.broadcasted_iota(jnp.int32, (nrow, ncol), 0), _log2(row_blk))
    c = jnp.right_shift(lax.broadcasted_iota(jnp.int32, (nrow, ncol), 1), _log2(col_blk))
    return r == c


def _tri_mask_heads(rev, strict, heads):
    t = lax.broadcasted_iota(jnp.int32, (CHUNK, heads * CHUNK), 0)
    s = jnp.bitwise_and(lax.broadcasted_iota(jnp.int32, (CHUNK, heads * CHUNK), 1), CHUNK - 1)
    if rev:
        return (s > t) if strict else (s >= t)
    return (s < t) if strict else (s <= t)


def _block_cum(rev, tl):
    t = jnp.arange(tl)[:, None]
    s = jnp.arange(tl)[None, :]
    same = (t // CHUNK) == (s // CHUNK)
    cum = (same & ((s >= t) if rev else (s <= t))).astype(jnp.bfloat16)
    return jnp.concatenate([cum, cum, cum], axis=1)


def _chunk_order(rev, nchunk):
    return range(nchunk - 1, -1, -1) if rev else range(nchunk)


def _row_parts_specs(parts, width):
    specs, starts, step0 = [], [], 0
    for _, row0, rows in parts:
        steps, blk0 = rows // TM_PROJ, row0 // TM_PROJ
        specs.append(pl.BlockSpec(
            (TM_PROJ, width), lambda i, s=step0, n=steps, b=blk0: (b + jnp.clip(i - s, 0, n - 1), 0)))
        starts.append(step0)
        step0 += steps
    return specs, starts


def _whole(arr):
    return (arr, 0, arr.shape[0])


def _row_parts_value(refs, starts):
    i = pl.program_id(0)
    x = refs[0][...]
    for ref, s in zip(refs[1:], starts[1:]):
        x = jnp.where(i >= s, ref[...], x)
    return x


def _in_proj_kernel(widths, starts, *refs):
    nparts = len(starts)
    g_ref, w_ref = refs[nparts:nparts + 2]
    o_refs = refs[nparts + 2:]
    xn = _rms(_row_parts_value(refs[:nparts], starts), g_ref[...]).astype(MXU_DTYPE)
    off = 0
    for o_ref, wd in zip(o_refs, widths):
        o_ref[...] = jnp.dot(xn, w_ref[:, off:off + wd], preferred_element_type=F32)
        off += wd


def _in_proj(x_parts, g, w_cat, widths):
    d = x_parts[0].shape[1]
    t = sum(a.shape[0] for a in x_parts)
    n = w_cat.shape[1]
    x_specs, starts = _row_parts_specs([_whole(a) for a in x_parts], d)
    return pl.pallas_call(
        functools.partial(_in_proj_kernel, widths, tuple(starts)),
        grid=(t // TM_PROJ,),
        in_specs=x_specs + [_const_spec((1, d)), _const_spec((d, n))],
        out_specs=[pl.BlockSpec((TM_PROJ, wd), lambda i: (i, 0)) for wd in widths],
        out_shape=[jax.ShapeDtypeStruct((t, wd), F32) for wd in widths],
        compiler_params=_params("parallel"),
        name="in_proj",
    )(*x_parts, g, w_cat)


def _out_proj_kernel(final, gla_heads, dn_heads, starts, *refs):
    nparts = len(starts)
    (ylru_ref, gof_ref, gob_ref, ggate_ref, dof_ref, dob_ref, dgate_ref, ys5_ref, g_ref, gng_ref, dng_ref,
     wmg_ref, bmg_ref, wbr_ref, wout_ref, fg_ref, o_ref) = refs[nparts:]

    def finish(of_ref, ob_ref, gate_ref, ng_ref, heads):
        o = of_ref[...] + ob_ref[...]
        dv = o.shape[1] // heads
        normed = [_rms(o[:, h * dv:(h + 1) * dv], ng_ref[...]) for h in range(heads)]
        return jnp.concatenate(normed, axis=1) * _silu(gate_ref[...])

    x = _row_parts_value(refs[:nparts], starts)
    xn = _rms(x, g_ref[...]).astype(MXU_DTYPE)
    ys = (ylru_ref[...], finish(gof_ref, gob_ref, ggate_ref, gng_ref, gla_heads),
          finish(dof_ref, dob_ref, dgate_ref, dng_ref, dn_heads), ys5_ref[...])
    merged = None
    for n, y in enumerate(ys):
        gate = _sigmoid(jnp.dot(xn, wmg_ref[n], preferred_element_type=F32) + bmg_ref[n])
        term = gate * _mm(y, wbr_ref[n])
        merged = term if merged is None else merged + term
    out = x + _mm(merged, wout_ref[...])
    if final:
        out = _rms(out, fg_ref[...])
    o_ref[...] = out


def _out_proj(x_parts, row0, y_lru, gla, dn, y_s5, g, wmg, bmg, wbr, wout, fg, final):
    d = x_parts[0][0].shape[1]
    t = sum(rows for _, _, rows in x_parts)
    bw = y_lru.shape[1]
    blk0 = row0 // TM_PROJ
    row = lambda wd, col=0: pl.BlockSpec((TM_PROJ, wd), lambda i: (blk0 + i, col))
    g_of, g_ob, g_proj, g_col, g_ng, g_heads = gla
    d_of, d_ob, d_proj, d_col, d_ng, d_heads = dn
    x_specs, starts = _row_parts_specs(x_parts, d)
    return pl.pallas_call(
        functools.partial(_out_proj_kernel, final, g_heads, d_heads, tuple(starts)),
        grid=(t // TM_PROJ,),
        in_specs=x_specs + [row(bw), row(bw), row(bw), row(bw, g_col), row(bw), row(bw), row(bw, d_col), row(bw),
                            _const_spec((1, d)), _const_spec(g_ng.shape), _const_spec(d_ng.shape),
                            _const_spec(wmg.shape), _const_spec(bmg.shape), _const_spec(wbr.shape),
                            _const_spec(wout.shape), _const_spec((1, d))],
        out_specs=pl.BlockSpec((TM_PROJ, d), lambda i: (i, 0)),
        out_shape=jax.ShapeDtypeStruct((t, d), F32),
        compiler_params=_params("parallel"),
        name="out_proj",
    )(*[a for a, _, _ in x_parts], y_lru, g_of, g_ob, g_proj, d_of, d_ob, d_proj, y_s5, g, g_ng, d_ng,
      wmg, bmg, wbr, wout, fg)


def _halo_specs(rev, tl, width, seq_len):
    pos = _seq_pos(rev)
    per = tl // CONV_HALO
    last = seq_len // CONV_HALO - 1
    nblk = seq_len // tl
    prev = pl.BlockSpec((None, CONV_HALO, width),
                        lambda b, i: (b, jnp.maximum(pos(i, nblk) * per - 1, 0), 0))
    nxt = pl.BlockSpec((None, CONV_HALO, width),
                       lambda b, i: (b, jnp.minimum((pos(i, nblk) + 1) * per, last), 0))
    return prev, nxt


def _centred_conv(ext_ref, u, prev, nxt, at_start, at_end, w_ref):
    tl = u.shape[0]
    ext_ref[0:CONV_HALO, :] = jnp.where(at_start, 0.0, prev)
    ext_ref[CONV_HALO:CONV_HALO + tl, :] = u
    ext_ref[CONV_HALO + tl:, :] = jnp.where(at_end, 0.0, nxt)
    left = CONV_W // 2
    acc = None
    for j in range(CONV_W):
        term = ext_ref[pl.ds(CONV_HALO + j - left, tl), :] * w_ref[j:j + 1, :]
        acc = term if acc is None else acc + term
    return acc


def _lru_kernel(rev, final, tl, *refs):
    if final:
        (main_ref, prev_ref, next_ref, hf_ref, cw_ref, cb_ref, wg_ref, bg_ref, lam_ref,
         o_ref, carry_ref, ext_ref, a_scr, d_scr) = refs
    else:
        (main_ref, prev_ref, next_ref, cw_ref, cb_ref, wg_ref, bg_ref, lam_ref,
         o_ref, carry_ref, ext_ref, a_scr, d_scr) = refs
    i = pl.program_id(1)
    nblk = pl.num_programs(1)
    blk = _seq_pos(rev)(i, nblk)
    width = o_ref.shape[1]
    nslab = width // LANES
    seg = tl // SUBLANES
    pitch = seg + SEG_PAD

    @pl.when(i == 0)
    def _():
        carry_ref[...] = jnp.zeros_like(carry_ref)

    u = main_ref[:, 0:width]
    xc = _centred_conv(ext_ref, u, prev_ref[...], next_ref[...], blk == 0, blk == nblk - 1, cw_ref) + cb_ref[...]
    c_all = -LRU_C * _softplus(-lam_ref[...])
    for h in range(nslab):
        sl = slice(h * LANES, (h + 1) * LANES)
        xh = xc[:, sl]
        pre = _mm(xh, wg_ref[h]) + bg_ref[h]
        r = _sigmoid(pre[:, 0:LANES])
        ig = _sigmoid(pre[:, LANES:])
        a = jnp.exp(c_all[:, sl] * r)
        drive = jnp.sqrt(1.0 - a * a) * (ig * xh)
        for s in range(SUBLANES):
            a_scr[h, s * pitch:s * pitch + seg, :] = a[s * seg:(s + 1) * seg, :]
            d_scr[h, s * pitch:s * pitch + seg, :] = drive[s * seg:(s + 1) * seg, :]

    order = range(seg - 1, -1, -1) if rev else range(seg)
    for h in range(nslab):
        hloc = jnp.zeros((SUBLANES, LANES), F32)
        prod = jnp.ones((SUBLANES, LANES), F32)
        for k in order:
            ak = a_scr[h, pl.ds(k, SUBLANES, stride=pitch), :]
            dk = d_scr[h, pl.ds(k, SUBLANES, stride=pitch), :]
            hloc = ak * hloc + dk
            prod = prod * ak
            d_scr[h, pl.ds(k, SUBLANES, stride=pitch), :] = hloc
            a_scr[h, pl.ds(k, SUBLANES, stride=pitch), :] = prod

    last = 0 if rev else seg - 1
    seg_order = range(SUBLANES - 1, -1, -1) if rev else range(SUBLANES)
    for h in range(nslab):
        sl = slice(h * LANES, (h + 1) * LANES)
        c = carry_ref[0:1, sl]
        for s in seg_order:
            rows = slice(s * seg, (s + 1) * seg)
            hs = d_scr[h, s * pitch:s * pitch + seg, :] + a_scr[h, s * pitch:s * pitch + seg, :] * c
            if final:
                gate = main_ref[rows, width + h * LANES:width + (h + 1) * LANES]
                o_ref[rows, sl] = (hf_ref[rows, sl] + hs) * _silu(gate)
            else:
                o_ref[rows, sl] = hs
            c = a_scr[h, pl.ds(s * pitch + last, 1), :] * c + d_scr[h, pl.ds(s * pitch + last, 1), :]
        carry_ref[0:1, sl] = c


def _lru_dir(rev, proj, hf, cw, cb, wg, bg, lam):
    nb, seq_len, two_w = proj.shape
    width = two_w // 2
    tl = TL_MIX
    nblk = seq_len // tl
    pos = _seq_pos(rev)
    blk_spec = lambda wd: pl.BlockSpec((None, tl, wd), lambda b, i: (b, pos(i, nblk), 0))
    prev, nxt = _halo_specs(rev, tl, width, seq_len)
    final = hf is not None
    pitch = tl // SUBLANES + SEG_PAD
    ins = [proj, proj, proj] + ([hf] if final else []) + [cw, cb, wg, bg, lam]
    specs = [blk_spec(two_w), prev, nxt] + ([blk_spec(width)] if final else []) + [
        _const_spec(cw.shape), _const_spec(cb.shape), _const_spec(wg.shape), _const_spec(bg.shape),
        _const_spec(lam.shape)]
    return pl.pallas_call(
        functools.partial(_lru_kernel, rev, final, tl),
        grid=(nb, nblk),
        in_specs=specs,
        out_specs=blk_spec(width),
        out_shape=jax.ShapeDtypeStruct((nb, seq_len, width), F32),
        scratch_shapes=[pltpu.VMEM((SUBLANES, width), F32),
                        pltpu.VMEM((tl + 2 * CONV_HALO, width), F32),
                        pltpu.VMEM((width // LANES, SUBLANES * pitch, LANES), F32),
                        pltpu.VMEM((width // LANES, SUBLANES * pitch, LANES), F32)],
        compiler_params=_params("parallel", "arbitrary"),
        name="lru_bwd" if rev else "lru_fwd",
    )(*ins)


def _gla_kernel(tl, heads, dk, dv, mf_ref, mb_ref, sf_ref, sb_ref, wup_ref, bup_ref, cumf_ref, cumb_ref,
                of_ref, ob_ref, st_ref):
    i = pl.program_id(1)
    hk = heads * dk
    nchunk = tl // CHUNK

    @pl.when(i == 0)
    def _():
        st_ref[...] = jnp.zeros_like(st_ref)

    hv = heads * dv
    kk_blocks = _block_mask(heads * CHUNK, hk, CHUNK, dk).astype(MXU_DTYPE)
    kv_blocks = _block_mask(heads * CHUNK, hv, CHUNK, dv).astype(MXU_DTYPE)
    st_blocks = _block_mask(hv, hk, dv, dk).astype(F32)
    dirs = ((False, mf_ref, sf_ref, cumf_ref, of_ref), (True, mb_ref, sb_ref, cumb_ref, ob_ref))
    for d, (rev, m_ref, s_ref, cum_ref, o_ref) in enumerate(dirs):
        incl = _tri_mask_heads(rev, False, heads)
        end = 0 if rev else CHUNK - 1
        pre = _mm(s_ref[...], wup_ref[d]) + bup_ref[d]
        la = (jnp.minimum(pre, 0.0) - jnp.log1p(jnp.exp(-jnp.abs(pre)))) * (1.0 / GLA_TAU)
        b = _cum_dot(cum_ref[...], la)
        q_dec = (m_ref[:, 0:hk] * (dk ** -0.5) * jnp.exp(b)).astype(MXU_DTYPE)
        k = m_ref[:, hk:2 * hk]
        k_inv = (k * jnp.exp(-b)).astype(MXU_DTYPE)
        state = st_ref[d]
        for c in _chunk_order(rev, nchunk):
            rows = slice(c * CHUNK, (c + 1) * CHUNK)
            b_end = b[c * CHUNK + end:c * CHUNK + end + 1, :]
            k_end = k[rows, :] * jnp.exp(b_end - b[rows, :])
            v = m_ref[rows, 2 * hk:2 * hk + hv].astype(MXU_DTYPE)
            k_exp = jnp.concatenate([k_inv[rows, :]] * heads, axis=0) * kk_blocks
            v_exp = jnp.concatenate([v] * heads, axis=0) * kv_blocks
            scores = jnp.where(incl, _mm_nt(q_dec[rows, :], k_exp), 0.0)
            o_ref[rows, :] = _mm(scores, v_exp) + _mm_nt(q_dec[rows, :], state)
            state = state * jnp.exp(b_end) + _mm_tn(v, k_end) * st_blocks
        st_ref[d] = state


def _gla_both(proj, small, wup, bup, heads, dk, dv):
    nb, seq_len, pw = proj.shape
    tl = TL_MIX
    nblk = seq_len // tl
    fwd = lambda wd: pl.BlockSpec((None, tl, wd), lambda b, i: (b, i, 0))
    bwd = lambda wd: pl.BlockSpec((None, tl, wd), lambda b, i: (b, nblk - 1 - i, 0))
    hv = heads * dv
    sw = small.shape[2]
    cum_f, cum_b = _block_cum(False, tl), _block_cum(True, tl)
    out = jax.ShapeDtypeStruct((nb, seq_len, hv), F32)
    return pl.pallas_call(
        functools.partial(_gla_kernel, tl, heads, dk, dv),
        grid=(nb, nblk),
        in_specs=[fwd(pw), bwd(pw), fwd(sw), bwd(sw), _const_spec(wup.shape), _const_spec(bup.shape),
                  _const_spec(cum_f.shape), _const_spec(cum_b.shape)],
        out_specs=[fwd(hv), bwd(hv)],
        out_shape=[out, out],
        scratch_shapes=[pltpu.VMEM((2, heads * dv, heads * dk), F32)],
        compiler_params=_params("parallel", "arbitrary"),
        name="gla",
    )(proj, proj, small, small, wup, bup, cum_f, cum_b)


def _dn_gates(small, gc_ref):
    beta = _sigmoid(small)
    g = -jnp.exp(gc_ref[0:1, :]) * _softplus(small + gc_ref[1:2, :])
    return jnp.where(gc_ref[2:3, :] > 0.5, beta, jnp.where(gc_ref[3:4, :] > 0.5, g, 0.0))


def _dn_forms(gx, d, heads):
    assert 2 * heads == SUBLANES and 2 * CHUNK == LANES
    base = 2 * heads * d
    rows8 = gx.T[base:base + SUBLANES, :]
    rows8_rot = pltpu.roll(rows8, CHUNK, axis=1)
    rep = lambda lane: jnp.broadcast_to(gx[:, lane:lane + 1], (gx.shape[0], LANES))
    return rows8, rows8_rot, [rep(base + h) for h in range(heads)], [rep(base + heads + h) for h in range(heads)]


def _head_rows(rows8, rows8_rot, r0, c, heads):
    nchunk = rows8.shape[1] // CHUNK
    low_half = lax.broadcasted_iota(jnp.int32, (1, LANES), 1) < CHUNK
    out = []
    for j in range(heads // 2):
        halves = []
        for h in (2 * j, 2 * j + 1):
            src, slot = (rows8, c) if c % 2 == h % 2 else (rows8_rot, (c + 1) % nchunk)
            halves.append(src[r0 + h:r0 + h + 1, LANES * (slot // 2):LANES * (slot // 2 + 1)])
        out.append(jnp.where(low_half, halves[0], halves[1]))
    return jnp.concatenate(out, axis=1)


def _head_cols(rep, c, heads):
    low_half = lax.broadcasted_iota(jnp.int32, (1, LANES), 1) < CHUNK
    rows = slice(c * CHUNK, (c + 1) * CHUNK)
    return jnp.concatenate([jnp.where(low_half, rep[2 * j][rows, :], rep[2 * j + 1][rows, :])
                            for j in range(heads // 2)], axis=1)


def _dn_gamma(rev, forms, c, heads):
    rows8, rows8_rot, _, g_rep = forms
    incl = _tri_mask_heads(rev, False, heads)
    diff = _head_cols(g_rep, c, heads) - _head_rows(rows8, rows8_rot, heads, c, heads)
    return jnp.where(incl, jnp.exp(jnp.where(incl, diff, 0.0)), 0.0)


def _head_expand(x, blocks, heads):
    return jnp.concatenate([x.astype(MXU_DTYPE)] * heads, axis=0) * blocks


def _dn_prep_kernel(tl, heads, dh, main_ref, prev_ref, next_ref, small_ref, cw_ref, gcst_ref, cumf_ref, cumb_ref,
                    qkv_ref, gx_ref, af_ref, ab_ref, ext_ref):
    i = pl.program_id(1)
    nblk = pl.num_programs(1)
    hw = heads * dh
    x = main_ref[:, 0:3 * hw]
    y = _silu(_centred_conv(ext_ref, x, prev_ref[...], next_ref[...], i == 0, i == nblk - 1, cw_ref))
    for h in range(heads):
        for part, scale in ((0, dh ** -0.5), (1, 1.0)):
            sl = slice(part * hw + h * dh, part * hw + (h + 1) * dh)
            z = y[:, sl]
            qkv_ref[:, sl] = z * lax.rsqrt(jnp.sum(z * z, axis=-1, keepdims=True) + NORM_EPS) * scale
    qkv_ref[:, 2 * hw:] = y[:, 2 * hw:]
    bg = _dn_gates(small_ref[...], gcst_ref)
    lane = lax.broadcasted_iota(jnp.int32, bg.shape, 1)
    run = jnp.where(lane < 2 * heads, _cum_dot(cumf_ref[...], bg), _cum_dot(cumb_ref[...], bg))
    gx = jnp.where(gcst_ref[3:4, :] > 0.5, run, bg)
    gx_ref[...] = gx
    blocks = _block_mask(heads * CHUNK, hw, CHUNK, dh).astype(MXU_DTYPE)
    for d, (rev, a_ref) in enumerate(((False, af_ref), (True, ab_ref))):
        strict = _tri_mask_heads(rev, True, heads)
        forms = _dn_forms(gx, d, heads)
        for c in range(tl // CHUNK):
            k = qkv_ref[c * CHUNK:(c + 1) * CHUNK, hw:2 * hw]
            kk = _mm_nt(k, _head_expand(k, blocks, heads))
            a_ref[c] = jnp.where(strict, kk * _head_cols(forms[2], c, heads) * _dn_gamma(rev, forms, c, heads), 0.0)


def _dn_prep(proj, small, cw, gate_consts, heads, dh):
    nb, seq_len, pw = proj.shape
    tl = TL_MIX
    nblk = seq_len // tl
    hw = heads * dh
    blk_spec = lambda wd: pl.BlockSpec((None, tl, wd), lambda b, i: (b, i, 0))
    prev, nxt = _halo_specs(False, tl, 3 * hw, seq_len)
    a_spec = pl.BlockSpec((None, tl // CHUNK, CHUNK, heads * CHUNK), lambda b, i: (b, i, 0, 0))
    a_shape = jax.ShapeDtypeStruct((nb, seq_len // CHUNK, CHUNK, heads * CHUNK), F32)
    lanes_shape = jax.ShapeDtypeStruct((nb, seq_len, LANES), F32)
    cum_f, cum_b = _block_cum(False, tl), _block_cum(True, tl)
    return pl.pallas_call(
        functools.partial(_dn_prep_kernel, tl, heads, dh),
        grid=(nb, nblk),
        in_specs=[blk_spec(pw), prev, nxt, blk_spec(small.shape[2]), _const_spec(cw.shape),
                  _const_spec(gate_consts.shape), _const_spec(cum_f.shape), _const_spec(cum_b.shape)],
        out_specs=[blk_spec(3 * hw), blk_spec(LANES), a_spec, a_spec],
        out_shape=[jax.ShapeDtypeStruct((nb, seq_len, 3 * hw), F32), lanes_shape, a_shape, a_shape],
        scratch_shapes=[pltpu.VMEM((tl + 2 * CONV_HALO, 3 * hw), F32)],
        compiler_params=_params("parallel", "parallel"),
        name="dn_prep",
    )(proj, proj, proj, small, cw, gate_consts, cum_f, cum_b)


def _dn_solve_kernel(upper, heads, a_ref, t_ref, a_scr, t_scr):
    nl = a_ref.shape[0]
    hc = heads * CHUNK
    half = CHUNK // 2
    for t in range(CHUNK):
        a_scr[t] = a_ref[:, t * hc:(t + 1) * hc].T.reshape(heads, CHUNK, nl)
    col = lax.broadcasted_iota(jnp.int32, (half, nl), 0)

    def row(idx, carry):
        t = (CHUNK - 1 - idx) if upper else idx

        def term_lo(s, accs):
            return tuple(acc - a_scr[t, h, pl.ds(s, 1), :] * t_scr[s, h, 0:half, :] for h, acc in enumerate(accs))

        def term_hi(s, accs):
            return tuple(acc - a_scr[t, h, pl.ds(s, 1), :] * t_scr[s, h, half:, :] for h, acc in enumerate(accs))

        if upper:
            lo_range, hi_range = (t + 1, jnp.maximum(t + 1, half)), (t + 1, CHUNK)
        else:
            lo_range, hi_range = (0, t), (half, jnp.maximum(t, half))
        lo = lax.fori_loop(*lo_range, term_lo, (jnp.where(col == t, 1.0, 0.0),) * heads)
        hi = lax.fori_loop(*hi_range, term_hi, (jnp.where(col + half == t, 1.0, 0.0),) * heads)
        for h in range(heads):
            t_scr[t, h, 0:half, :] = lo[h]
            t_scr[t, h, half:, :] = hi[h]
        return carry

    lax.fori_loop(0, CHUNK, row, 0)
    for t in range(CHUNK):
        t_ref[:, t * hc:(t + 1) * hc] = t_scr[t].reshape(hc, nl).T.astype(t_ref.dtype)


def _dn_solve(a, upper, heads):
    nb, nc = a.shape[:2]
    flat = a.reshape(nb * nc, CHUNK * heads * CHUNK)
    nl = min(SOLVE_CHUNKS, nb * nc)
    spec = pl.BlockSpec((nl, flat.shape[1]), lambda i: (i, 0))
    scr = pltpu.VMEM((CHUNK, heads, CHUNK, nl), F32)
    return pl.pallas_call(
        functools.partial(_dn_solve_kernel, upper, heads),
        grid=(nb * nc // nl,),
        in_specs=[spec],
        out_specs=spec,
        out_shape=jax.ShapeDtypeStruct(flat.shape, MXU_DTYPE),
        scratch_shapes=[scr, scr],
        compiler_params=_params("parallel"),
        name="dn_solve_bwd" if upper else "dn_solve_fwd",
    )(flat).reshape(a.shape)


def _dn_main_kernel(tl, heads, dh, qf_ref, gxf_ref, tf_ref, qb_ref, gxb_ref, tb_ref, of_ref, ob_ref, s_ref):
    i = pl.program_id(1)
    hw = heads * dh
    nchunk = tl // CHUNK
    npair = heads // 2
    pw = 2 * dh

    @pl.when(i == 0)
    def _():
        s_ref[...] = jnp.zeros_like(s_ref)

    blocks = _block_mask(heads * CHUNK, hw, CHUNK, dh).astype(MXU_DTYPE)
    t_blocks = _block_mask(heads * CHUNK, heads * CHUNK, CHUNK, CHUNK).astype(MXU_DTYPE)
    pair_blocks = _block_mask(pw, pw, dh, dh).astype(F32)
    stack = lambda x: jnp.concatenate([x[:, h * dh:(h + 1) * dh] for h in range(heads)], axis=0)
    dirs = ((False, qf_ref, gxf_ref, tf_ref, of_ref), (True, qb_ref, gxb_ref, tb_ref, ob_ref))
    for d, (rev, qkv_ref, gx_ref, t_ref, o_ref) in enumerate(dirs):
        incl = _tri_mask_heads(rev, False, heads)
        end = 0 if rev else CHUNK - 1
        forms = _dn_forms(gx_ref[...], d, heads)
        b_nat = jnp.concatenate(forms[2], axis=1)
        g_nat = jnp.concatenate(forms[3], axis=1)
        eg_nat = jnp.exp(g_nat)
        states = [s_ref[d, j] for j in range(npair)]
        for c in _chunk_order(rev, nchunk):
            rows = slice(c * CHUNK, (c + 1) * CHUNK)
            q = qkv_ref[rows, 0:hw]
            k = qkv_ref[rows, hw:2 * hw]
            v = qkv_ref[rows, 2 * hw:3 * hw]
            eg = eg_nat[rows, :]
            attn = jnp.where(incl, _mm_nt(q, _head_expand(k, blocks, heads)) * _dn_gamma(rev, forms, c, heads), 0.0)
            beta = b_nat[rows, :]
            t_bd = _head_expand(t_ref[c], t_blocks, heads)
            wu = _mm(t_bd, jnp.concatenate([stack(k * beta * eg), stack(v * beta)], axis=1))
            w = jnp.concatenate([wu[h * CHUNK:(h + 1) * CHUNK, 0:dh] for h in range(heads)], axis=1)
            u = jnp.concatenate([wu[h * CHUNK:(h + 1) * CHUNK, dh:] for h in range(heads)], axis=1)
            lhs = jnp.concatenate([w, q * eg], axis=0)
            xs = [_mm(lhs[:, j * pw:(j + 1) * pw], states[j]) for j in range(npair)]
            v_new = u - jnp.concatenate([x[0:CHUNK, :] for x in xs], axis=1)
            o_ref[rows, :] = (jnp.concatenate([x[CHUNK:, :] for x in xs], axis=1)
                              + _mm(attn, _head_expand(v_new, blocks, heads)))
            g_end = g_nat[c * CHUNK + end:c * CHUNK + end + 1, :]
            k_end = k * jnp.exp(g_end - g_nat[rows, :])
            e_end = jnp.exp(g_end)
            for j in range(npair):
                ps = slice(j * pw, (j + 1) * pw)
                states[j] = states[j] * e_end[:, ps] + _mm_tn(k_end[:, ps], v_new[:, ps]) * pair_blocks
        for j in range(npair):
            s_ref[d, j] = states[j]


def _dn_main(qkv, gx, t_f, t_b, heads, dh):
    nb, seq_len, _ = qkv.shape
    tl = TL_MIX
    nblk = seq_len // tl
    hw = heads * dh
    fwd = lambda wd: pl.BlockSpec((None, tl, wd), lambda b, i: (b, i, 0))
    bwd = lambda wd: pl.BlockSpec((None, tl, wd), lambda b, i: (b, nblk - 1 - i, 0))
    t_blk = (None, tl // CHUNK, CHUNK, heads * CHUNK)
    t_fwd = pl.BlockSpec(t_blk, lambda b, i: (b, i, 0, 0))
    t_bwd = pl.BlockSpec(t_blk, lambda b, i: (b, nblk - 1 - i, 0, 0))
    out = jax.ShapeDtypeStruct((nb, seq_len, hw), F32)
    return pl.pallas_call(
        functools.partial(_dn_main_kernel, tl, heads, dh),
        grid=(nb, nblk),
        in_specs=[fwd(3 * hw), fwd(LANES), t_fwd, bwd(3 * hw), bwd(LANES), t_bwd],
        out_specs=[fwd(hw), bwd(hw)],
        out_shape=[out, out],
        scratch_shapes=[pltpu.VMEM((2, heads // 2, 2 * dh, 2 * dh), F32)],
        compiler_params=_params("parallel", "arbitrary"),
        name="dn_main",
    )(qkv, gx, t_f, qkv, gx, t_b)


def _dn_branch(proj, small, cw, gate_consts, heads, dh):
    qkv, gx, a_f, a_b = _dn_prep(proj, small, cw, gate_consts, heads, dh)
    return _dn_main(qkv, gx, _dn_solve(a_f, False, heads), _dn_solve(a_b, True, heads), heads, dh)


S5_CHUNK = 16


def _cpow(lam_re, lam_im, dt, e):
    mag = jnp.exp(lam_re * dt * e)
    ang = lam_im * dt * e
    return mag * jnp.cos(ang), mag * jnp.sin(ang)


def _s5_wt_kernel(gw, lre_ref, lim_ref, ldt_ref, bre_ref, bim_ref, ctre_ref, ctim_ref, cre_ref, cim_ref,
                  k_ref, zwre_ref, zwim_ref, cwre_ref, cwimn_ref):
    backward = pl.program_id(0) == 1
    lam_re, lam_im, dt = lre_ref[...], lim_ref[...], jnp.exp(ldt_ref[...])
    a_re, a_im = _cpow(lam_re, lam_im, dt, 1.0)
    den = lam_re * lam_re + lam_im * lam_im
    n_re = a_re - 1.0
    f_re = (n_re * lam_re + a_im * lam_im) / den
    f_im = (a_im * lam_re - n_re * lam_im) / den
    bb_re = f_re * bre_ref[...] - f_im * bim_ref[...]
    bb_im = f_re * bim_ref[...] + f_im * bre_ref[...]
    pos = jnp.right_shift(lax.broadcasted_iota(jnp.int32, lam_re.shape, 1), _log2(gw)).astype(F32)
    p_re, p_im = _cpow(lam_re, lam_im, dt, pos)
    e_re = p_re * bb_re - p_im * bb_im
    e_im = p_re * bb_im + p_im * bb_re
    k_ref[...] = (jnp.dot(cre_ref[...], e_re, precision=HI, preferred_element_type=F32)
                  - jnp.dot(cim_ref[...], e_im, precision=HI, preferred_element_type=F32))
    z_re, z_im = _cpow(lam_re, lam_im, dt, jnp.where(backward, pos, (S5_CHUNK - 1.0) - pos))
    zwre_ref[...] = z_re * bb_re - z_im * bb_im
    zwim_ref[...] = z_re * bb_im + z_im * bb_re
    c_re, c_im = _cpow(lam_re, lam_im, dt, jnp.where(backward, S5_CHUNK - pos, pos + 1.0))
    cwre_ref[...] = ctre_ref[...] * c_re - ctim_ref[...] * c_im
    cwimn_ref[...] = -(ctre_ref[...] * c_im + ctim_ref[...] * c_re)


def _s5_weights(lam_re, lam_im, log_dt, b_re, b_im, c_re, c_im):
    _, groups, states, gw = b_re.shape
    wl = S5_CHUNK * gw
    rep = lambda a: jnp.broadcast_to(a[..., None], (2, groups, states, wl))
    tile = lambda a: jnp.tile(a, (1, 1, 1, S5_CHUNK))
    blk = lambda r, c: pl.BlockSpec((None, None, r, c), lambda d, g: (d, g, 0, 0))
    ct = lambda a: tile(a.transpose(0, 1, 3, 2))
    big = jax.ShapeDtypeStruct((2, groups, states, wl), F32)
    return pl.pallas_call(
        functools.partial(_s5_wt_kernel, gw),
        grid=(2, groups),
        in_specs=[blk(states, wl)] * 7 + [blk(gw, states)] * 2,
        out_specs=[blk(gw, wl)] + [blk(states, wl)] * 4,
        out_shape=[jax.ShapeDtypeStruct((2, groups, gw, wl), F32), big, big, big, big],
        compiler_params=_params("parallel", "parallel"),
        name="s5_wt",
    )(rep(lam_re), rep(lam_im), rep(jnp.broadcast_to(log_dt[..., None], lam_re.shape)), tile(b_re), tile(b_im),
      ct(c_re), ct(c_im), c_re, c_im)


def _s5_pow_kernel(seg, lre_ref, lim_ref, ldt_ref, pw_ref):
    a_re, a_im = _cpow(lre_ref[...], lim_ref[...], jnp.exp(ldt_ref[...]), float(S5_CHUNK))
    x_re, x_im = a_re, a_im
    for k in range(seg):
        pw_ref[0, k] = x_re
        pw_ref[1, k] = x_im
        x_re, x_im = x_re * a_re - x_im * a_im, x_re * a_im + x_im * a_re


def _s5_pow(lre, lim, ldt, seg):
    return pl.pallas_call(
        functools.partial(_s5_pow_kernel, seg),
        out_shape=jax.ShapeDtypeStruct((2, seg) + lre.shape, F32),
        name="s5_pow",
    )(lre, lim, ldt)


def _gelu_tanh(x):
    return 0.5 * x * (1.0 + jnp.tanh(math.sqrt(2.0 / math.pi) * (x + 0.044715 * (x * x * x))))


def _s5_mix_kernel(nr, u_ref, tf_ref, tb_ref, zw_ref, cw_ref, pw_ref, y_ref, pad_re, pad_im, f_re, f_im, b_re, b_im):
    seg = nr // SUBLANES
    pitch = seg + SEG_PAD
    half = LANES // 2
    fwd_lanes = lax.broadcasted_iota(jnp.int32, (1, LANES), 1) < half
    w1 = jnp.concatenate([(tf_ref[...] + tb_ref[...]).astype(MXU_DTYPE), zw_ref[...]], axis=1)
    yz = jnp.dot(u_ref[...], w1, preferred_element_type=F32)
    wl = tf_ref.shape[1]

    for pad, col in ((pad_re, wl), (pad_im, wl + LANES)):
        pad[0:SUBLANES, :] = jnp.zeros((SUBLANES, LANES), F32)
        pad[SUBLANES:SUBLANES + nr, :] = yz[:, col:col + LANES]
        pad[SUBLANES + nr:, :] = jnp.zeros((SUBLANES, LANES), F32)
    for s in range(SUBLANES):
        dst = slice(s * pitch, s * pitch + seg)
        for pad, fw, bw in ((pad_re, f_re, b_re), (pad_im, f_im, b_im)):
            fw[dst, :] = pad[pl.ds(SUBLANES - 1 + s * seg, seg), :]
            bw[dst, :] = pad[pl.ds(SUBLANES + 1 + s * seg, seg), :]

    a_re = jnp.broadcast_to(pw_ref[0, 0:1, :], (SUBLANES, LANES))
    a_im = jnp.broadcast_to(pw_ref[1, 0:1, :], (SUBLANES, LANES))
    for re_scr, im_scr, order in ((f_re, f_im, range(seg)), (b_re, b_im, range(seg - 1, -1, -1))):
        sr = jnp.zeros((SUBLANES, LANES), F32)
        si = jnp.zeros((SUBLANES, LANES), F32)
        for k in order:
            sr, si = (a_re * sr - a_im * si + re_scr[pl.ds(k, SUBLANES, stride=pitch), :],
                      a_re * si + a_im * sr + im_scr[pl.ds(k, SUBLANES, stride=pitch), :])
            re_scr[pl.ds(k, SUBLANES, stride=pitch), :] = sr
            im_scr[pl.ds(k, SUBLANES, stride=pitch), :] = si

    pr, pi = pw_ref[2], pw_ref[3]
    sg_re = pw_ref[0, seg - 1:seg, :]
    sg_im = pw_ref[1, seg - 1:seg, :]
    zero = jnp.zeros((1, LANES), F32)
    cf, cb = [(zero, zero)], [(zero, zero)]
    for s in range(SUBLANES - 1):
        er, ei = f_re[pl.ds(s * pitch + seg - 1, 1), :], f_im[pl.ds(s * pitch + seg - 1, 1), :]
        cr, ci = cf[-1]
        cf.append((sg_re * cr - sg_im * ci + er, sg_re * ci + sg_im * cr + ei))
        t = SUBLANES - 1 - s
        er, ei = b_re[pl.ds(t * pitch, 1), :], b_im[pl.ds(t * pitch, 1), :]
        cr, ci = cb[-1]
        cb.append((sg_re * cr - sg_im * ci + er, sg_re * ci + sg_im * cr + ei))
    cb = cb[::-1]
    parts_re, parts_im = [], []
    for s in range(SUBLANES):
        src = slice(s * pitch, s * pitch + seg)
        cr = jnp.where(fwd_lanes, cf[s][0], cb[s][0])
        ci = jnp.where(fwd_lanes, cf[s][1], cb[s][1])
        parts_re.append(jnp.where(fwd_lanes, f_re[src, :], b_re[src, :]) + pr * cr - pi * ci)
        parts_im.append(jnp.where(fwd_lanes, f_im[src, :], b_im[src, :]) + pr * ci + pi * cr)
    states = jnp.concatenate([jnp.concatenate(parts_re, axis=0), jnp.concatenate(parts_im, axis=0)], axis=1)
    y_ref[...] = yz[:, 0:wl] + _mm(states, cw_ref[...])


def _s5_mix(u_g, toep_f, toep_b, zw, cw, pw):
    groups, nb, nr, wl = u_g.shape
    seg = nr // SUBLANES
    pitch = seg + SEG_PAD
    per_g = lambda r, c: pl.BlockSpec((None, r, c), lambda g, b: (g, 0, 0))
    rows = pl.BlockSpec((None, None, nr, wl), lambda g, b: (g, b, 0, 0))
    scan = pltpu.VMEM((SUBLANES * pitch, LANES), F32)
    pad = pltpu.VMEM((nr + 2 * SUBLANES, LANES), F32)
    return pl.pallas_call(
        functools.partial(_s5_mix_kernel, nr),
        grid=(groups, nb),
        in_specs=[rows, per_g(wl, wl), per_g(wl, wl), per_g(wl, 2 * LANES), per_g(2 * LANES, wl),
                  pl.BlockSpec((None, 4, seg, LANES), lambda g, b: (g, 0, 0, 0))],
        out_specs=rows,
        out_shape=jax.ShapeDtypeStruct((groups, nb, nr, wl), F32),
        scratch_shapes=[pad, pad, scan, scan, scan, scan],
        compiler_params=_params("parallel", "parallel"),
        name="s5_mix",
    )(u_g, toep_f, toep_b, zw, cw, pw)


def _s5_to_groups_kernel(tl, gw, x_ref, o_ref, xs, acc):
    per = LANES // gw
    nrow = tl // S5_CHUNK
    nslab = x_ref.shape[1] // LANES
    for v in range(nslab):
        xs[v] = x_ref[:, v * LANES:(v + 1) * LANES]
    for r in range(S5_CHUNK):
        dst = (r // per) * LANES + (r % per) * gw
        for v in range(nslab):
            rows = xs[v, pl.ds(r, nrow, stride=S5_CHUNK), :]
            for s in range(per):
                rot = rows if s == 0 else pltpu.roll(rows, s * gw, axis=1)
                g = per * v + (r - s) % per
                acc[g, :, dst:dst + gw] = rot[:, (r % per) * gw:(r % per + 1) * gw]
    o_ref[...] = acc[...].astype(o_ref.dtype)


def _s5_to_groups(proj, groups, gw):
    nb, seq_len, two_w = proj.shape
    width = two_w // 2
    tl = TL_MIX
    nrow = tl // S5_CHUNK
    wl = S5_CHUNK * gw
    return pl.pallas_call(
        functools.partial(_s5_to_groups_kernel, tl, gw),
        grid=(nb, seq_len // tl),
        in_specs=[pl.BlockSpec((None, tl, width), lambda b, i: (b, i, 0))],
        out_specs=pl.BlockSpec((groups, None, nrow, wl), lambda b, i: (0, b, i, 0)),
        out_shape=jax.ShapeDtypeStruct((groups, nb, seq_len // S5_CHUNK, wl), MXU_DTYPE),
        scratch_shapes=[pltpu.VMEM((width // LANES, tl, LANES), F32), pltpu.VMEM((groups, nrow, wl), F32)],
        compiler_params=_params("parallel", "parallel"),
        name="s5_to_groups",
    )(proj)


def _s5_finish_kernel(tl, gw, y_ref, main_ref, dsk_ref, wglu_ref, bglu_ref, o_ref, tmp, ynat):
    per = LANES // gw
    nrow = tl // S5_CHUNK
    width = o_ref.shape[1]
    nslab = width // LANES
    for g in range(y_ref.shape[0]):
        v, slot = g // per, g % per
        for h in range(S5_CHUNK // per):
            rows = y_ref[g, :, h * LANES:(h + 1) * LANES]
            for s in range(per):
                rot = rows if s == 0 else pltpu.roll(rows, s * gw, axis=1)
                r = h * per + (slot - s) % per
                tmp[v, r * nrow:(r + 1) * nrow, slot * gw:(slot + 1) * gw] = rot[:, slot * gw:(slot + 1) * gw]
    for n in range(nrow):
        for r0 in range(0, S5_CHUNK, SUBLANES):
            for v in range(nslab):
                ynat[n * S5_CHUNK + r0:n * S5_CHUNK + r0 + SUBLANES, v * LANES:(v + 1) * LANES] = (
                    tmp[v, pl.ds(r0 * nrow + n, SUBLANES, stride=nrow), :])
    y = ynat[...] + dsk_ref[...] * main_ref[:, 0:width]
    z = _gelu_tanh(y)
    z = z * _sigmoid(_mm(z, wglu_ref[...]) + bglu_ref[...])
    o_ref[...] = z * _silu(main_ref[:, width:])


def _s5_finish(y_g, proj, dsk, wglu, bglu, gw):
    groups, nb, _, wl = y_g.shape
    _, seq_len, two_w = proj.shape
    width = two_w // 2
    tl = TL_MIX
    nrow = tl // S5_CHUNK
    return pl.pallas_call(
        functools.partial(_s5_finish_kernel, tl, gw),
        grid=(nb, seq_len // tl),
        in_specs=[pl.BlockSpec((groups, None, nrow, wl), lambda b, i: (0, b, i, 0)),
                  pl.BlockSpec((None, tl, two_w), lambda b, i: (b, i, 0)),
                  _const_spec(dsk.shape), _const_spec(wglu.shape), _const_spec(bglu.shape)],
        out_specs=pl.BlockSpec((None, tl, width), lambda b, i: (b, i, 0)),
        out_shape=jax.ShapeDtypeStruct((nb, seq_len, width), F32),
        scratch_shapes=[pltpu.VMEM((width // LANES, tl, LANES), F32), pltpu.VMEM((tl, width), F32)],
        compiler_params=_params("parallel", "parallel"),
        name="s5_finish",
    )(y_g, proj, dsk, wglu, bglu)


def _s5_toeplitz_branch(proj, lam_re, lam_im, log_dt, b_re, b_im, c_re, c_im, dsk, wglu, bglu):
    nb, seq_len, two_w = proj.shape
    width = two_w // 2
    _, groups, states, gw = b_re.shape
    nr = seq_len // S5_CHUNK
    seg = nr // SUBLANES
    wl = S5_CHUNK * gw
    kk, zw_re, zw_im, cw_re, cw_imn = _s5_weights(lam_re, lam_im, log_dt, b_re, b_im, c_re, c_im)

    lag = jnp.arange(S5_CHUNK)[None, :] - jnp.arange(S5_CHUNK)[:, None]
    kt = kk.reshape(2, groups, gw, S5_CHUNK, gw).transpose(0, 1, 3, 4, 2)

    def toeplitz(d, lag_d):
        t = jnp.where((lag_d >= 0)[None, :, :, None, None], kt[d][:, jnp.maximum(lag_d, 0)], 0.0)
        return t.transpose(0, 1, 3, 2, 4).reshape(groups, wl, wl)

    toep_f, toep_b = toeplitz(0, lag), toeplitz(1, -lag)
    by_rows = lambda a: a.transpose(0, 2, 1)
    zw = jnp.concatenate([by_rows(zw_re[0]), by_rows(zw_re[1]), by_rows(zw_im[0]), by_rows(zw_im[1])],
                         axis=2).astype(MXU_DTYPE)
    cw = jnp.concatenate([cw_re[0], cw_re[1], cw_imn[0], cw_imn[1]], axis=1).astype(MXU_DTYPE)

    flat = lambda a: a.reshape(2, groups * states // LANES, LANES)
    pw = _s5_pow(flat(lam_re), flat(lam_im), flat(jnp.broadcast_to(log_dt[..., None], lam_re.shape)), seg)
    pw = pw.reshape(2, seg, 2, groups, states).transpose(2, 3, 0, 1, 4)
    plain = jnp.concatenate([pw[0], pw[1]], axis=3)
    by_row = jnp.concatenate([pw[0], jnp.flip(pw[1], axis=2)], axis=3)
    pw_all = jnp.concatenate([plain, by_row], axis=1)

    y_g = _s5_mix(_s5_to_groups(proj, groups, gw), toep_f, toep_b, zw, cw, pw_all)
    return _s5_finish(y_g, proj, dsk, wglu, bglu, gw)


def _layer(x_parts, group_rows, seq_len, p, final_g, final):
    d = x_parts[0].shape[1]
    nb = sum(a.shape[0] for a in x_parts) // seq_len
    bw = d // 2
    bf = lambda a: a.astype(MXU_DTYPE)
    row = lambda a: a.reshape(1, -1)

    lru_blocks = p['lru_w_a'].shape[1]
    gla_heads = GLA_HEADS
    gla_dk = p['gla_w_up'].shape[2] // gla_heads
    gla_dv = bw // gla_heads
    gla_rank = p['gla_w_up'].shape[1]
    dn_heads = p['dn_a_log'].shape[1]
    dn_dh = bw // dn_heads
    widths = (bw, bw, gla_heads * gla_dk, gla_heads * gla_dk, bw, bw, 2 * gla_rank, 3 * bw, bw, 4 * dn_heads, bw, bw)
    offs = [0]
    for wd in widths:
        offs.append(offs[-1] + wd)
    w_in = p['w_in']
    cols = lambda a, b: w_in[:, offs[a]:offs[b]]
    n_gate = 4 * dn_heads
    n_small = n_gate + 2 * gla_rank
    w_small = jnp.concatenate([cols(9, 10), cols(6, 7), jnp.zeros((d, LANES - n_small), w_in.dtype)], axis=1)
    w_cat = bf(jnp.concatenate([cols(0, 2), cols(2, 6), cols(7, 9), cols(10, 12), w_small], axis=1))
    out_w = (2 * bw, offs[6] - offs[2], 4 * bw, 2 * bw, LANES)
    lru_p, gla_p, dn_p, s5_p, small = _in_proj(x_parts, row(p['norm_g']), w_cat, out_w)
    seq = lambda a: a.reshape(nb, seq_len, a.shape[1])
    flat = lambda a: a.reshape(nb * seq_len, a.shape[2])
    lru_s, gla_s, dn_s, s5_s, small_s = seq(lru_p), seq(gla_p), seq(dn_p), seq(s5_p), seq(small)

    lanes_per_blk = bw // lru_blocks
    y_lru = None
    for dct, rev in enumerate((False, True)):
        wg = bf(jnp.concatenate([p['lru_w_a'][dct], p['lru_w_x'][dct]], axis=2))
        bg = jnp.concatenate([p['lru_b_a'][dct].reshape(lru_blocks, 1, lanes_per_blk),
                              p['lru_b_x'][dct].reshape(lru_blocks, 1, lanes_per_blk)], axis=2)
        y_lru = _lru_dir(rev, lru_s, y_lru, p['lru_conv_w'], row(p['lru_conv_b']), wg, bg, row(p['lru_lambda'][dct]))

    wup = jnp.zeros((2, LANES, gla_heads * gla_dk), F32)
    for dct in range(2):
        lo = n_gate + dct * gla_rank
        wup = wup.at[dct, lo:lo + gla_rank].set(p['gla_w_up'][dct])
    gla_f, gla_b = _gla_both(gla_s, small_s, bf(wup), p['gla_b_up'].reshape(2, 1, -1), gla_heads, gla_dk, gla_dv)

    alpha_lanes = (jnp.arange(2)[:, None] * 2 * dn_heads + dn_heads + jnp.arange(dn_heads)[None, :]).reshape(-1)
    beta_lanes = alpha_lanes - dn_heads
    gate_consts = jnp.zeros((SUBLANES, LANES), F32)
    gate_consts = gate_consts.at[0, alpha_lanes].set(p['dn_a_log'].reshape(-1))
    gate_consts = gate_consts.at[1, alpha_lanes].set(p['dn_dt_bias'].reshape(-1))
    gate_consts = gate_consts.at[2, beta_lanes].set(1.0).at[3, alpha_lanes].set(1.0)
    dn_f, dn_b = _dn_branch(dn_s, small_s, p['dn_conv_w'], gate_consts, dn_heads, dn_dh)

    y_s5 = _s5_toeplitz_branch(s5_s, p['s5_lambda_re'], p['s5_lambda_im'], p['s5_log_dt'], p['s5_b_re'], p['s5_b_im'],
                      p['s5_c_re'], p['s5_c_im'], row(p['s5_d']), bf(p['s5_w_glu']), row(p['s5_b_glu']))

    gla_gate_col = (offs[5] - offs[2]) // bw
    dn_gate_col = (offs[8] - offs[7]) // bw
    rest = (flat(y_lru),
            (flat(gla_f), flat(gla_b), gla_p, gla_gate_col, row(p['gla_norm_g']), gla_heads),
            (flat(dn_f), flat(dn_b), dn_p, dn_gate_col, row(p['dn_norm_g']), dn_heads),
            flat(y_s5), row(p['norm_g']),
            bf(p['w_merge_gate']), p['b_merge_gate'].reshape(4, 1, d), bf(p['w_branch']), bf(p['w_out']),
            row(final_g), final)
    if not final:
        return [_out_proj([_whole(a) for a in x_parts], 0, *rest)]
    outs, row0 = [], 0
    for gi, rows in enumerate(group_rows):
        x_range = (x_parts[0], row0, rows) if len(x_parts) == 1 else _whole(x_parts[gi])
        outs.append(_out_proj([x_range], row0, *rest))
        row0 += rows
    return outs


_LAYER_PARAM_NAMES = (
    'norm_g', 'w_in', 'lru_conv_w', 'lru_conv_b', 'lru_w_a', 'lru_b_a', 'lru_w_x', 'lru_b_x', 'lru_lambda',
    'gla_w_up', 'gla_b_up', 'gla_norm_g', 'dn_conv_w', 'dn_a_log', 'dn_dt_bias', 'dn_norm_g',
    's5_lambda_re', 's5_lambda_im', 's5_log_dt', 's5_b_re', 's5_b_im', 's5_c_re', 's5_c_im', 's5_d',
    's5_w_glu', 's5_b_glu', 'w_branch', 'w_merge_gate', 'b_merge_gate', 'w_out')


def kernel(x_prompt, x_sample, norm_g, w_in, lru_conv_w, lru_conv_b, lru_w_a, lru_b_a, lru_w_x, lru_b_x, lru_lambda, gla_w_up, gla_b_up, gla_norm_g, dn_conv_w, dn_a_log, dn_dt_bias, dn_norm_g, s5_lambda_re, s5_lambda_im, s5_log_dt, s5_b_re, s5_b_im, s5_c_re, s5_c_im, s5_d, s5_w_glu, s5_b_glu, w_branch, w_merge_gate, b_merge_gate, w_out, final_norm_g):
    stacked = (norm_g, w_in, lru_conv_w, lru_conv_b, lru_w_a, lru_b_a, lru_w_x, lru_b_x, lru_lambda,
               gla_w_up, gla_b_up, gla_norm_g, dn_conv_w, dn_a_log, dn_dt_bias, dn_norm_g,
               s5_lambda_re, s5_lambda_im, s5_log_dt, s5_b_re, s5_b_im, s5_c_re, s5_c_im, s5_d,
               s5_w_glu, s5_b_glu, w_branch, w_merge_gate, b_merge_gate, w_out)
    depth = norm_g.shape[0]
    seq_len, d = x_prompt.shape[1:]
    assert x_sample.shape[1] == seq_len
    groups = (x_prompt, x_sample)
    parts = [a.reshape(-1, d) for a in groups]
    group_rows = [a.shape[0] for a in parts]
    for layer in range(depth):
        p = {name: arr[layer] for name, arr in zip(_LAYER_PARAM_NAMES, stacked)}
        parts = _layer(parts, group_rows, seq_len, p, final_norm_g, layer == depth - 1)
    return tuple(y.reshape(a.shape) for y, a in zip(parts, groups))
```

```python
import functools
import math

import jax
import jax.numpy as jnp
from jax import lax
from jax.experimental import pallas as pl
from jax.experimental.pallas import tpu as pltpu

F32 = jnp.float32
MXU_DTYPE = jnp.bfloat16
HI = lax.Precision.HIGHEST

NORM_EPS = 1e-6
CONV_W = 4
CONV_HALO = 8
LRU_C = 8.0
GLA_TAU = 16.0
GLA_HEADS = 4
CHUNK = 64
LANES = 128
SUBLANES = 8
SEG_PAD = 4
VMEM_LIMIT = 56 * 1024 * 1024

TM_PROJ = 256
TL_MIX = 256
SOLVE_CHUNKS = 128


def _mm(a, b):
    return jnp.dot(a.astype(MXU_DTYPE), b.astype(MXU_DTYPE), preferred_element_type=F32)


def _mm_nt(a, b):
    return lax.dot_general(a.astype(MXU_DTYPE), b.astype(MXU_DTYPE), (((1,), (1,)), ((), ())),
                           preferred_element_type=F32)


def _mm_tn(a, b):
    return lax.dot_general(a.astype(MXU_DTYPE), b.astype(MXU_DTYPE), (((0,), (0,)), ((), ())),
                           preferred_element_type=F32)


def _cum_dot(cum3, x):
    hi = x.astype(jnp.bfloat16)
    r1 = x - hi.astype(F32)
    mid = r1.astype(jnp.bfloat16)
    lo = (r1 - mid.astype(F32)).astype(jnp.bfloat16)
    return jnp.dot(cum3, jnp.concatenate([hi, mid, lo], axis=0), preferred_element_type=F32)


def _sigmoid(x):
    return 1.0 / (1.0 + jnp.exp(-x))


def _silu(x):
    return x * _sigmoid(x)


def _softplus(x):
    return jnp.maximum(x, 0.0) + jnp.log1p(jnp.exp(-jnp.abs(x)))


def _rms(x, g):
    return x * lax.rsqrt(jnp.mean(x * x, axis=-1, keepdims=True) + NORM_EPS) * g


def _params(*sem):
    return pltpu.CompilerParams(dimension_semantics=sem, vmem_limit_bytes=VMEM_LIMIT)


def _const_spec(shape):
    nd = len(shape)
    return pl.BlockSpec(shape, lambda *_: (0,) * nd)


def _seq_pos(rev):
    if rev:
        return lambda i, n: n - 1 - i
    return lambda i, n: i


def _log2(n):
    assert n > 0 and n & (n - 1) == 0, n
    return n.bit_length() - 1


def _block_mask(nrow, ncol, row_blk, col_blk):
    r = jnp.right_shift(lax.broadcasted_iota(jnp.int32, (nrow, ncol), 0), _log2(row_blk))
    c = jnp.right_shift(lax.broadcasted_iota(jnp.int32, (nrow, ncol), 1), _log2(col_blk))
    return r == c


def _tri_mask_heads(rev, strict, heads):
    t = lax.broadcasted_iota(jnp.int32, (CHUNK, heads * CHUNK), 0)
    s = jnp.bitwise_and(lax.broadcasted_iota(jnp.int32, (CHUNK, heads * CHUNK), 1), CHUNK - 1)
    if rev:
        return (s > t) if strict else (s >= t)
    return (s < t) if strict else (s <= t)


def _block_cum(rev, tl):
    t = jnp.arange(tl)[:, None]
    s = jnp.arange(tl)[None, :]
    same = (t // CHUNK) == (s // CHUNK)
    cum = (same & ((s >= t) if rev else (s <= t))).astype(jnp.bfloat16)
    return jnp.concatenate([cum, cum, cum], axis=1)


def _chunk_order(rev, nchunk):
    return range(nchunk - 1, -1, -1) if rev else range(nchunk)


def _row_parts_specs(parts, width):
    specs, starts, step0 = [], [], 0
    for _, row0, rows in parts:
        steps, blk0 = rows // TM_PROJ, row0 // TM_PROJ
        specs.append(pl.BlockSpec(
            (TM_PROJ, width), lambda i, s=step0, n=steps, b=blk0: (b + jnp.clip(i - s, 0, n - 1), 0)))
        starts.append(step0)
        step0 += steps
    return specs, starts


def _whole(arr):
    return (arr, 0, arr.shape[0])


def _row_parts_value(refs, starts):
    i = pl.program_id(0)
    x = refs[0][...]
    for ref, s in zip(refs[1:], starts[1:]):
        x = jnp.where(i >= s, ref[...], x)
    return x


def _in_proj_kernel(widths, starts, s5_index, s5_gw, *refs):
    nparts = len(starts)
    g_ref, w_ref = refs[nparts:nparts + 2]
    o_refs = refs[nparts + 2:nparts + 2 + len(widths)]
    ug_ref, xs, acc = refs[nparts + 2 + len(widths):]
    xn = _rms(_row_parts_value(refs[:nparts], starts), g_ref[...]).astype(MXU_DTYPE)
    off = 0
    for n, (o_ref, wd) in enumerate(zip(o_refs, widths)):
        y = jnp.dot(xn, w_ref[:, off:off + wd], preferred_element_type=F32)
        o_ref[...] = y
        if n == s5_index:
            _rows_to_groups(y[:, 0:wd // 2], s5_gw, ug_ref, xs, acc)
        off += wd


def _in_proj(x_parts, g, w_cat, widths, seq_len, s5_index, s5_groups, s5_gw):
    d = x_parts[0].shape[1]
    t = sum(a.shape[0] for a in x_parts)
    n = w_cat.shape[1]
    x_specs, starts = _row_parts_specs([_whole(a) for a in x_parts], d)
    per_seq = seq_len // TM_PROJ
    nrow = TM_PROJ // S5_CHUNK
    wl = S5_CHUNK * s5_gw
    s5_w = widths[s5_index] // 2
    return pl.pallas_call(
        functools.partial(_in_proj_kernel, widths, tuple(starts), s5_index, s5_gw),
        grid=(t // TM_PROJ,),
        in_specs=x_specs + [_const_spec((1, d)), _const_spec((d, n))],
        out_specs=[pl.BlockSpec((TM_PROJ, wd), lambda i: (i, 0)) for wd in widths] + [
            pl.BlockSpec((s5_groups, None, nrow, wl), lambda i: (0, i // per_seq, i % per_seq, 0))],
        out_shape=[jax.ShapeDtypeStruct((t, wd), F32) for wd in widths] + [
            jax.ShapeDtypeStruct((s5_groups, t // seq_len, seq_len // S5_CHUNK, wl), MXU_DTYPE)],
        scratch_shapes=[pltpu.VMEM((s5_w // LANES, TM_PROJ, LANES), F32), pltpu.VMEM((s5_groups, nrow, wl), F32)],
        compiler_params=_params("parallel"),
        name="in_proj",
    )(*x_parts, g, w_cat)


def _out_proj_kernel(final, gla_heads, dn_heads, s5_gw, starts, *refs):
    nparts = len(starts)
    (ylru_ref, gof_ref, gob_ref, ggate_ref, dof_ref, dob_ref, dgate_ref, s5y_ref, s5p_ref, dsk_ref, wglu_ref,
     bglu_ref, g_ref, gng_ref, dng_ref, wmg_ref, bmg_ref, wbr_ref, wout_ref, fg_ref, o_ref, s5_tmp, s5_nat
     ) = refs[nparts:]

    def finish(of_ref, ob_ref, gate_ref, ng_ref, heads):
        o = of_ref[...] + ob_ref[...]
        dv = o.shape[1] // heads
        normed = [_rms(o[:, h * dv:(h + 1) * dv], ng_ref[...]) for h in range(heads)]
        return jnp.concatenate(normed, axis=1) * _silu(gate_ref[...])

    x = _row_parts_value(refs[:nparts], starts)
    xn = _rms(x, g_ref[...]).astype(MXU_DTYPE)
    y_s5 = _s5_finish(_groups_to_rows(s5y_ref, s5_gw, s5_tmp, s5_nat), s5p_ref[...], dsk_ref[...], wglu_ref[...],
                      bglu_ref[...])
    ys = (ylru_ref[...], finish(gof_ref, gob_ref, ggate_ref, gng_ref, gla_heads),
          finish(dof_ref, dob_ref, dgate_ref, dng_ref, dn_heads), y_s5)
    merged = None
    for n, y in enumerate(ys):
        gate = _sigmoid(jnp.dot(xn, wmg_ref[n], preferred_element_type=F32) + bmg_ref[n])
        term = gate * _mm(y, wbr_ref[n])
        merged = term if merged is None else merged + term
    out = x + _mm(merged, wout_ref[...])
    if final:
        out = _rms(out, fg_ref[...])
    o_ref[...] = out


def _out_proj(x_parts, row0, seq_len, y_lru, gla, dn, s5, g, wmg, bmg, wbr, wout, fg, final):
    d = x_parts[0][0].shape[1]
    t = sum(rows for _, _, rows in x_parts)
    bw = y_lru.shape[1]
    blk0 = row0 // TM_PROJ
    row = lambda wd, col=0: pl.BlockSpec((TM_PROJ, wd), lambda i: (blk0 + i, col))
    g_of, g_ob, g_proj, g_col, g_ng, g_heads = gla
    d_of, d_ob, d_proj, d_col, d_ng, d_heads = dn
    s5_y, s5_proj, dsk, wglu, bglu, s5_gw = s5
    per_seq = seq_len // TM_PROJ
    s5_spec = pl.BlockSpec((s5_y.shape[0], None, TM_PROJ // S5_CHUNK, s5_y.shape[3]),
                           lambda i: (0, (blk0 + i) // per_seq, (blk0 + i) % per_seq, 0))
    x_specs, starts = _row_parts_specs(x_parts, d)
    return pl.pallas_call(
        functools.partial(_out_proj_kernel, final, g_heads, d_heads, s5_gw, tuple(starts)),
        grid=(t // TM_PROJ,),
        in_specs=x_specs + [row(bw), row(bw), row(bw), row(bw, g_col), row(bw), row(bw), row(bw, d_col),
                            s5_spec, row(2 * bw), _const_spec(dsk.shape), _const_spec(wglu.shape),
                            _const_spec(bglu.shape),
                            _const_spec((1, d)), _const_spec(g_ng.shape), _const_spec(d_ng.shape),
                            _const_spec(wmg.shape), _const_spec(bmg.shape), _const_spec(wbr.shape),
                            _const_spec(wout.shape), _const_spec((1, d))],
        out_specs=pl.BlockSpec((TM_PROJ, d), lambda i: (i, 0)),
        out_shape=jax.ShapeDtypeStruct((t, d), F32),
        scratch_shapes=[pltpu.VMEM((bw // LANES, TM_PROJ, LANES), F32), pltpu.VMEM((TM_PROJ, bw), F32)],
        compiler_params=_params("parallel"),
        name="out_proj",
    )(*[a for a, _, _ in x_parts], y_lru, g_of, g_ob, g_proj, d_of, d_ob, d_proj, s5_y, s5_proj, dsk, wglu, bglu,
      g, g_ng, d_ng, wmg, bmg, wbr, wout, fg)


def _halo_specs(rev, tl, width, seq_len):
    pos = _seq_pos(rev)
    per = tl // CONV_HALO
    last = seq_len // CONV_HALO - 1
    nblk = seq_len // tl
    prev = pl.BlockSpec((None, CONV_HALO, width),
                        lambda b, i: (b, jnp.maximum(pos(i, nblk) * per - 1, 0), 0))
    nxt = pl.BlockSpec((None, CONV_HALO, width),
                       lambda b, i: (b, jnp.minimum((pos(i, nblk) + 1) * per, last), 0))
    return prev, nxt


def _centred_conv(ext_ref, u, prev, nxt, at_start, at_end, w_ref):
    tl = u.shape[0]
    ext_ref[0:CONV_HALO, :] = jnp.where(at_start, 0.0, prev)
    ext_ref[CONV_HALO:CONV_HALO + tl, :] = u
    ext_ref[CONV_HALO + tl:, :] = jnp.where(at_end, 0.0, nxt)
    left = CONV_W // 2
    acc = None
    for j in range(CONV_W):
        term = ext_ref[pl.ds(CONV_HALO + j - left, tl), :] * w_ref[j:j + 1, :]
        acc = term if acc is None else acc + term
    return acc


def _lru_kernel(rev, final, tl, *refs):
    if final:
        (main_ref, prev_ref, next_ref, hf_ref, cw_ref, cb_ref, wg_ref, bg_ref, lam_ref,
         o_ref, carry_ref, ext_ref, a_scr, d_scr) = refs
    else:
        (main_ref, prev_ref, next_ref, cw_ref, cb_ref, wg_ref, bg_ref, lam_ref,
         o_ref, carry_ref, ext_ref, a_scr, d_scr) = refs
    i = pl.program_id(1)
    nblk = pl.num_programs(1)
    blk = _seq_pos(rev)(i, nblk)
    width = o_ref.shape[1]
    nslab = width // LANES
    seg = tl // SUBLANES
    pitch = seg + SEG_PAD

    @pl.when(i == 0)
    def _():
        carry_ref[...] = jnp.zeros_like(carry_ref)

    u = main_ref[:, 0:width]
    xc = _centred_conv(ext_ref, u, prev_ref[...], next_ref[...], blk == 0, blk == nblk - 1, cw_ref) + cb_ref[...]
    c_all = -LRU_C * _softplus(-lam_ref[...])
    for h in range(nslab):
        sl = slice(h * LANES, (h + 1) * LANES)
        xh = xc[:, sl]
        pre = _mm(xh, wg_ref[h]) + bg_ref[h]
        r = _sigmoid(pre[:, 0:LANES])
        ig = _sigmoid(pre[:, LANES:])
        a = jnp.exp(c_all[:, sl] * r)
        drive = jnp.sqrt(1.0 - a * a) * (ig * xh)
        for s in range(SUBLANES):
            a_scr[h, s * pitch:s * pitch + seg, :] = a[s * seg:(s + 1) * seg, :]
            d_scr[h, s * pitch:s * pitch + seg, :] = drive[s * seg:(s + 1) * seg, :]

    order = range(seg - 1, -1, -1) if rev else range(seg)
    for h in range(nslab):
        hloc = jnp.zeros((SUBLANES, LANES), F32)
        prod = jnp.ones((SUBLANES, LANES), F32)
        for k in order:
            ak = a_scr[h, pl.ds(k, SUBLANES, stride=pitch), :]
            dk = d_scr[h, pl.ds(k, SUBLANES, stride=pitch), :]
            hloc = ak * hloc + dk
            prod = prod * ak
            d_scr[h, pl.ds(k, SUBLANES, stride=pitch), :] = hloc
            a_scr[h, pl.ds(k, SUBLANES, stride=pitch), :] = prod

    last = 0 if rev else seg - 1
    seg_order = range(SUBLANES - 1, -1, -1) if rev else range(SUBLANES)
    for h in range(nslab):
        sl = slice(h * LANES, (h + 1) * LANES)
        c = carry_ref[0:1, sl]
        for s in seg_order:
            rows = slice(s * seg, (s + 1) * seg)
            hs = d_scr[h, s * pitch:s * pitch + seg, :] + a_scr[h, s * pitch:s * pitch + seg, :] * c
            if final:
                gate = main_ref[rows, width + h * LANES:width + (h + 1) * LANES]
                o_ref[rows, sl] = (hf_ref[rows, sl] + hs) * _silu(gate)
            else:
                o_ref[rows, sl] = hs
            c = a_scr[h, pl.ds(s * pitch + last, 1), :] * c + d_scr[h, pl.ds(s * pitch + last, 1), :]
        carry_ref[0:1, sl] = c


def _lru_dir(rev, proj, hf, cw, cb, wg, bg, lam):
    nb, seq_len, two_w = proj.shape
    width = two_w // 2
    tl = TL_MIX
    nblk = seq_len // tl
    pos = _seq_pos(rev)
    blk_spec = lambda wd: pl.BlockSpec((None, tl, wd), lambda b, i: (b, pos(i, nblk), 0))
    prev, nxt = _halo_specs(rev, tl, width, seq_len)
    final = hf is not None
    pitch = tl // SUBLANES + SEG_PAD
    ins = [proj, proj, proj] + ([hf] if final else []) + [cw, cb, wg, bg, lam]
    specs = [blk_spec(two_w), prev, nxt] + ([blk_spec(width)] if final else []) + [
        _const_spec(cw.shape), _const_spec(cb.shape), _const_spec(wg.shape), _const_spec(bg.shape),
        _const_spec(lam.shape)]
    return pl.pallas_call(
        functools.partial(_lru_kernel, rev, final, tl),
        grid=(nb, nblk),
        in_specs=specs,
        out_specs=blk_spec(width),
        out_shape=jax.ShapeDtypeStruct((nb, seq_len, width), F32),
        scratch_shapes=[pltpu.VMEM((SUBLANES, width), F32),
                        pltpu.VMEM((tl + 2 * CONV_HALO, width), F32),
                        pltpu.VMEM((width // LANES, SUBLANES * pitch, LANES), F32),
                        pltpu.VMEM((width // LANES, SUBLANES * pitch, LANES), F32)],
        compiler_params=_params("parallel", "arbitrary"),
        name="lru_bwd" if rev else "lru_fwd",
    )(*ins)


def _gla_kernel(tl, heads, dk, dv, mf_ref, mb_ref, sf_ref, sb_ref, wup_ref, bup_ref, cumf_ref, cumb_ref,
                of_ref, ob_ref, st_ref):
    i = pl.program_id(1)
    hk = heads * dk
    nchunk = tl // CHUNK

    @pl.when(i == 0)
    def _():
        st_ref[...] = jnp.zeros_like(st_ref)

    hv = heads * dv
    kk_blocks = _block_mask(heads * CHUNK, hk, CHUNK, dk).astype(MXU_DTYPE)
    kv_blocks = _block_mask(heads * CHUNK, hv, CHUNK, dv).astype(MXU_DTYPE)
    st_blocks = _block_mask(hv, hk, dv, dk).astype(F32)
    dirs = ((False, mf_ref, sf_ref, cumf_ref, of_ref), (True, mb_ref, sb_ref, cumb_ref, ob_ref))
    for d, (rev, m_ref, s_ref, cum_ref, o_ref) in enumerate(dirs):
        incl = _tri_mask_heads(rev, False, heads)
        end = 0 if rev else CHUNK - 1
        pre = _mm(s_ref[...], wup_ref[d]) + bup_ref[d]
        la = (jnp.minimum(pre, 0.0) - jnp.log1p(jnp.exp(-jnp.abs(pre)))) * (1.0 / GLA_TAU)
        b = _cum_dot(cum_ref[...], la)
        q_dec = (m_ref[:, 0:hk] * (dk ** -0.5) * jnp.exp(b)).astype(MXU_DTYPE)
        k = m_ref[:, hk:2 * hk]
        k_inv = (k * jnp.exp(-b)).astype(MXU_DTYPE)
        state = st_ref[d]
        for c in _chunk_order(rev, nchunk):
            rows = slice(c * CHUNK, (c + 1) * CHUNK)
            b_end = b[c * CHUNK + end:c * CHUNK + end + 1, :]
            k_end = k[rows, :] * jnp.exp(b_end - b[rows, :])
            v = m_ref[rows, 2 * hk:2 * hk + hv].astype(MXU_DTYPE)
            k_exp = jnp.concatenate([k_inv[rows, :]] * heads, axis=0) * kk_blocks
            v_exp = jnp.concatenate([v] * heads, axis=0) * kv_blocks
            scores = jnp.where(incl, _mm_nt(q_dec[rows, :], k_exp), 0.0)
            o_ref[rows, :] = _mm(scores, v_exp) + _mm_nt(q_dec[rows, :], state)
            state = state * jnp.exp(b_end) + _mm_tn(v, k_end) * st_blocks
        st_ref[d] = state


def _gla_both(proj, small, wup, bup, heads, dk, dv):
    nb, seq_len, pw = proj.shape
    tl = TL_MIX
    nblk = seq_len // tl
    fwd = lambda wd: pl.BlockSpec((None, tl, wd), lambda b, i: (b, i, 0))
    bwd = lambda wd: pl.BlockSpec((None, tl, wd), lambda b, i: (b, nblk - 1 - i, 0))
    hv = heads * dv
    sw = small.shape[2]
    cum_f, cum_b = _block_cum(False, tl), _block_cum(True, tl)
    out = jax.ShapeDtypeStruct((nb, seq_len, hv), F32)
    return pl.pallas_call(
        functools.partial(_gla_kernel, tl, heads, dk, dv),
        grid=(nb, nblk),
        in_specs=[fwd(pw), bwd(pw), fwd(sw), bwd(sw), _const_spec(wup.shape), _const_spec(bup.shape),
                  _const_spec(cum_f.shape), _const_spec(cum_b.shape)],
        out_specs=[fwd(hv), bwd(hv)],
        out_shape=[out, out],
        scratch_shapes=[pltpu.VMEM((2, heads * dv, heads * dk), F32)],
        compiler_params=_params("parallel", "arbitrary"),
        name="gla",
    )(proj, proj, small, small, wup, bup, cum_f, cum_b)


def _dn_gates(small, gc_ref):
    beta = _sigmoid(small)
    g = -jnp.exp(gc_ref[0:1, :]) * _softplus(small + gc_ref[1:2, :])
    return jnp.where(gc_ref[2:3, :] > 0.5, beta, jnp.where(gc_ref[3:4, :] > 0.5, g, 0.0))


def _dn_forms(gx, d, heads):
    assert 2 * heads == SUBLANES and 2 * CHUNK == LANES
    base = 2 * heads * d
    rows8 = gx.T[base:base + SUBLANES, :]
    rows8_rot = pltpu.roll(rows8, CHUNK, axis=1)
    rep = lambda lane: jnp.broadcast_to(gx[:, lane:lane + 1], (gx.shape[0], LANES))
    return rows8, rows8_rot, [rep(base + h) for h in range(heads)], [rep(base + heads + h) for h in range(heads)]


def _head_rows(rows8, rows8_rot, r0, c, heads):
    nchunk = rows8.shape[1] // CHUNK
    low_half = lax.broadcasted_iota(jnp.int32, (1, LANES), 1) < CHUNK
    out = []
    for j in range(heads // 2):
        halves = []
        for h in (2 * j, 2 * j + 1):
            src, slot = (rows8, c) if c % 2 == h % 2 else (rows8_rot, (c + 1) % nchunk)
            halves.append(src[r0 + h:r0 + h + 1, LANES * (slot // 2):LANES * (slot // 2 + 1)])
        out.append(jnp.where(low_half, halves[0], halves[1]))
    return jnp.concatenate(out, axis=1)


def _head_cols(rep, c, heads):
    low_half = lax.broadcasted_iota(jnp.int32, (1, LANES), 1) < CHUNK
    rows = slice(c * CHUNK, (c + 1) * CHUNK)
    return jnp.concatenate([jnp.where(low_half, rep[2 * j][rows, :], rep[2 * j + 1][rows, :])
                            for j in range(heads // 2)], axis=1)


def _dn_gamma(rev, forms, c, heads):
    rows8, rows8_rot, _, g_rep = forms
    incl = _tri_mask_heads(rev, False, heads)
    diff = _head_cols(g_rep, c, heads) - _head_rows(rows8, rows8_rot, heads, c, heads)
    return jnp.where(incl, jnp.exp(jnp.where(incl, diff, 0.0)), 0.0)


def _head_expand(x, blocks, heads):
    return jnp.concatenate([x.astype(MXU_DTYPE)] * heads, axis=0) * blocks


def _dn_prep_kernel(tl, heads, dh, main_ref, prev_ref, next_ref, small_ref, cw_ref, gcst_ref, cumf_ref, cumb_ref,
                    qkv_ref, gx_ref, af_ref, ab_ref, ext_ref):
    i = pl.program_id(1)
    nblk = pl.num_programs(1)
    hw = heads * dh
    x = main_ref[:, 0:3 * hw]
    y = _silu(_centred_conv(ext_ref, x, prev_ref[...], next_ref[...], i == 0, i == nblk - 1, cw_ref))
    for h in range(heads):
        for part, scale in ((0, dh ** -0.5), (1, 1.0)):
            sl = slice(part * hw + h * dh, part * hw + (h + 1) * dh)
            z = y[:, sl]
            qkv_ref[:, sl] = z * lax.rsqrt(jnp.sum(z * z, axis=-1, keepdims=True) + NORM_EPS) * scale
    qkv_ref[:, 2 * hw:] = y[:, 2 * hw:]
    bg = _dn_gates(small_ref[...], gcst_ref)
    lane = lax.broadcasted_iota(jnp.int32, bg.shape, 1)
    run = jnp.where(lane < 2 * heads, _cum_dot(cumf_ref[...], bg), _cum_dot(cumb_ref[...], bg))
    gx = jnp.where(gcst_ref[3:4, :] > 0.5, run, bg)
    gx_ref[...] = gx
    blocks = _block_mask(heads * CHUNK, hw, CHUNK, dh).astype(MXU_DTYPE)
    for d, (rev, a_ref) in enumerate(((False, af_ref), (True, ab_ref))):
        strict = _tri_mask_heads(rev, True, heads)
        forms = _dn_forms(gx, d, heads)
        for c in range(tl // CHUNK):
            k = qkv_ref[c * CHUNK:(c + 1) * CHUNK, hw:2 * hw]
            kk = _mm_nt(k, _head_expand(k, blocks, heads))
            a_ref[c] = jnp.where(strict, kk * _head_cols(forms[2], c, heads) * _dn_gamma(rev, forms, c, heads), 0.0)


def _dn_prep(proj, small, cw, gate_consts, heads, dh):
    nb, seq_len, pw = proj.shape
    tl = TL_MIX
    nblk = seq_len // tl
    hw = heads * dh
    blk_spec = lambda wd: pl.BlockSpec((None, tl, wd), lambda b, i: (b, i, 0))
    prev, nxt = _halo_specs(False, tl, 3 * hw, seq_len)
    a_spec = pl.BlockSpec((None, tl // CHUNK, CHUNK, heads * CHUNK), lambda b, i: (b, i, 0, 0))
    a_shape = jax.ShapeDtypeStruct((nb, seq_len // CHUNK, CHUNK, heads * CHUNK), F32)
    lanes_shape = jax.ShapeDtypeStruct((nb, seq_len, LANES), F32)
    cum_f, cum_b = _block_cum(False, tl), _block_cum(True, tl)
    return pl.pallas_call(
        functools.partial(_dn_prep_kernel, tl, heads, dh),
        grid=(nb, nblk),
        in_specs=[blk_spec(pw), prev, nxt, blk_spec(small.shape[2]), _const_spec(cw.shape),
                  _const_spec(gate_consts.shape), _const_spec(cum_f.shape), _const_spec(cum_b.shape)],
        out_specs=[blk_spec(3 * hw), blk_spec(LANES), a_spec, a_spec],
        out_shape=[jax.ShapeDtypeStruct((nb, seq_len, 3 * hw), F32), lanes_shape, a_shape, a_shape],
        scratch_shapes=[pltpu.VMEM((tl + 2 * CONV_HALO, 3 * hw), F32)],
        compiler_params=_params("parallel", "parallel"),
        name="dn_prep",
    )(proj, proj, proj, small, cw, gate_consts, cum_f, cum_b)


def _dn_solve_kernel(upper, heads, a_ref, t_ref, a_scr, t_scr):
    nl = a_ref.shape[0]
    hc = heads * CHUNK
    half = CHUNK // 2
    for t in range(CHUNK):
        a_scr[t] = a_ref[:, t * hc:(t + 1) * hc].T.reshape(heads, CHUNK, nl)
    col = lax.broadcasted_iota(jnp.int32, (half, nl), 0)

    def row(idx, carry):
        t = (CHUNK - 1 - idx) if upper else idx

        def term_lo(s, accs):
            return tuple(acc - a_scr[t, h, pl.ds(s, 1), :] * t_scr[s, h, 0:half, :] for h, acc in enumerate(accs))

        def term_hi(s, accs):
            return tuple(acc - a_scr[t, h, pl.ds(s, 1), :] * t_scr[s, h, half:, :] for h, acc in enumerate(accs))

        if upper:
            lo_range, hi_range = (t + 1, jnp.maximum(t + 1, half)), (t + 1, CHUNK)
        else:
            lo_range, hi_range = (0, t), (half, jnp.maximum(t, half))
        lo = lax.fori_loop(*lo_range, term_lo, (jnp.where(col == t, 1.0, 0.0),) * heads)
        hi = lax.fori_loop(*hi_range, term_hi, (jnp.where(col + half == t, 1.0, 0.0),) * heads)
        for h in range(heads):
            t_scr[t, h, 0:half, :] = lo[h]
            t_scr[t, h, half:, :] = hi[h]
        return carry

    lax.fori_loop(0, CHUNK, row, 0)
    for t in range(CHUNK):
        t_ref[:, t * hc:(t + 1) * hc] = t_scr[t].reshape(hc, nl).T.astype(t_ref.dtype)


def _dn_solve(a, upper, heads):
    nb, nc = a.shape[:2]
    flat = a.reshape(nb * nc, CHUNK * heads * CHUNK)
    nl = min(SOLVE_CHUNKS, nb * nc)
    spec = pl.BlockSpec((nl, flat.shape[1]), lambda i: (i, 0))
    scr = pltpu.VMEM((CHUNK, heads, CHUNK, nl), F32)
    return pl.pallas_call(
        functools.partial(_dn_solve_kernel, upper, heads),
        grid=(nb * nc // nl,),
        in_specs=[spec],
        out_specs=spec,
        out_shape=jax.ShapeDtypeStruct(flat.shape, MXU_DTYPE),
        scratch_shapes=[scr, scr],
        compiler_params=_params("parallel"),
        name="dn_solve_bwd" if upper else "dn_solve_fwd",
    )(flat).reshape(a.shape)


def _dn_main_kernel(tl, heads, dh, qf_ref, gxf_ref, tf_ref, qb_ref, gxb_ref, tb_ref, of_ref, ob_ref, s_ref):
    i = pl.program_id(1)
    hw = heads * dh
    nchunk = tl // CHUNK
    npair = heads // 2
    pw = 2 * dh

    @pl.when(i == 0)
    def _():
        s_ref[...] = jnp.zeros_like(s_ref)

    blocks = _block_mask(heads * CHUNK, hw, CHUNK, dh).astype(MXU_DTYPE)
    t_blocks = _block_mask(heads * CHUNK, heads * CHUNK, CHUNK, CHUNK).astype(MXU_DTYPE)
    pair_blocks = _block_mask(pw, pw, dh, dh).astype(F32)
    stack = lambda x: jnp.concatenate([x[:, h * dh:(h + 1) * dh] for h in range(heads)], axis=0)
    dirs = ((False, qf_ref, gxf_ref, tf_ref, of_ref), (True, qb_ref, gxb_ref, tb_ref, ob_ref))
    for d, (rev, qkv_ref, gx_ref, t_ref, o_ref) in enumerate(dirs):
        incl = _tri_mask_heads(rev, False, heads)
        end = 0 if rev else CHUNK - 1
        forms = _dn_forms(gx_ref[...], d, heads)
        b_nat = jnp.concatenate(forms[2], axis=1)
        g_nat = jnp.concatenate(forms[3], axis=1)
        eg_nat = jnp.exp(g_nat)
        states = [s_ref[d, j] for j in range(npair)]
        for c in _chunk_order(rev, nchunk):
            rows = slice(c * CHUNK, (c + 1) * CHUNK)
            q = qkv_ref[rows, 0:hw]
            k = qkv_ref[rows, hw:2 * hw]
            v = qkv_ref[rows, 2 * hw:3 * hw]
            eg = eg_nat[rows, :]
            attn = jnp.where(incl, _mm_nt(q, _head_expand(k, blocks, heads)) * _dn_gamma(rev, forms, c, heads), 0.0)
            beta = b_nat[rows, :]
            t_bd = _head_expand(t_ref[c], t_blocks, heads)
            wu = _mm(t_bd, jnp.concatenate([stack(k * beta * eg), stack(v * beta)], axis=1))
            w = jnp.concatenate([wu[h * CHUNK:(h + 1) * CHUNK, 0:dh] for h in range(heads)], axis=1)
            u = jnp.concatenate([wu[h * CHUNK:(h + 1) * CHUNK, dh:] for h in range(heads)], axis=1)
            lhs = jnp.concatenate([w, q * eg], axis=0)
            xs = [_mm(lhs[:, j * pw:(j + 1) * pw], states[j]) for j in range(npair)]
            v_new = u - jnp.concatenate([x[0:CHUNK, :] for x in xs], axis=1)
            o_ref[rows, :] = (jnp.concatenate([x[CHUNK:, :] for x in xs], axis=1)
                              + _mm(attn, _head_expand(v_new, blocks, heads)))
            g_end = g_nat[c * CHUNK + end:c * CHUNK + end + 1, :]
            k_end = k * jnp.exp(g_end - g_nat[rows, :])
            e_end = jnp.exp(g_end)
            for j in range(npair):
                ps = slice(j * pw, (j + 1) * pw)
                states[j] = states[j] * e_end[:, ps] + _mm_tn(k_end[:, ps], v_new[:, ps]) * pair_blocks
        for j in range(npair):
            s_ref[d, j] = states[j]


def _dn_main(qkv, gx, t_f, t_b, heads, dh):
    nb, seq_len, _ = qkv.shape
    tl = TL_MIX
    nblk = seq_len // tl
    hw = heads * dh
    fwd = lambda wd: pl.BlockSpec((None, tl, wd), lambda b, i: (b, i, 0))
    bwd = lambda wd: pl.BlockSpec((None, tl, wd), lambda b, i: (b, nblk - 1 - i, 0))
    t_blk = (None, tl // CHUNK, CHUNK, heads * CHUNK)
    t_fwd = pl.BlockSpec(t_blk, lambda b, i: (b, i, 0, 0))
    t_bwd = pl.BlockSpec(t_blk, lambda b, i: (b, nblk - 1 - i, 0, 0))
    out = jax.ShapeDtypeStruct((nb, seq_len, hw), F32)
    return pl.pallas_call(
        functools.partial(_dn_main_kernel, tl, heads, dh),
        grid=(nb, nblk),
        in_specs=[fwd(3 * hw), fwd(LANES), t_fwd, bwd(3 * hw), bwd(LANES), t_bwd],
        out_specs=[fwd(hw), bwd(hw)],
        out_shape=[out, out],
        scratch_shapes=[pltpu.VMEM((2, heads // 2, 2 * dh, 2 * dh), F32)],
        compiler_params=_params("parallel", "arbitrary"),
        name="dn_main",
    )(qkv, gx, t_f, qkv, gx, t_b)


def _dn_branch(proj, small, cw, gate_consts, heads, dh):
    qkv, gx, a_f, a_b = _dn_prep(proj, small, cw, gate_consts, heads, dh)
    return _dn_main(qkv, gx, _dn_solve(a_f, False, heads), _dn_solve(a_b, True, heads), heads, dh)


S5_CHUNK = 16


def _cpow(lam_re, lam_im, dt, e):
    mag = jnp.exp(lam_re * dt * e)
    ang = lam_im * dt * e
    return mag * jnp.cos(ang), mag * jnp.sin(ang)


def _s5_wt_kernel(gw, lre_ref, lim_ref, ldt_ref, bre_ref, bim_ref, ctre_ref, ctim_ref, cre_ref, cim_ref,
                  k_ref, zwre_ref, zwim_ref, cwre_ref, cwimn_ref):
    backward = pl.program_id(0) == 1
    lam_re, lam_im, dt = lre_ref[...], lim_ref[...], jnp.exp(ldt_ref[...])
    a_re, a_im = _cpow(lam_re, lam_im, dt, 1.0)
    den = lam_re * lam_re + lam_im * lam_im
    n_re = a_re - 1.0
    f_re = (n_re * lam_re + a_im * lam_im) / den
    f_im = (a_im * lam_re - n_re * lam_im) / den
    bb_re = f_re * bre_ref[...] - f_im * bim_ref[...]
    bb_im = f_re * bim_ref[...] + f_im * bre_ref[...]
    pos = jnp.right_shift(lax.broadcasted_iota(jnp.int32, lam_re.shape, 1), _log2(gw)).astype(F32)
    p_re, p_im = _cpow(lam_re, lam_im, dt, pos)
    e_re = p_re * bb_re - p_im * bb_im
    e_im = p_re * bb_im + p_im * bb_re
    k_ref[...] = (jnp.dot(cre_ref[...], e_re, precision=HI, preferred_element_type=F32)
                  - jnp.dot(cim_ref[...], e_im, precision=HI, preferred_element_type=F32))
    z_re, z_im = _cpow(lam_re, lam_im, dt, jnp.where(backward, pos, (S5_CHUNK - 1.0) - pos))
    zwre_ref[...] = z_re * bb_re - z_im * bb_im
    zwim_ref[...] = z_re * bb_im + z_im * bb_re
    c_re, c_im = _cpow(lam_re, lam_im, dt, jnp.where(backward, S5_CHUNK - pos, pos + 1.0))
    cwre_ref[...] = ctre_ref[...] * c_re - ctim_ref[...] * c_im
    cwimn_ref[...] = -(ctre_ref[...] * c_im + ctim_ref[...] * c_re)


def _s5_weights(lam_re, lam_im, log_dt, b_re, b_im, c_re, c_im):
    _, groups, states, gw = b_re.shape
    wl = S5_CHUNK * gw
    rep = lambda a: jnp.broadcast_to(a[..., None], (2, groups, states, wl))
    tile = lambda a: jnp.tile(a, (1, 1, 1, S5_CHUNK))
    blk = lambda r, c: pl.BlockSpec((None, None, r, c), lambda d, g: (d, g, 0, 0))
    ct = lambda a: tile(a.transpose(0, 1, 3, 2))
    big = jax.ShapeDtypeStruct((2, groups, states, wl), F32)
    return pl.pallas_call(
        functools.partial(_s5_wt_kernel, gw),
        grid=(2, groups),
        in_specs=[blk(states, wl)] * 7 + [blk(gw, states)] * 2,
        out_specs=[blk(gw, wl)] + [blk(states, wl)] * 4,
        out_shape=[jax.ShapeDtypeStruct((2, groups, gw, wl), F32), big, big, big, big],
        compiler_params=_params("parallel", "parallel"),
        name="s5_wt",
    )(rep(lam_re), rep(lam_im), rep(jnp.broadcast_to(log_dt[..., None], lam_re.shape)), tile(b_re), tile(b_im),
      ct(c_re), ct(c_im), c_re, c_im)


def _s5_pow_kernel(seg, lre_ref, lim_ref, ldt_ref, pw_ref):
    a_re, a_im = _cpow(lre_ref[...], lim_ref[...], jnp.exp(ldt_ref[...]), float(S5_CHUNK))
    x_re, x_im = a_re, a_im
    for k in range(seg):
        pw_ref[0, k] = x_re
        pw_ref[1, k] = x_im
        x_re, x_im = x_re * a_re - x_im * a_im, x_re * a_im + x_im * a_re


def _s5_pow(lre, lim, ldt, seg):
    return pl.pallas_call(
        functools.partial(_s5_pow_kernel, seg),
        out_shape=jax.ShapeDtypeStruct((2, seg) + lre.shape, F32),
        name="s5_pow",
    )(lre, lim, ldt)


def _gelu_tanh(x):
    return 0.5 * x * (1.0 + jnp.tanh(math.sqrt(2.0 / math.pi) * (x + 0.044715 * (x * x * x))))


def _s5_mix_kernel(nr, u_ref, tf_ref, tb_ref, zw_ref, cw_ref, pw_ref, y_ref, pad_re, pad_im, f_re, f_im, b_re, b_im):
    seg = nr // SUBLANES
    pitch = seg + SEG_PAD
    half = LANES // 2
    fwd_lanes = lax.broadcasted_iota(jnp.int32, (1, LANES), 1) < half
    w1 = jnp.concatenate([(tf_ref[...] + tb_ref[...]).astype(MXU_DTYPE), zw_ref[...]], axis=1)
    yz = jnp.dot(u_ref[...], w1, preferred_element_type=F32)
    wl = tf_ref.shape[1]

    for pad, col in ((pad_re, wl), (pad_im, wl + LANES)):
        pad[0:SUBLANES, :] = jnp.zeros((SUBLANES, LANES), F32)
        pad[SUBLANES:SUBLANES + nr, :] = yz[:, col:col + LANES]
        pad[SUBLANES + nr:, :] = jnp.zeros((SUBLANES, LANES), F32)
    for s in range(SUBLANES):
        dst = slice(s * pitch, s * pitch + seg)
        for pad, fw, bw in ((pad_re, f_re, b_re), (pad_im, f_im, b_im)):
            fw[dst, :] = pad[pl.ds(SUBLANES - 1 + s * seg, seg), :]
            bw[dst, :] = pad[pl.ds(SUBLANES + 1 + s * seg, seg), :]

    a_re = jnp.broadcast_to(pw_ref[0, 0:1, :], (SUBLANES, LANES))
    a_im = jnp.broadcast_to(pw_ref[1, 0:1, :], (SUBLANES, LANES))
    for re_scr, im_scr, order in ((f_re, f_im, range(seg)), (b_re, b_im, range(seg - 1, -1, -1))):
        sr = jnp.zeros((SUBLANES, LANES), F32)
        si = jnp.zeros((SUBLANES, LANES), F32)
        for k in order:
            sr, si = (a_re * sr - a_im * si + re_scr[pl.ds(k, SUBLANES, stride=pitch), :],
                      a_re * si + a_im * sr + im_scr[pl.ds(k, SUBLANES, stride=pitch), :])
            re_scr[pl.ds(k, SUBLANES, stride=pitch), :] = sr
            im_scr[pl.ds(k, SUBLANES, stride=pitch), :] = si

    pr, pi = pw_ref[2], pw_ref[3]
    sg_re = pw_ref[0, seg - 1:seg, :]
    sg_im = pw_ref[1, seg - 1:seg, :]
    zero = jnp.zeros((1, LANES), F32)
    cf, cb = [(zero, zero)], [(zero, zero)]
    for s in range(SUBLANES - 1):
        er, ei = f_re[pl.ds(s * pitch + seg - 1, 1), :], f_im[pl.ds(s * pitch + seg - 1, 1), :]
        cr, ci = cf[-1]
        cf.append((sg_re * cr - sg_im * ci + er, sg_re * ci + sg_im * cr + ei))
        t = SUBLANES - 1 - s
        er, ei = b_re[pl.ds(t * pitch, 1), :], b_im[pl.ds(t * pitch, 1), :]
        cr, ci = cb[-1]
        cb.append((sg_re * cr - sg_im * ci + er, sg_re * ci + sg_im * cr + ei))
    cb = cb[::-1]
    parts_re, parts_im = [], []
    for s in range(SUBLANES):
        src = slice(s * pitch, s * pitch + seg)
        cr = jnp.where(fwd_lanes, cf[s][0], cb[s][0])
        ci = jnp.where(fwd_lanes, cf[s][1], cb[s][1])
        parts_re.append(jnp.where(fwd_lanes, f_re[src, :], b_re[src, :]) + pr * cr - pi * ci)
        parts_im.append(jnp.where(fwd_lanes, f_im[src, :], b_im[src, :]) + pr * ci + pi * cr)
    states = jnp.concatenate([jnp.concatenate(parts_re, axis=0), jnp.concatenate(parts_im, axis=0)], axis=1)
    y_ref[...] = yz[:, 0:wl] + _mm(states, cw_ref[...])


def _s5_mix(u_g, toep_f, toep_b, zw, cw, pw):
    groups, nb, nr, wl = u_g.shape
    seg = nr // SUBLANES
    pitch = seg + SEG_PAD
    per_g = lambda r, c: pl.BlockSpec((None, r, c), lambda g, b: (g, 0, 0))
    rows = pl.BlockSpec((None, None, nr, wl), lambda g, b: (g, b, 0, 0))
    scan = pltpu.VMEM((SUBLANES * pitch, LANES), F32)
    pad = pltpu.VMEM((nr + 2 * SUBLANES, LANES), F32)
    return pl.pallas_call(
        functools.partial(_s5_mix_kernel, nr),
        grid=(groups, nb),
        in_specs=[rows, per_g(wl, wl), per_g(wl, wl), per_g(wl, 2 * LANES), per_g(2 * LANES, wl),
                  pl.BlockSpec((None, 4, seg, LANES), lambda g, b: (g, 0, 0, 0))],
        out_specs=rows,
        out_shape=jax.ShapeDtypeStruct((groups, nb, nr, wl), F32),
        scratch_shapes=[pad, pad, scan, scan, scan, scan],
        compiler_params=_params("parallel", "parallel"),
        name="s5_mix",
    )(u_g, toep_f, toep_b, zw, cw, pw)


def _rows_to_groups(x, gw, o_ref, xs, acc):
    per = LANES // gw
    nrow = x.shape[0] // S5_CHUNK
    nslab = x.shape[1] // LANES
    for v in range(nslab):
        xs[v] = x[:, v * LANES:(v + 1) * LANES]
    for r in range(S5_CHUNK):
        dst = (r // per) * LANES + (r % per) * gw
        for v in range(nslab):
            rows = xs[v, pl.ds(r, nrow, stride=S5_CHUNK), :]
            for s in range(per):
                rot = rows if s == 0 else pltpu.roll(rows, s * gw, axis=1)
                g = per * v + (r - s) % per
                acc[g, :, dst:dst + gw] = rot[:, (r % per) * gw:(r % per + 1) * gw]
    o_ref[...] = acc[...].astype(o_ref.dtype)


def _groups_to_rows(y_ref, gw, tmp, ynat):
    per = LANES // gw
    tl, width = ynat.shape
    nrow = tl // S5_CHUNK
    nslab = width // LANES
    for g in range(y_ref.shape[0]):
        v, slot = g // per, g % per
        for h in range(S5_CHUNK // per):
            rows = y_ref[g, :, h * LANES:(h + 1) * LANES]
            for s in range(per):
                rot = rows if s == 0 else pltpu.roll(rows, s * gw, axis=1)
                r = h * per + (slot - s) % per
                tmp[v, r * nrow:(r + 1) * nrow, slot * gw:(slot + 1) * gw] = rot[:, slot * gw:(slot + 1) * gw]
    for n in range(nrow):
        for r0 in range(0, S5_CHUNK, SUBLANES):
            for v in range(nslab):
                ynat[n * S5_CHUNK + r0:n * S5_CHUNK + r0 + SUBLANES, v * LANES:(v + 1) * LANES] = (
                    tmp[v, pl.ds(r0 * nrow + n, SUBLANES, stride=nrow), :])
    return ynat[...]


def _s5_finish(y, main, dsk, wglu, bglu):
    width = y.shape[1]
    z = _gelu_tanh(y + dsk * main[:, 0:width])
    z = z * _sigmoid(_mm(z, wglu) + bglu)
    return z * _silu(main[:, width:])


def _s5_toeplitz_branch(u_g, lam_re, lam_im, log_dt, b_re, b_im, c_re, c_im):
    _, groups, states, gw = b_re.shape
    nr = u_g.shape[2]
    seg = nr // SUBLANES
    wl = S5_CHUNK * gw
    kk, zw_re, zw_im, cw_re, cw_imn = _s5_weights(lam_re, lam_im, log_dt, b_re, b_im, c_re, c_im)

    lag = jnp.arange(S5_CHUNK)[None, :] - jnp.arange(S5_CHUNK)[:, None]
    kt = kk.reshape(2, groups, gw, S5_CHUNK, gw).transpose(0, 1, 3, 4, 2)

    def toeplitz(d, lag_d):
        t = jnp.where((lag_d >= 0)[None, :, :, None, None], kt[d][:, jnp.maximum(lag_d, 0)], 0.0)
        return t.transpose(0, 1, 3, 2, 4).reshape(groups, wl, wl)

    toep_f, toep_b = toeplitz(0, lag), toeplitz(1, -lag)
    by_rows = lambda a: a.transpose(0, 2, 1)
    zw = jnp.concatenate([by_rows(zw_re[0]), by_rows(zw_re[1]), by_rows(zw_im[0]), by_rows(zw_im[1])],
                         axis=2).astype(MXU_DTYPE)
    cw = jnp.concatenate([cw_re[0], cw_re[1], cw_imn[0], cw_imn[1]], axis=1).astype(MXU_DTYPE)

    flat = lambda a: a.reshape(2, groups * states // LANES, LANES)
    pw = _s5_pow(flat(lam_re), flat(lam_im), flat(jnp.broadcast_to(log_dt[..., None], lam_re.shape)), seg)
    pw = pw.reshape(2, seg, 2, groups, states).transpose(2, 3, 0, 1, 4)
    plain = jnp.concatenate([pw[0], pw[1]], axis=3)
    by_row = jnp.concatenate([pw[0], jnp.flip(pw[1], axis=2)], axis=3)
    pw_all = jnp.concatenate([plain, by_row], axis=1)

    return _s5_mix(u_g, toep_f, toep_b, zw, cw, pw_all)


def _layer(x_parts, group_rows, seq_len, p, final_g, final):
    d = x_parts[0].shape[1]
    nb = sum(a.shape[0] for a in x_parts) // seq_len
    bw = d // 2
    bf = lambda a: a.astype(MXU_DTYPE)
    row = lambda a: a.reshape(1, -1)

    lru_blocks = p['lru_w_a'].shape[1]
    gla_heads = GLA_HEADS
    gla_dk = p['gla_w_up'].shape[2] // gla_heads
    gla_dv = bw // gla_heads
    gla_rank = p['gla_w_up'].shape[1]
    dn_heads = p['dn_a_log'].shape[1]
    dn_dh = bw // dn_heads
    widths = (bw, bw, gla_heads * gla_dk, gla_heads * gla_dk, bw, bw, 2 * gla_rank, 3 * bw, bw, 4 * dn_heads, bw, bw)
    offs = [0]
    for wd in widths:
        offs.append(offs[-1] + wd)
    w_in = p['w_in']
    cols = lambda a, b: w_in[:, offs[a]:offs[b]]
    n_gate = 4 * dn_heads
    n_small = n_gate + 2 * gla_rank
    w_small = jnp.concatenate([cols(9, 10), cols(6, 7), jnp.zeros((d, LANES - n_small), w_in.dtype)], axis=1)
    w_cat = bf(jnp.concatenate([cols(0, 2), cols(2, 6), cols(7, 9), cols(10, 12), w_small], axis=1))
    out_w = (2 * bw, offs[6] - offs[2], 4 * bw, 2 * bw, LANES)
    s5_groups, s5_gw = p['s5_b_re'].shape[1], p['s5_b_re'].shape[3]
    lru_p, gla_p, dn_p, s5_p, small, s5_ug = _in_proj(x_parts, row(p['norm_g']), w_cat, out_w, seq_len, 3,
                                                      s5_groups, s5_gw)
    seq = lambda a: a.reshape(nb, seq_len, a.shape[1])
    flat = lambda a: a.reshape(nb * seq_len, a.shape[2])
    lru_s, gla_s, dn_s, small_s = seq(lru_p), seq(gla_p), seq(dn_p), seq(small)

    lanes_per_blk = bw // lru_blocks
    y_lru = None
    for dct, rev in enumerate((False, True)):
        wg = bf(jnp.concatenate([p['lru_w_a'][dct], p['lru_w_x'][dct]], axis=2))
        bg = jnp.concatenate([p['lru_b_a'][dct].reshape(lru_blocks, 1, lanes_per_blk),
                              p['lru_b_x'][dct].reshape(lru_blocks, 1, lanes_per_blk)], axis=2)
        y_lru = _lru_dir(rev, lru_s, y_lru, p['lru_conv_w'], row(p['lru_conv_b']), wg, bg, row(p['lru_lambda'][dct]))

    wup = jnp.zeros((2, LANES, gla_heads * gla_dk), F32)
    for dct in range(2):
        lo = n_gate + dct * gla_rank
        wup = wup.at[dct, lo:lo + gla_rank].set(p['gla_w_up'][dct])
    gla_f, gla_b = _gla_both(gla_s, small_s, bf(wup), p['gla_b_up'].reshape(2, 1, -1), gla_heads, gla_dk, gla_dv)

    alpha_lanes = (jnp.arange(2)[:, None] * 2 * dn_heads + dn_heads + jnp.arange(dn_heads)[None, :]).reshape(-1)
    beta_lanes = alpha_lanes - dn_heads
    gate_consts = jnp.zeros((SUBLANES, LANES), F32)
    gate_consts = gate_consts.at[0, alpha_lanes].set(p['dn_a_log'].reshape(-1))
    gate_consts = gate_consts.at[1, alpha_lanes].set(p['dn_dt_bias'].reshape(-1))
    gate_consts = gate_consts.at[2, beta_lanes].set(1.0).at[3, alpha_lanes].set(1.0)
    dn_f, dn_b = _dn_branch(dn_s, small_s, p['dn_conv_w'], gate_consts, dn_heads, dn_dh)

    s5_yg = _s5_toeplitz_branch(s5_ug, p['s5_lambda_re'], p['s5_lambda_im'], p['s5_log_dt'], p['s5_b_re'],
                                p['s5_b_im'], p['s5_c_re'], p['s5_c_im'])

    gla_gate_col = (offs[5] - offs[2]) // bw
    dn_gate_col = (offs[8] - offs[7]) // bw
    rest = (seq_len, flat(y_lru),
            (flat(gla_f), flat(gla_b), gla_p, gla_gate_col, row(p['gla_norm_g']), gla_heads),
            (flat(dn_f), flat(dn_b), dn_p, dn_gate_col, row(p['dn_norm_g']), dn_heads),
            (s5_yg, s5_p, row(p['s5_d']), bf(p['s5_w_glu']), row(p['s5_b_glu']), s5_gw), row(p['norm_g']),
            bf(p['w_merge_gate']), p['b_merge_gate'].reshape(4, 1, d), bf(p['w_branch']), bf(p['w_out']),
            row(final_g), final)
    if not final:
        return [_out_proj([_whole(a) for a in x_parts], 0, *rest)]
    outs, row0 = [], 0
    for gi, rows in enumerate(group_rows):
        x_range = (x_parts[0], row0, rows) if len(x_parts) == 1 else _whole(x_parts[gi])
        outs.append(_out_proj([x_range], row0, *rest))
        row0 += rows
    return outs


_LAYER_PARAM_NAMES = (
    'norm_g', 'w_in', 'lru_conv_w', 'lru_conv_b', 'lru_w_a', 'lru_b_a', 'lru_w_x', 'lru_b_x', 'lru_lambda',
    'gla_w_up', 'gla_b_up', 'gla_norm_g', 'dn_conv_w', 'dn_a_log', 'dn_dt_bias', 'dn_norm_g',
    's5_lambda_re', 's5_lambda_im', 's5_log_dt', 's5_b_re', 's5_b_im', 's5_c_re', 's5_c_im', 's5_d',
    's5_w_glu', 's5_b_glu', 'w_branch', 'w_merge_gate', 'b_merge_gate', 'w_out')


def kernel(x_prompt, x_sample, norm_g, w_in, lru_conv_w, lru_conv_b, lru_w_a, lru_b_a, lru_w_x, lru_b_x, lru_lambda, gla_w_up, gla_b_up, gla_norm_g, dn_conv_w, dn_a_log, dn_dt_bias, dn_norm_g, s5_lambda_re, s5_lambda_im, s5_log_dt, s5_b_re, s5_b_im, s5_c_re, s5_c_im, s5_d, s5_w_glu, s5_b_glu, w_branch, w_merge_gate, b_merge_gate, w_out, final_norm_g):
    stacked = (norm_g, w_in, lru_conv_w, lru_conv_b, lru_w_a, lru_b_a, lru_w_x, lru_b_x, lru_lambda,
               gla_w_up, gla_b_up, gla_norm_g, dn_conv_w, dn_a_log, dn_dt_bias, dn_norm_g,
               s5_lambda_re, s5_lambda_im, s5_log_dt, s5_b_re, s5_b_im, s5_c_re, s5_c_im, s5_d,
               s5_w_glu, s5_b_glu, w_branch, w_merge_gate, b_merge_gate, w_out)
    depth = norm_g.shape[0]
    seq_len, d = x_prompt.shape[1:]
    assert x_sample.shape[1] == seq_len
    groups = (x_prompt, x_sample)
    parts = [a.reshape(-1, d) for a in groups]
    group_rows = [a.shape[0] for a in parts]
    for layer in range(depth):
        p = {name: arr[layer] for name, arr in zip(_LAYER_PARAM_NAMES, stacked)}
        parts = _layer(parts, group_rows, seq_len, p, final_norm_g, layer == depth - 1)
    return tuple(y.reshape(a.shape) for y, a in zip(parts, groups))
```

```python
import functools
import math

import jax
import jax.numpy as jnp
from jax import lax
from jax.experimental import pallas as pl
from jax.experimental.pallas import tpu as pltpu

F32 = jnp.float32
MXU_DTYPE = jnp.bfloat16
HI = lax.Precision.HIGHEST

NORM_EPS = 1e-6
CONV_W = 4
CONV_HALO = 8
LRU_C = 8.0
GLA_TAU = 16.0
GLA_HEADS = 4
CHUNK = 64
LANES = 128
SUBLANES = 8
SEG_PAD = 4
VMEM_LIMIT = 56 * 1024 * 1024

TM_PROJ = 256
TL_MIX = 256
SOLVE_CHUNKS = 128


def _mm(a, b):
    return jnp.dot(a.astype(MXU_DTYPE), b.astype(MXU_DTYPE), preferred_element_type=F32)


def _mm_nt(a, b):
    return lax.dot_general(a.astype(MXU_DTYPE), b.astype(MXU_DTYPE), (((1,), (1,)), ((), ())),
                           preferred_element_type=F32)


def _mm_tn(a, b):
    return lax.dot_general(a.astype(MXU_DTYPE), b.astype(MXU_DTYPE), (((0,), (0,)), ((), ())),
                           preferred_element_type=F32)


def _cum_dot(cum3, x):
    hi = x.astype(jnp.bfloat16)
    r1 = x - hi.astype(F32)
    mid = r1.astype(jnp.bfloat16)
    lo = (r1 - mid.astype(F32)).astype(jnp.bfloat16)
    return jnp.dot(cum3, jnp.concatenate([hi, mid, lo], axis=0), preferred_element_type=F32)


def _sigmoid(x):
    return 1.0 / (1.0 + jnp.exp(-x))


def _silu(x):
    return x * _sigmoid(x)


def _softplus(x):
    return jnp.maximum(x, 0.0) + jnp.log1p(jnp.exp(-jnp.abs(x)))


def _rms(x, g):
    return x * lax.rsqrt(jnp.mean(x * x, axis=-1, keepdims=True) + NORM_EPS) * g


def _params(*sem):
    return pltpu.CompilerParams(dimension_semantics=sem, vmem_limit_bytes=VMEM_LIMIT)


def _const_spec(shape):
    nd = len(shape)
    return pl.BlockSpec(shape, lambda *_: (0,) * nd)


def _seq_pos(rev):
    if rev:
        return lambda i, n: n - 1 - i
    return lambda i, n: i


def _log2(n):
    assert n > 0 and n & (n - 1) == 0, n
    return n.bit_length() - 1


def _block_mask(nrow, ncol, row_blk, col_blk):
    r = jnp.right_shift(lax.broadcasted_iota(jnp.int32, (nrow, ncol), 0), _log2(row_blk))
    c = jnp.right_shift(lax.broadcasted_iota(jnp.int32, (nrow, ncol), 1), _log2(col_blk))
    return r == c


def _tri_mask_heads(rev, strict, heads):
    t = lax.broadcasted_iota(jnp.int32, (CHUNK, heads * CHUNK), 0)
    s = jnp.bitwise_and(lax.broadcasted_iota(jnp.int32, (CHUNK, heads * CHUNK), 1), CHUNK - 1)
    if rev:
        return (s > t) if strict else (s >= t)
    return (s < t) if strict else (s <= t)


def _block_cum(rev, tl):
    t = jnp.arange(tl)[:, None]
    s = jnp.arange(tl)[None, :]
    same = (t // CHUNK) == (s // CHUNK)
    cum = (same & ((s >= t) if rev else (s <= t))).astype(jnp.bfloat16)
    return jnp.concatenate([cum, cum, cum], axis=1)


def _chunk_order(rev, nchunk):
    return range(nchunk - 1, -1, -1) if rev else range(nchunk)


def _row_parts_specs(parts, width):
    specs, starts, step0 = [], [], 0
    for _, row0, rows in parts:
        steps, blk0 = rows // TM_PROJ, row0 // TM_PROJ
        specs.append(pl.BlockSpec(
            (TM_PROJ, width), lambda i, s=step0, n=steps, b=blk0: (b + jnp.clip(i - s, 0, n - 1), 0)))
        starts.append(step0)
        step0 += steps
    return specs, starts


def _whole(arr):
    return (arr, 0, arr.shape[0])


def _row_parts_value(refs, starts):
    i = pl.program_id(0)
    x = refs[0][...]
    for ref, s in zip(refs[1:], starts[1:]):
        x = jnp.where(i >= s, ref[...], x)
    return x


def _in_proj_kernel(widths, starts, s5_index, s5_gw, *refs):
    nparts = len(starts)
    g_ref, w_ref = refs[nparts:nparts + 2]
    o_refs = refs[nparts + 2:nparts + 2 + len(widths)]
    ug_ref, xs, acc = refs[nparts + 2 + len(widths):]
    xn = _rms(_row_parts_value(refs[:nparts], starts), g_ref[...]).astype(MXU_DTYPE)
    offs = [sum(widths[:n]) for n in range(len(widths))]
    for n in [s5_index] + [n for n in range(len(widths)) if n != s5_index]:
        y = jnp.dot(xn, w_ref[:, offs[n]:offs[n] + widths[n]], preferred_element_type=F32)
        o_refs[n][...] = y
        if n == s5_index:
            _rows_to_groups(y[:, 0:widths[n] // 2], s5_gw, ug_ref, xs, acc)


def _in_proj(x_parts, g, w_cat, widths, seq_len, s5_index, s5_groups, s5_gw):
    d = x_parts[0].shape[1]
    t = sum(a.shape[0] for a in x_parts)
    n = w_cat.shape[1]
    x_specs, starts = _row_parts_specs([_whole(a) for a in x_parts], d)
    per_seq = seq_len // TM_PROJ
    nrow = TM_PROJ // S5_CHUNK
    wl = S5_CHUNK * s5_gw
    s5_w = widths[s5_index] // 2
    return pl.pallas_call(
        functools.partial(_in_proj_kernel, widths, tuple(starts), s5_index, s5_gw),
        grid=(t // TM_PROJ,),
        in_specs=x_specs + [_const_spec((1, d)), _const_spec((d, n))],
        out_specs=[pl.BlockSpec((TM_PROJ, wd), lambda i: (i, 0)) for wd in widths] + [
            pl.BlockSpec((s5_groups, None, nrow, wl), lambda i: (0, i // per_seq, i % per_seq, 0))],
        out_shape=[jax.ShapeDtypeStruct((t, wd), F32) for wd in widths] + [
            jax.ShapeDtypeStruct((s5_groups, t // seq_len, seq_len // S5_CHUNK, wl), MXU_DTYPE)],
        scratch_shapes=[pltpu.VMEM((s5_w // LANES, TM_PROJ, LANES), F32), pltpu.VMEM((s5_groups, nrow, wl), F32)],
        compiler_params=_params("parallel"),
        name="in_proj",
    )(*x_parts, g, w_cat)


def _out_proj_kernel(final, gla_heads, dn_heads, s5_gw, starts, *refs):
    nparts = len(starts)
    (ylru_ref, gof_ref, gob_ref, ggate_ref, dof_ref, dob_ref, dgate_ref, s5y_ref, s5p_ref, dsk_ref, wglu_ref,
     bglu_ref, g_ref, gng_ref, dng_ref, wmg_ref, bmg_ref, wbr_ref, wout_ref, fg_ref, o_ref, s5_tmp, s5_nat
     ) = refs[nparts:]

    def finish(of_ref, ob_ref, gate_ref, ng_ref, heads):
        o = of_ref[...] + ob_ref[...]
        dv = o.shape[1] // heads
        normed = [_rms(o[:, h * dv:(h + 1) * dv], ng_ref[...]) for h in range(heads)]
        return jnp.concatenate(normed, axis=1) * _silu(gate_ref[...])

    x = _row_parts_value(refs[:nparts], starts)
    xn = _rms(x, g_ref[...]).astype(MXU_DTYPE)
    branches = (
        lambda: ylru_ref[...],
        lambda: finish(gof_ref, gob_ref, ggate_ref, gng_ref, gla_heads),
        lambda: finish(dof_ref, dob_ref, dgate_ref, dng_ref, dn_heads),
        lambda: _s5_finish(_groups_to_rows(s5y_ref, s5_gw, s5_tmp, s5_nat), s5p_ref[...], dsk_ref[...],
                           wglu_ref[...], bglu_ref[...]))
    merged = None
    for n, branch in enumerate(branches):
        y = branch()
        gate = _sigmoid(jnp.dot(xn, wmg_ref[n], preferred_element_type=F32) + bmg_ref[n])
        term = gate * _mm(y, wbr_ref[n])
        merged = term if merged is None else merged + term
    out = x + _mm(merged, wout_ref[...])
    if final:
        out = _rms(out, fg_ref[...])
    o_ref[...] = out


def _out_proj(x_parts, row0, seq_len, y_lru, gla, dn, s5, g, wmg, bmg, wbr, wout, fg, final):
    d = x_parts[0][0].shape[1]
    t = sum(rows for _, _, rows in x_parts)
    bw = y_lru.shape[1]
    blk0 = row0 // TM_PROJ
    row = lambda wd, col=0: pl.BlockSpec((TM_PROJ, wd), lambda i: (blk0 + i, col))
    g_of, g_ob, g_proj, g_col, g_ng, g_heads = gla
    d_of, d_ob, d_proj, d_col, d_ng, d_heads = dn
    s5_y, s5_proj, dsk, wglu, bglu, s5_gw = s5
    per_seq = seq_len // TM_PROJ
    s5_spec = pl.BlockSpec((s5_y.shape[0], None, TM_PROJ // S5_CHUNK, s5_y.shape[3]),
                           lambda i: (0, (blk0 + i) // per_seq, (blk0 + i) % per_seq, 0))
    x_specs, starts = _row_parts_specs(x_parts, d)
    return pl.pallas_call(
        functools.partial(_out_proj_kernel, final, g_heads, d_heads, s5_gw, tuple(starts)),
        grid=(t // TM_PROJ,),
        in_specs=x_specs + [row(bw), row(bw), row(bw), row(bw, g_col), row(bw), row(bw), row(bw, d_col),
                            s5_spec, row(2 * bw), _const_spec(dsk.shape), _const_spec(wglu.shape),
                            _const_spec(bglu.shape),
                            _const_spec((1, d)), _const_spec(g_ng.shape), _const_spec(d_ng.shape),
                            _const_spec(wmg.shape), _const_spec(bmg.shape), _const_spec(wbr.shape),
                            _const_spec(wout.shape), _const_spec((1, d))],
        out_specs=pl.BlockSpec((TM_PROJ, d), lambda i: (i, 0)),
        out_shape=jax.ShapeDtypeStruct((t, d), F32),
        scratch_shapes=[pltpu.VMEM((bw // LANES, TM_PROJ, LANES), F32), pltpu.VMEM((TM_PROJ, bw), F32)],
        compiler_params=_params("parallel"),
        name="out_proj",
    )(*[a for a, _, _ in x_parts], y_lru, g_of, g_ob, g_proj, d_of, d_ob, d_proj, s5_y, s5_proj, dsk, wglu, bglu,
      g, g_ng, d_ng, wmg, bmg, wbr, wout, fg)


def _halo_specs(rev, tl, width, seq_len):
    pos = _seq_pos(rev)
    per = tl // CONV_HALO
    last = seq_len // CONV_HALO - 1
    nblk = seq_len // tl
    prev = pl.BlockSpec((None, CONV_HALO, width),
                        lambda b, i: (b, jnp.maximum(pos(i, nblk) * per - 1, 0), 0))
    nxt = pl.BlockSpec((None, CONV_HALO, width),
                       lambda b, i: (b, jnp.minimum((pos(i, nblk) + 1) * per, last), 0))
    return prev, nxt


def _centred_conv(ext_ref, u, prev, nxt, at_start, at_end, w_ref):
    tl = u.shape[0]
    ext_ref[0:CONV_HALO, :] = jnp.where(at_start, 0.0, prev)
    ext_ref[CONV_HALO:CONV_HALO + tl, :] = u
    ext_ref[CONV_HALO + tl:, :] = jnp.where(at_end, 0.0, nxt)
    left = CONV_W // 2
    acc = None
    for j in range(CONV_W):
        term = ext_ref[pl.ds(CONV_HALO + j - left, tl), :] * w_ref[j:j + 1, :]
        acc = term if acc is None else acc + term
    return acc


def _lru_kernel(rev, final, tl, *refs):
    if final:
        (main_ref, prev_ref, next_ref, hf_ref, cw_ref, cb_ref, wg_ref, bg_ref, lam_ref,
         o_ref, carry_ref, ext_ref, a_scr, d_scr) = refs
    else:
        (main_ref, prev_ref, next_ref, cw_ref, cb_ref, wg_ref, bg_ref, lam_ref,
         o_ref, carry_ref, ext_ref, a_scr, d_scr) = refs
    i = pl.program_id(1)
    nblk = pl.num_programs(1)
    blk = _seq_pos(rev)(i, nblk)
    width = o_ref.shape[1]
    nslab = width // LANES
    seg = tl // SUBLANES
    pitch = seg + SEG_PAD

    @pl.when(i == 0)
    def _():
        carry_ref[...] = jnp.zeros_like(carry_ref)

    u = main_ref[:, 0:width]
    xc = _centred_conv(ext_ref, u, prev_ref[...], next_ref[...], blk == 0, blk == nblk - 1, cw_ref) + cb_ref[...]
    c_all = -LRU_C * _softplus(-lam_ref[...])
    for h in range(nslab):
        sl = slice(h * LANES, (h + 1) * LANES)
        xh = xc[:, sl]
        pre = _mm(xh, wg_ref[h]) + bg_ref[h]
        r = _sigmoid(pre[:, 0:LANES])
        ig = _sigmoid(pre[:, LANES:])
        a = jnp.exp(c_all[:, sl] * r)
        drive = jnp.sqrt(1.0 - a * a) * (ig * xh)
        for s in range(SUBLANES):
            a_scr[h, s * pitch:s * pitch + seg, :] = a[s * seg:(s + 1) * seg, :]
            d_scr[h, s * pitch:s * pitch + seg, :] = drive[s * seg:(s + 1) * seg, :]

    order = range(seg - 1, -1, -1) if rev else range(seg)
    for h in range(nslab):
        hloc = jnp.zeros((SUBLANES, LANES), F32)
        prod = jnp.ones((SUBLANES, LANES), F32)
        for k in order:
            ak = a_scr[h, pl.ds(k, SUBLANES, stride=pitch), :]
            dk = d_scr[h, pl.ds(k, SUBLANES, stride=pitch), :]
            hloc = ak * hloc + dk
            prod = prod * ak
            d_scr[h, pl.ds(k, SUBLANES, stride=pitch), :] = hloc
            a_scr[h, pl.ds(k, SUBLANES, stride=pitch), :] = prod

    last = 0 if rev else seg - 1
    seg_order = range(SUBLANES - 1, -1, -1) if rev else range(SUBLANES)
    for h in range(nslab):
        sl = slice(h * LANES, (h + 1) * LANES)
        c = carry_ref[0:1, sl]
        for s in seg_order:
            rows = slice(s * seg, (s + 1) * seg)
            hs = d_scr[h, s * pitch:s * pitch + seg, :] + a_scr[h, s * pitch:s * pitch + seg, :] * c
            if final:
                gate = main_ref[rows, width + h * LANES:width + (h + 1) * LANES]
                o_ref[rows, sl] = (hf_ref[rows, sl] + hs) * _silu(gate)
            else:
                o_ref[rows, sl] = hs
            c = a_scr[h, pl.ds(s * pitch + last, 1), :] * c + d_scr[h, pl.ds(s * pitch + last, 1), :]
        carry_ref[0:1, sl] = c


def _lru_dir(rev, proj, hf, cw, cb, wg, bg, lam):
    nb, seq_len, two_w = proj.shape
    width = two_w // 2
    tl = TL_MIX
    nblk = seq_len // tl
    pos = _seq_pos(rev)
    blk_spec = lambda wd: pl.BlockSpec((None, tl, wd), lambda b, i: (b, pos(i, nblk), 0))
    prev, nxt = _halo_specs(rev, tl, width, seq_len)
    final = hf is not None
    pitch = tl // SUBLANES + SEG_PAD
    ins = [proj, proj, proj] + ([hf] if final else []) + [cw, cb, wg, bg, lam]
    specs = [blk_spec(two_w), prev, nxt] + ([blk_spec(width)] if final else []) + [
        _const_spec(cw.shape), _const_spec(cb.shape), _const_spec(wg.shape), _const_spec(bg.shape),
        _const_spec(lam.shape)]
    return pl.pallas_call(
        functools.partial(_lru_kernel, rev, final, tl),
        grid=(nb, nblk),
        in_specs=specs,
        out_specs=blk_spec(width),
        out_shape=jax.ShapeDtypeStruct((nb, seq_len, width), F32),
        scratch_shapes=[pltpu.VMEM((SUBLANES, width), F32),
                        pltpu.VMEM((tl + 2 * CONV_HALO, width), F32),
                        pltpu.VMEM((width // LANES, SUBLANES * pitch, LANES), F32),
                        pltpu.VMEM((width // LANES, SUBLANES * pitch, LANES), F32)],
        compiler_params=_params("parallel", "arbitrary"),
        name="lru_bwd" if rev else "lru_fwd",
    )(*ins)


def _gla_kernel(tl, heads, dk, dv, mf_ref, mb_ref, sf_ref, sb_ref, wup_ref, bup_ref, cumf_ref, cumb_ref,
                of_ref, ob_ref, st_ref):
    i = pl.program_id(1)
    hk = heads * dk
    nchunk = tl // CHUNK

    @pl.when(i == 0)
    def _():
        st_ref[...] = jnp.zeros_like(st_ref)

    hv = heads * dv
    kk_blocks = _block_mask(heads * CHUNK, hk, CHUNK, dk).astype(MXU_DTYPE)
    kv_blocks = _block_mask(heads * CHUNK, hv, CHUNK, dv).astype(MXU_DTYPE)
    st_blocks = _block_mask(hv, hk, dv, dk).astype(F32)
    dirs = ((False, mf_ref, sf_ref, cumf_ref, of_ref), (True, mb_ref, sb_ref, cumb_ref, ob_ref))
    for d, (rev, m_ref, s_ref, cum_ref, o_ref) in enumerate(dirs):
        incl = _tri_mask_heads(rev, False, heads)
        end = 0 if rev else CHUNK - 1
        pre = _mm(s_ref[...], wup_ref[d]) + bup_ref[d]
        la = (jnp.minimum(pre, 0.0) - jnp.log1p(jnp.exp(-jnp.abs(pre)))) * (1.0 / GLA_TAU)
        b = _cum_dot(cum_ref[...], la)
        q_dec = (m_ref[:, 0:hk] * (dk ** -0.5) * jnp.exp(b)).astype(MXU_DTYPE)
        k = m_ref[:, hk:2 * hk]
        k_inv = (k * jnp.exp(-b)).astype(MXU_DTYPE)
        state = st_ref[d]
        for c in _chunk_order(rev, nchunk):
            rows = slice(c * CHUNK, (c + 1) * CHUNK)
            b_end = b[c * CHUNK + end:c * CHUNK + end + 1, :]
            k_end = k[rows, :] * jnp.exp(b_end - b[rows, :])
            v = m_ref[rows, 2 * hk:2 * hk + hv].astype(MXU_DTYPE)
            kv_t = _mm_tn(v, k_end)
            k_exp = jnp.concatenate([k_inv[rows, :]] * heads, axis=0) * kk_blocks
            v_exp = jnp.concatenate([v] * heads, axis=0) * kv_blocks
            scores = jnp.where(incl, _mm_nt(q_dec[rows, :], k_exp), 0.0)
            o_ref[rows, :] = _mm(scores, v_exp) + _mm_nt(q_dec[rows, :], state)
            state = state * jnp.exp(b_end) + kv_t * st_blocks
        st_ref[d] = state


def _gla_both(proj, small, wup, bup, heads, dk, dv):
    nb, seq_len, pw = proj.shape
    tl = TL_MIX
    nblk = seq_len // tl
    fwd = lambda wd: pl.BlockSpec((None, tl, wd), lambda b, i: (b, i, 0))
    bwd = lambda wd: pl.BlockSpec((None, tl, wd), lambda b, i: (b, nblk - 1 - i, 0))
    hv = heads * dv
    sw = small.shape[2]
    cum_f, cum_b = _block_cum(False, tl), _block_cum(True, tl)
    out = jax.ShapeDtypeStruct((nb, seq_len, hv), F32)
    return pl.pallas_call(
        functools.partial(_gla_kernel, tl, heads, dk, dv),
        grid=(nb, nblk),
        in_specs=[fwd(pw), bwd(pw), fwd(sw), bwd(sw), _const_spec(wup.shape), _const_spec(bup.shape),
                  _const_spec(cum_f.shape), _const_spec(cum_b.shape)],
        out_specs=[fwd(hv), bwd(hv)],
        out_shape=[out, out],
        scratch_shapes=[pltpu.VMEM((2, heads * dv, heads * dk), F32)],
        compiler_params=_params("parallel", "arbitrary"),
        name="gla",
    )(proj, proj, small, small, wup, bup, cum_f, cum_b)


def _dn_gates(small, gc_ref):
    beta = _sigmoid(small)
    g = -jnp.exp(gc_ref[0:1, :]) * _softplus(small + gc_ref[1:2, :])
    return jnp.where(gc_ref[2:3, :] > 0.5, beta, jnp.where(gc_ref[3:4, :] > 0.5, g, 0.0))


def _dn_forms(gx, d, heads):
    assert 2 * heads == SUBLANES and 2 * CHUNK == LANES
    base = 2 * heads * d
    rows8 = gx.T[base:base + SUBLANES, :]
    rows8_rot = pltpu.roll(rows8, CHUNK, axis=1)
    rep = lambda lane: jnp.broadcast_to(gx[:, lane:lane + 1], (gx.shape[0], LANES))
    return rows8, rows8_rot, [rep(base + h) for h in range(heads)], [rep(base + heads + h) for h in range(heads)]


def _head_rows(rows8, rows8_rot, r0, c, heads):
    nchunk = rows8.shape[1] // CHUNK
    low_half = lax.broadcasted_iota(jnp.int32, (1, LANES), 1) < CHUNK
    out = []
    for j in range(heads // 2):
        halves = []
        for h in (2 * j, 2 * j + 1):
            src, slot = (rows8, c) if c % 2 == h % 2 else (rows8_rot, (c + 1) % nchunk)
            halves.append(src[r0 + h:r0 + h + 1, LANES * (slot // 2):LANES * (slot // 2 + 1)])
        out.append(jnp.where(low_half, halves[0], halves[1]))
    return jnp.concatenate(out, axis=1)


def _head_cols(rep, c, heads):
    low_half = lax.broadcasted_iota(jnp.int32, (1, LANES), 1) < CHUNK
    rows = slice(c * CHUNK, (c + 1) * CHUNK)
    return jnp.concatenate([jnp.where(low_half, rep[2 * j][rows, :], rep[2 * j + 1][rows, :])
                            for j in range(heads // 2)], axis=1)


def _dn_gamma(rev, forms, c, heads):
    rows8, rows8_rot, _, g_rep = forms
    incl = _tri_mask_heads(rev, False, heads)
    diff = _head_cols(g_rep, c, heads) - _head_rows(rows8, rows8_rot, heads, c, heads)
    return jnp.where(incl, jnp.exp(jnp.where(incl, diff, 0.0)), 0.0)


def _head_expand(x, blocks, heads):
    return jnp.concatenate([x.astype(MXU_DTYPE)] * heads, axis=0) * blocks


def _dn_prep_kernel(tl, heads, dh, main_ref, prev_ref, next_ref, small_ref, cw_ref, gcst_ref, cumf_ref, cumb_ref,
                    qkv_ref, gx_ref, af_ref, ab_ref, ext_ref):
    i = pl.program_id(1)
    nblk = pl.num_programs(1)
    hw = heads * dh
    x = main_ref[:, 0:3 * hw]
    y = _silu(_centred_conv(ext_ref, x, prev_ref[...], next_ref[...], i == 0, i == nblk - 1, cw_ref))
    for h in range(heads):
        for part, scale in ((0, dh ** -0.5), (1, 1.0)):
            sl = slice(part * hw + h * dh, part * hw + (h + 1) * dh)
            z = y[:, sl]
            qkv_ref[:, sl] = z * lax.rsqrt(jnp.sum(z * z, axis=-1, keepdims=True) + NORM_EPS) * scale
    qkv_ref[:, 2 * hw:] = y[:, 2 * hw:]
    bg = _dn_gates(small_ref[...], gcst_ref)
    lane = lax.broadcasted_iota(jnp.int32, bg.shape, 1)
    run = jnp.where(lane < 2 * heads, _cum_dot(cumf_ref[...], bg), _cum_dot(cumb_ref[...], bg))
    gx = jnp.where(gcst_ref[3:4, :] > 0.5, run, bg)
    gx_ref[...] = gx
    blocks = _block_mask(heads * CHUNK, hw, CHUNK, dh).astype(MXU_DTYPE)
    for d, (rev, a_ref) in enumerate(((False, af_ref), (True, ab_ref))):
        strict = _tri_mask_heads(rev, True, heads)
        forms = _dn_forms(gx, d, heads)
        for c in range(tl // CHUNK):
            k = qkv_ref[c * CHUNK:(c + 1) * CHUNK, hw:2 * hw]
            kk = _mm_nt(k, _head_expand(k, blocks, heads))
            a_ref[c] = jnp.where(strict, kk * _head_cols(forms[2], c, heads) * _dn_gamma(rev, forms, c, heads), 0.0)


def _dn_prep(proj, small, cw, gate_consts, heads, dh):
    nb, seq_len, pw = proj.shape
    tl = TL_MIX
    nblk = seq_len // tl
    hw = heads * dh
    blk_spec = lambda wd: pl.BlockSpec((None, tl, wd), lambda b, i: (b, i, 0))
    prev, nxt = _halo_specs(False, tl, 3 * hw, seq_len)
    a_spec = pl.BlockSpec((None, tl // CHUNK, CHUNK, heads * CHUNK), lambda b, i: (b, i, 0, 0))
    a_shape = jax.ShapeDtypeStruct((nb, seq_len // CHUNK, CHUNK, heads * CHUNK), F32)
    lanes_shape = jax.ShapeDtypeStruct((nb, seq_len, LANES), F32)
    cum_f, cum_b = _block_cum(False, tl), _block_cum(True, tl)
    return pl.pallas_call(
        functools.partial(_dn_prep_kernel, tl, heads, dh),
        grid=(nb, nblk),
        in_specs=[blk_spec(pw), prev, nxt, blk_spec(small.shape[2]), _const_spec(cw.shape),
                  _const_spec(gate_consts.shape), _const_spec(cum_f.shape), _const_spec(cum_b.shape)],
        out_specs=[blk_spec(3 * hw), blk_spec(LANES), a_spec, a_spec],
        out_shape=[jax.ShapeDtypeStruct((nb, seq_len, 3 * hw), F32), lanes_shape, a_shape, a_shape],
        scratch_shapes=[pltpu.VMEM((tl + 2 * CONV_HALO, 3 * hw), F32)],
        compiler_params=_params("parallel", "parallel"),
        name="dn_prep",
    )(proj, proj, proj, small, cw, gate_consts, cum_f, cum_b)


def _dn_solve_kernel(upper, heads, a_ref, t_ref, a_scr, t_scr):
    nl = a_ref.shape[0]
    hc = heads * CHUNK
    half = CHUNK // 2
    for t in range(CHUNK):
        a_scr[t] = a_ref[:, t * hc:(t + 1) * hc].T.reshape(heads, CHUNK, nl)
    col = lax.broadcasted_iota(jnp.int32, (half, nl), 0)

    def row(idx, carry):
        t = (CHUNK - 1 - idx) if upper else idx

        def term_lo(s, accs):
            return tuple(acc - a_scr[t, h, pl.ds(s, 1), :] * t_scr[s, h, 0:half, :] for h, acc in enumerate(accs))

        def term_hi(s, accs):
            return tuple(acc - a_scr[t, h, pl.ds(s, 1), :] * t_scr[s, h, half:, :] for h, acc in enumerate(accs))

        if upper:
            lo_range, hi_range = (t + 1, jnp.maximum(t + 1, half)), (t + 1, CHUNK)
        else:
            lo_range, hi_range = (0, t), (half, jnp.maximum(t, half))
        lo = lax.fori_loop(*lo_range, term_lo, (jnp.where(col == t, 1.0, 0.0),) * heads)
        hi = lax.fori_loop(*hi_range, term_hi, (jnp.where(col + half == t, 1.0, 0.0),) * heads)
        for h in range(heads):
            t_scr[t, h, 0:half, :] = lo[h]
            t_scr[t, h, half:, :] = hi[h]
        return carry

    lax.fori_loop(0, CHUNK, row, 0)
    for t in range(CHUNK):
        t_ref[:, t * hc:(t + 1) * hc] = t_scr[t].reshape(hc, nl).T.astype(t_ref.dtype)


def _dn_solve(a, upper, heads):
    nb, nc = a.shape[:2]
    flat = a.reshape(nb * nc, CHUNK * heads * CHUNK)
    nl = min(SOLVE_CHUNKS, nb * nc)
    spec = pl.BlockSpec((nl, flat.shape[1]), lambda i: (i, 0))
    scr = pltpu.VMEM((CHUNK, heads, CHUNK, nl), F32)
    return pl.pallas_call(
        functools.partial(_dn_solve_kernel, upper, heads),
        grid=(nb * nc // nl,),
        in_specs=[spec],
        out_specs=spec,
        out_shape=jax.ShapeDtypeStruct(flat.shape, MXU_DTYPE),
        scratch_shapes=[scr, scr],
        compiler_params=_params("parallel"),
        name="dn_solve_bwd" if upper else "dn_solve_fwd",
    )(flat).reshape(a.shape)


def _dn_main_kernel(tl, heads, dh, qf_ref, gxf_ref, tf_ref, qb_ref, gxb_ref, tb_ref, of_ref, ob_ref, s_ref):
    i = pl.program_id(1)
    hw = heads * dh
    nchunk = tl // CHUNK
    npair = heads // 2
    pw = 2 * dh

    @pl.when(i == 0)
    def _():
        s_ref[...] = jnp.zeros_like(s_ref)

    blocks = _block_mask(heads * CHUNK, hw, CHUNK, dh).astype(MXU_DTYPE)
    t_blocks = _block_mask(heads * CHUNK, heads * CHUNK, CHUNK, CHUNK).astype(MXU_DTYPE)
    pair_blocks = _block_mask(pw, pw, dh, dh).astype(F32)
    stack = lambda x: jnp.concatenate([x[:, h * dh:(h + 1) * dh] for h in range(heads)], axis=0)
    dirs = ((False, qf_ref, gxf_ref, tf_ref, of_ref), (True, qb_ref, gxb_ref, tb_ref, ob_ref))
    for d, (rev, qkv_ref, gx_ref, t_ref, o_ref) in enumerate(dirs):
        incl = _tri_mask_heads(rev, False, heads)
        end = 0 if rev else CHUNK - 1
        forms = _dn_forms(gx_ref[...], d, heads)
        b_nat = jnp.concatenate(forms[2], axis=1)
        g_nat = jnp.concatenate(forms[3], axis=1)
        eg_nat = jnp.exp(g_nat)
        states = [s_ref[d, j] for j in range(npair)]
        for c in _chunk_order(rev, nchunk):
            rows = slice(c * CHUNK, (c + 1) * CHUNK)
            q = qkv_ref[rows, 0:hw]
            k = qkv_ref[rows, hw:2 * hw]
            v = qkv_ref[rows, 2 * hw:3 * hw]
            eg = eg_nat[rows, :]
            beta = b_nat[rows, :]
            g_end = g_nat[c * CHUNK + end:c * CHUNK + end + 1, :]
            k_end = k * jnp.exp(g_end - g_nat[rows, :])
            e_end = jnp.exp(g_end)
            t_bd = _head_expand(t_ref[c], t_blocks, heads)
            wu = _mm(t_bd, jnp.concatenate([stack(k * beta * eg), stack(v * beta)], axis=1))
            attn = jnp.where(incl, _mm_nt(q, _head_expand(k, blocks, heads)) * _dn_gamma(rev, forms, c, heads), 0.0)
            w = jnp.concatenate([wu[h * CHUNK:(h + 1) * CHUNK, 0:dh] for h in range(heads)], axis=1)
            u = jnp.concatenate([wu[h * CHUNK:(h + 1) * CHUNK, dh:] for h in range(heads)], axis=1)
            lhs = jnp.concatenate([w, q * eg], axis=0)
            xs = [_mm(lhs[:, j * pw:(j + 1) * pw], states[j]) for j in range(npair)]
            v_new = u - jnp.concatenate([x[0:CHUNK, :] for x in xs], axis=1)
            o_ref[rows, :] = (jnp.concatenate([x[CHUNK:, :] for x in xs], axis=1)
                              + _mm(attn, _head_expand(v_new, blocks, heads)))
            for j in range(npair):
                ps = slice(j * pw, (j + 1) * pw)
                states[j] = states[j] * e_end[:, ps] + _mm_tn(k_end[:, ps], v_new[:, ps]) * pair_blocks
        for j in range(npair):
            s_ref[d, j] = states[j]


def _dn_main(qkv, gx, t_f, t_b, heads, dh):
    nb, seq_len, _ = qkv.shape
    tl = TL_MIX
    nblk = seq_len // tl
    hw = heads * dh
    fwd = lambda wd: pl.BlockSpec((None, tl, wd), lambda b, i: (b, i, 0))
    bwd = lambda wd: pl.BlockSpec((None, tl, wd), lambda b, i: (b, nblk - 1 - i, 0))
    t_blk = (None, tl // CHUNK, CHUNK, heads * CHUNK)
    t_fwd = pl.BlockSpec(t_blk, lambda b, i: (b, i, 0, 0))
    t_bwd = pl.BlockSpec(t_blk, lambda b, i: (b, nblk - 1 - i, 0, 0))
    out = jax.ShapeDtypeStruct((nb, seq_len, hw), F32)
    return pl.pallas_call(
        functools.partial(_dn_main_kernel, tl, heads, dh),
        grid=(nb, nblk),
        in_specs=[fwd(3 * hw), fwd(LANES), t_fwd, bwd(3 * hw), bwd(LANES), t_bwd],
        out_specs=[fwd(hw), bwd(hw)],
        out_shape=[out, out],
        scratch_shapes=[pltpu.VMEM((2, heads // 2, 2 * dh, 2 * dh), F32)],
        compiler_params=_params("parallel", "arbitrary"),
        name="dn_main",
    )(qkv, gx, t_f, qkv, gx, t_b)


def _dn_branch(proj, small, cw, gate_consts, heads, dh):
    qkv, gx, a_f, a_b = _dn_prep(proj, small, cw, gate_consts, heads, dh)
    return _dn_main(qkv, gx, _dn_solve(a_f, False, heads), _dn_solve(a_b, True, heads), heads, dh)


S5_CHUNK = 16


def _cpow(lam_re, lam_im, dt, e):
    mag = jnp.exp(lam_re * dt * e)
    ang = lam_im * dt * e
    return mag * jnp.cos(ang), mag * jnp.sin(ang)


def _s5_wt_kernel(gw, lre_ref, lim_ref, ldt_ref, bre_ref, bim_ref, ctre_ref, ctim_ref, cre_ref, cim_ref,
                  k_ref, zwre_ref, zwim_ref, cwre_ref, cwimn_ref):
    backward = pl.program_id(0) == 1
    lam_re, lam_im, dt = lre_ref[...], lim_ref[...], jnp.exp(ldt_ref[...])
    a_re, a_im = _cpow(lam_re, lam_im, dt, 1.0)
    den = lam_re * lam_re + lam_im * lam_im
    n_re = a_re - 1.0
    f_re = (n_re * lam_re + a_im * lam_im) / den
    f_im = (a_im * lam_re - n_re * lam_im) / den
    bb_re = f_re * bre_ref[...] - f_im * bim_ref[...]
    bb_im = f_re * bim_ref[...] + f_im * bre_ref[...]
    pos = jnp.right_shift(lax.broadcasted_iota(jnp.int32, lam_re.shape, 1), _log2(gw)).astype(F32)
    p_re, p_im = _cpow(lam_re, lam_im, dt, pos)
    e_re = p_re * bb_re - p_im * bb_im
    e_im = p_re * bb_im + p_im * bb_re
    k_ref[...] = (jnp.dot(cre_ref[...], e_re, precision=HI, preferred_element_type=F32)
                  - jnp.dot(cim_ref[...], e_im, precision=HI, preferred_element_type=F32))
    z_re, z_im = _cpow(lam_re, lam_im, dt, jnp.where(backward, pos, (S5_CHUNK - 1.0) - pos))
    zwre_ref[...] = z_re * bb_re - z_im * bb_im
    zwim_ref[...] = z_re * bb_im + z_im * bb_re
    c_re, c_im = _cpow(lam_re, lam_im, dt, jnp.where(backward, S5_CHUNK - pos, pos + 1.0))
    cwre_ref[...] = ctre_ref[...] * c_re - ctim_ref[...] * c_im
    cwimn_ref[...] = -(ctre_ref[...] * c_im + ctim_ref[...] * c_re)


def _s5_weights(lam_re, lam_im, log_dt, b_re, b_im, c_re, c_im):
    _, groups, states, gw = b_re.shape
    wl = S5_CHUNK * gw
    rep = lambda a: jnp.broadcast_to(a[..., None], (2, groups, states, wl))
    tile = lambda a: jnp.tile(a, (1, 1, 1, S5_CHUNK))
    blk = lambda r, c: pl.BlockSpec((None, None, r, c), lambda d, g: (d, g, 0, 0))
    ct = lambda a: tile(a.transpose(0, 1, 3, 2))
    big = jax.ShapeDtypeStruct((2, groups, states, wl), F32)
    return pl.pallas_call(
        functools.partial(_s5_wt_kernel, gw),
        grid=(2, groups),
        in_specs=[blk(states, wl)] * 7 + [blk(gw, states)] * 2,
        out_specs=[blk(gw, wl)] + [blk(states, wl)] * 4,
        out_shape=[jax.ShapeDtypeStruct((2, groups, gw, wl), F32), big, big, big, big],
        compiler_params=_params("parallel", "parallel"),
        name="s5_wt",
    )(rep(lam_re), rep(lam_im), rep(jnp.broadcast_to(log_dt[..., None], lam_re.shape)), tile(b_re), tile(b_im),
      ct(c_re), ct(c_im), c_re, c_im)


def _s5_pow_kernel(seg, lre_ref, lim_ref, ldt_ref, pw_ref):
    a_re, a_im = _cpow(lre_ref[...], lim_ref[...], jnp.exp(ldt_ref[...]), float(S5_CHUNK))
    x_re, x_im = a_re, a_im
    for k in range(seg):
        pw_ref[0, k] = x_re
        pw_ref[1, k] = x_im
        x_re, x_im = x_re * a_re - x_im * a_im, x_re * a_im + x_im * a_re


def _s5_pow(lre, lim, ldt, seg):
    return pl.pallas_call(
        functools.partial(_s5_pow_kernel, seg),
        out_shape=jax.ShapeDtypeStruct((2, seg) + lre.shape, F32),
        name="s5_pow",
    )(lre, lim, ldt)


def _gelu_tanh(x):
    return 0.5 * x * (1.0 + jnp.tanh(math.sqrt(2.0 / math.pi) * (x + 0.044715 * (x * x * x))))


def _s5_mix_kernel(nr, u_ref, tf_ref, tb_ref, zw_ref, cw_ref, pw_ref, y_ref, pad_re, pad_im, f_re, f_im, b_re, b_im):
    seg = nr // SUBLANES
    pitch = seg + SEG_PAD
    half = LANES // 2
    fwd_lanes = lax.broadcasted_iota(jnp.int32, (1, LANES), 1) < half
    w1 = jnp.concatenate([(tf_ref[...] + tb_ref[...]).astype(MXU_DTYPE), zw_ref[...]], axis=1)
    yz = jnp.dot(u_ref[...], w1, preferred_element_type=F32)
    wl = tf_ref.shape[1]

    for pad, col in ((pad_re, wl), (pad_im, wl + LANES)):
        pad[0:SUBLANES, :] = jnp.zeros((SUBLANES, LANES), F32)
        pad[SUBLANES:SUBLANES + nr, :] = yz[:, col:col + LANES]
        pad[SUBLANES + nr:, :] = jnp.zeros((SUBLANES, LANES), F32)
    for s in range(SUBLANES):
        dst = slice(s * pitch, s * pitch + seg)
        for pad, fw, bw in ((pad_re, f_re, b_re), (pad_im, f_im, b_im)):
            fw[dst, :] = pad[pl.ds(SUBLANES - 1 + s * seg, seg), :]
            bw[dst, :] = pad[pl.ds(SUBLANES + 1 + s * seg, seg), :]

    a_re = jnp.broadcast_to(pw_ref[0, 0:1, :], (SUBLANES, LANES))
    a_im = jnp.broadcast_to(pw_ref[1, 0:1, :], (SUBLANES, LANES))
    for re_scr, im_scr, order in ((f_re, f_im, range(seg)), (b_re, b_im, range(seg - 1, -1, -1))):
        sr = jnp.zeros((SUBLANES, LANES), F32)
        si = jnp.zeros((SUBLANES, LANES), F32)
        for k in order:
            sr, si = (a_re * sr - a_im * si + re_scr[pl.ds(k, SUBLANES, stride=pitch), :],
                      a_re * si + a_im * sr + im_scr[pl.ds(k, SUBLANES, stride=pitch), :])
            re_scr[pl.ds(k, SUBLANES, stride=pitch), :] = sr
            im_scr[pl.ds(k, SUBLANES, stride=pitch), :] = si

    pr, pi = pw_ref[2], pw_ref[3]
    sg_re = pw_ref[0, seg - 1:seg, :]
    sg_im = pw_ref[1, seg - 1:seg, :]
    zero = jnp.zeros((1, LANES), F32)
    cf, cb = [(zero, zero)], [(zero, zero)]
    for s in range(SUBLANES - 1):
        er, ei = f_re[pl.ds(s * pitch + seg - 1, 1), :], f_im[pl.ds(s * pitch + seg - 1, 1), :]
        cr, ci = cf[-1]
        cf.append((sg_re * cr - sg_im * ci + er, sg_re * ci + sg_im * cr + ei))
        t = SUBLANES - 1 - s
        er, ei = b_re[pl.ds(t * pitch, 1), :], b_im[pl.ds(t * pitch, 1), :]
        cr, ci = cb[-1]
        cb.append((sg_re * cr - sg_im * ci + er, sg_re * ci + sg_im * cr + ei))
    cb = cb[::-1]
    parts_re, parts_im = [], []
    for s in range(SUBLANES):
        src = slice(s * pitch, s * pitch + seg)
        cr = jnp.where(fwd_lanes, cf[s][0], cb[s][0])
        ci = jnp.where(fwd_lanes, cf[s][1], cb[s][1])
        parts_re.append(jnp.where(fwd_lanes, f_re[src, :], b_re[src, :]) + pr * cr - pi * ci)
        parts_im.append(jnp.where(fwd_lanes, f_im[src, :], b_im[src, :]) + pr * ci + pi * cr)
    states = jnp.concatenate([jnp.concatenate(parts_re, axis=0), jnp.concatenate(parts_im, axis=0)], axis=1)
    y_ref[...] = yz[:, 0:wl] + _mm(states, cw_ref[...])


def _s5_mix(u_g, toep_f, toep_b, zw, cw, pw):
    groups, nb, nr, wl = u_g.shape
    seg = nr // SUBLANES
    pitch = seg + SEG_PAD
    per_g = lambda r, c: pl.BlockSpec((None, r, c), lambda g, b: (g, 0, 0))
    rows = pl.BlockSpec((None, None, nr, wl), lambda g, b: (g, b, 0, 0))
    scan = pltpu.VMEM((SUBLANES * pitch, LANES), F32)
    pad = pltpu.VMEM((nr + 2 * SUBLANES, LANES), F32)
    return pl.pallas_call(
        functools.partial(_s5_mix_kernel, nr),
        grid=(groups, nb),
        in_specs=[rows, per_g(wl, wl), per_g(wl, wl), per_g(wl, 2 * LANES), per_g(2 * LANES, wl),
                  pl.BlockSpec((None, 4, seg, LANES), lambda g, b: (g, 0, 0, 0))],
        out_specs=rows,
        out_shape=jax.ShapeDtypeStruct((groups, nb, nr, wl), F32),
        scratch_shapes=[pad, pad, scan, scan, scan, scan],
        compiler_params=_params("parallel", "parallel"),
        name="s5_mix",
    )(u_g, toep_f, toep_b, zw, cw, pw)


def _rows_to_groups(x, gw, o_ref, xs, acc):
    per = LANES // gw
    nrow = x.shape[0] // S5_CHUNK
    nslab = x.shape[1] // LANES
    for v in range(nslab):
        xs[v] = x[:, v * LANES:(v + 1) * LANES]
    for r in range(S5_CHUNK):
        dst = (r // per) * LANES + (r % per) * gw
        for v in range(nslab):
            rows = xs[v, pl.ds(r, nrow, stride=S5_CHUNK), :]
            for s in range(per):
                rot = rows if s == 0 else pltpu.roll(rows, s * gw, axis=1)
                g = per * v + (r - s) % per
                acc[g, :, dst:dst + gw] = rot[:, (r % per) * gw:(r % per + 1) * gw]
    o_ref[...] = acc[...].astype(o_ref.dtype)


def _groups_to_rows(y_ref, gw, tmp, ynat):
    per = LANES // gw
    tl, width = ynat.shape
    nrow = tl // S5_CHUNK
    nslab = width // LANES
    for g in range(y_ref.shape[0]):
        v, slot = g // per, g % per
        for h in range(S5_CHUNK // per):
            rows = y_ref[g, :, h * LANES:(h + 1) * LANES]
            for s in range(per):
                rot = rows if s == 0 else pltpu.roll(rows, s * gw, axis=1)
                r = h * per + (slot - s) % per
                tmp[v, r * nrow:(r + 1) * nrow, slot * gw:(slot + 1) * gw] = rot[:, slot * gw:(slot + 1) * gw]
    for n in range(nrow):
        for r0 in range(0, S5_CHUNK, SUBLANES):
            for v in range(nslab):
                ynat[n * S5_CHUNK + r0:n * S5_CHUNK + r0 + SUBLANES, v * LANES:(v + 1) * LANES] = (
                    tmp[v, pl.ds(r0 * nrow + n, SUBLANES, stride=nrow), :])
    return ynat[...]


def _s5_finish(y, main, dsk, wglu, bglu):
    width = y.shape[1]
    z = _gelu_tanh(y + dsk * main[:, 0:width])
    z = z * _sigmoid(_mm(z, wglu) + bglu)
    return z * _silu(main[:, width:])


def _s5_toeplitz_branch(u_g, lam_re, lam_im, log_dt, b_re, b_im, c_re, c_im):
    _, groups, states, gw = b_re.shape
    nr = u_g.shape[2]
    seg = nr // SUBLANES
    wl = S5_CHUNK * gw
    kk, zw_re, zw_im, cw_re, cw_imn = _s5_weights(lam_re, lam_im, log_dt, b_re, b_im, c_re, c_im)

    lag = jnp.arange(S5_CHUNK)[None, :] - jnp.arange(S5_CHUNK)[:, None]
    kt = kk.reshape(2, groups, gw, S5_CHUNK, gw).transpose(0, 1, 3, 4, 2)

    def toeplitz(d, lag_d):
        t = jnp.where((lag_d >= 0)[None, :, :, None, None], kt[d][:, jnp.maximum(lag_d, 0)], 0.0)
        return t.transpose(0, 1, 3, 2, 4).reshape(groups, wl, wl)

    toep_f, toep_b = toeplitz(0, lag), toeplitz(1, -lag)
    by_rows = lambda a: a.transpose(0, 2, 1)
    zw = jnp.concatenate([by_rows(zw_re[0]), by_rows(zw_re[1]), by_rows(zw_im[0]), by_rows(zw_im[1])],
                         axis=2).astype(MXU_DTYPE)
    cw = jnp.concatenate([cw_re[0], cw_re[1], cw_imn[0], cw_imn[1]], axis=1).astype(MXU_DTYPE)

    flat = lambda a: a.reshape(2, groups * states // LANES, LANES)
    pw = _s5_pow(flat(lam_re), flat(lam_im), flat(jnp.broadcast_to(log_dt[..., None], lam_re.shape)), seg)
    pw = pw.reshape(2, seg, 2, groups, states).transpose(2, 3, 0, 1, 4)
    plain = jnp.concatenate([pw[0], pw[1]], axis=3)
    by_row = jnp.concatenate([pw[0], jnp.flip(pw[1], axis=2)], axis=3)
    pw_all = jnp.concatenate([plain, by_row], axis=1)

    return _s5_mix(u_g, toep_f, toep_b, zw, cw, pw_all)


def _layer(x_parts, group_rows, seq_len, p, final_g, final):
    d = x_parts[0].shape[1]
    nb = sum(a.shape[0] for a in x_parts) // seq_len
    bw = d // 2
    bf = lambda a: a.astype(MXU_DTYPE)
    row = lambda a: a.reshape(1, -1)

    lru_blocks = p['lru_w_a'].shape[1]
    gla_heads = GLA_HEADS
    gla_dk = p['gla_w_up'].shape[2] // gla_heads
    gla_dv = bw // gla_heads
    gla_rank = p['gla_w_up'].shape[1]
    dn_heads = p['dn_a_log'].shape[1]
    dn_dh = bw // dn_heads
    widths = (bw, bw, gla_heads * gla_dk, gla_heads * gla_dk, bw, bw, 2 * gla_rank, 3 * bw, bw, 4 * dn_heads, bw, bw)
    offs = [0]
    for wd in widths:
        offs.append(offs[-1] + wd)
    w_in = p['w_in']
    cols = lambda a, b: w_in[:, offs[a]:offs[b]]
    n_gate = 4 * dn_heads
    n_small = n_gate + 2 * gla_rank
    w_small = jnp.concatenate([cols(9, 10), cols(6, 7), jnp.zeros((d, LANES - n_small), w_in.dtype)], axis=1)
    w_cat = bf(jnp.concatenate([cols(0, 2), cols(2, 6), cols(7, 9), cols(10, 12), w_small], axis=1))
    out_w = (2 * bw, offs[6] - offs[2], 4 * bw, 2 * bw, LANES)
    s5_groups, s5_gw = p['s5_b_re'].shape[1], p['s5_b_re'].shape[3]
    lru_p, gla_p, dn_p, s5_p, small, s5_ug = _in_proj(x_parts, row(p['norm_g']), w_cat, out_w, seq_len, 3,
                                                      s5_groups, s5_gw)
    seq = lambda a: a.reshape(nb, seq_len, a.shape[1])
    flat = lambda a: a.reshape(nb * seq_len, a.shape[2])
    lru_s, gla_s, dn_s, small_s = seq(lru_p), seq(gla_p), seq(dn_p), seq(small)

    lanes_per_blk = bw // lru_blocks
    y_lru = None
    for dct, rev in enumerate((False, True)):
        wg = bf(jnp.concatenate([p['lru_w_a'][dct], p['lru_w_x'][dct]], axis=2))
        bg = jnp.concatenate([p['lru_b_a'][dct].reshape(lru_blocks, 1, lanes_per_blk),
                              p['lru_b_x'][dct].reshape(lru_blocks, 1, lanes_per_blk)], axis=2)
        y_lru = _lru_dir(rev, lru_s, y_lru, p['lru_conv_w'], row(p['lru_conv_b']), wg, bg, row(p['lru_lambda'][dct]))

    wup = jnp.zeros((2, LANES, gla_heads * gla_dk), F32)
    for dct in range(2):
        lo = n_gate + dct * gla_rank
        wup = wup.at[dct, lo:lo + gla_rank].set(p['gla_w_up'][dct])
    gla_f, gla_b = _gla_both(gla_s, small_s, bf(wup), p['gla_b_up'].reshape(2, 1, -1), gla_heads, gla_dk, gla_dv)

    alpha_lanes = (jnp.arange(2)[:, None] * 2 * dn_heads + dn_heads + jnp.arange(dn_heads)[None, :]).reshape(-1)
    beta_lanes = alpha_lanes - dn_heads
    gate_consts = jnp.zeros((SUBLANES, LANES), F32)
    gate_consts = gate_consts.at[0, alpha_lanes].set(p['dn_a_log'].reshape(-1))
    gate_consts = gate_consts.at[1, alpha_lanes].set(p['dn_dt_bias'].reshape(-1))
    gate_consts = gate_consts.at[2, beta_lanes].set(1.0).at[3, alpha_lanes].set(1.0)
    dn_f, dn_b = _dn_branch(dn_s, small_s, p['dn_conv_w'], gate_consts, dn_heads, dn_dh)

    s5_yg = _s5_toeplitz_branch(s5_ug, p['s5_lambda_re'], p['s5_lambda_im'], p['s5_log_dt'], p['s5_b_re'],
                                p['s5_b_im'], p['s5_c_re'], p['s5_c_im'])

    gla_gate_col = (offs[5] - offs[2]) // bw
    dn_gate_col = (offs[8] - offs[7]) // bw
    rest = (seq_len, flat(y_lru),
            (flat(gla_f), flat(gla_b), gla_p, gla_gate_col, row(p['gla_norm_g']), gla_heads),
            (flat(dn_f), flat(dn_b), dn_p, dn_gate_col, row(p['dn_norm_g']), dn_heads),
            (s5_yg, s5_p, row(p['s5_d']), bf(p['s5_w_glu']), row(p['s5_b_glu']), s5_gw), row(p['norm_g']),
            bf(p['w_merge_gate']), p['b_merge_gate'].reshape(4, 1, d), bf(p['w_branch']), bf(p['w_out']),
            row(final_g), final)
    if not final:
        return [_out_proj([_whole(a) for a in x_parts], 0, *rest)]
    outs, row0 = [], 0
    for gi, rows in enumerate(group_rows):
        x_range = (x_parts[0], row0, rows) if len(x_parts) == 1 else _whole(x_parts[gi])
        outs.append(_out_proj([x_range], row0, *rest))
        row0 += rows
    return outs


_LAYER_PARAM_NAMES = (
    'norm_g', 'w_in', 'lru_conv_w', 'lru_conv_b', 'lru_w_a', 'lru_b_a', 'lru_w_x', 'lru_b_x', 'lru_lambda',
    'gla_w_up', 'gla_b_up', 'gla_norm_g', 'dn_conv_w', 'dn_a_log', 'dn_dt_bias', 'dn_norm_g',
    's5_lambda_re', 's5_lambda_im', 's5_log_dt', 's5_b_re', 's5_b_im', 's5_c_re', 's5_c_im', 's5_d',
    's5_w_glu', 's5_b_glu', 'w_branch', 'w_merge_gate', 'b_merge_gate', 'w_out')


def kernel(x_prompt, x_sample, norm_g, w_in, lru_conv_w, lru_conv_b, lru_w_a, lru_b_a, lru_w_x, lru_b_x, lru_lambda, gla_w_up, gla_b_up, gla_norm_g, dn_conv_w, dn_a_log, dn_dt_bias, dn_norm_g, s5_lambda_re, s5_lambda_im, s5_log_dt, s5_b_re, s5_b_im, s5_c_re, s5_c_im, s5_d, s5_w_glu, s5_b_glu, w_branch, w_merge_gate, b_merge_gate, w_out, final_norm_g):
    stacked = (norm_g, w_in, lru_conv_w, lru_conv_b, lru_w_a, lru_b_a, lru_w_x, lru_b_x, lru_lambda,
               gla_w_up, gla_b_up, gla_norm_g, dn_conv_w, dn_a_log, dn_dt_bias, dn_norm_g,
               s5_lambda_re, s5_lambda_im, s5_log_dt, s5_b_re, s5_b_im, s5_c_re, s5_c_im, s5_d,
               s5_w_glu, s5_b_glu, w_branch, w_merge_gate, b_merge_gate, w_out)
    depth = norm_g.shape[0]
    seq_len, d = x_prompt.shape[1:]
    assert x_sample.shape[1] == seq_len
    groups = (x_prompt, x_sample)
    parts = [a.reshape(-1, d) for a in groups]
    group_rows = [a.shape[0] for a in parts]
    for layer in range(depth):
        p = {name: arr[layer] for name, arr in zip(_LAYER_PARAM_NAMES, stacked)}
        parts = _layer(parts, group_rows, seq_len, p, final_norm_g, layer == depth - 1)
    return tuple(y.reshape(a.shape) for y, a in zip(parts, groups))
```

```python
import functools
import math

import jax
import jax.numpy as jnp
from jax import lax
from jax.experimental import pallas as pl
from jax.experimental.pallas import tpu as pltpu

F32 = jnp.float32
MXU_DTYPE = jnp.bfloat16
HI = lax.Precision.HIGHEST

NORM_EPS = 1e-6
CONV_W = 4
CONV_HALO = 8
LRU_C = 8.0
GLA_TAU = 16.0
GLA_HEADS = 4
CHUNK = 64
LANES = 128
SUBLANES = 8
SEG_PAD = 4
VMEM_LIMIT = 56 * 1024 * 1024

TM_PROJ = 256
TL_MIX = 256
SOLVE_CHUNKS = 128


def _mm(a, b):
    return jnp.dot(a.astype(MXU_DTYPE), b.astype(MXU_DTYPE), preferred_element_type=F32)


def _mm_nt(a, b):
    return lax.dot_general(a.astype(MXU_DTYPE), b.astype(MXU_DTYPE), (((1,), (1,)), ((), ())),
                           preferred_element_type=F32)


def _mm_tn(a, b):
    return lax.dot_general(a.astype(MXU_DTYPE), b.astype(MXU_DTYPE), (((0,), (0,)), ((), ())),
                           preferred_element_type=F32)


def _cum_dot(cum3, x):
    hi = x.astype(jnp.bfloat16)
    r1 = x - hi.astype(F32)
    mid = r1.astype(jnp.bfloat16)
    lo = (r1 - mid.astype(F32)).astype(jnp.bfloat16)
    return jnp.dot(cum3, jnp.concatenate([hi, mid, lo], axis=0), preferred_element_type=F32)


def _sigmoid(x):
    return 1.0 / (1.0 + jnp.exp(-x))


def _silu(x):
    return x * _sigmoid(x)


def _softplus(x):
    return jnp.maximum(x, 0.0) + jnp.log1p(jnp.exp(-jnp.abs(x)))


def _rms(x, g):
    return x * lax.rsqrt(jnp.mean(x * x, axis=-1, keepdims=True) + NORM_EPS) * g


def _params(*sem):
    return pltpu.CompilerParams(dimension_semantics=sem, vmem_limit_bytes=VMEM_LIMIT)


def _const_spec(shape):
    nd = len(shape)
    return pl.BlockSpec(shape, lambda *_: (0,) * nd)


def _seq_pos(rev):
    if rev:
        return lambda i, n: n - 1 - i
    return lambda i, n: i


def _log2(n):
    assert n > 0 and n & (n - 1) == 0, n
    return n.bit_length() - 1


def _block_mask(nrow, ncol, row_blk, col_blk):
    r = jnp.right_shift(lax.broadcasted_iota(jnp.int32, (nrow, ncol), 0), _log2(row_blk))
    c = jnp.right_shift(lax.broadcasted_iota(jnp.int32, (nrow, ncol), 1), _log2(col_blk))
    return r == c


def _tri_mask_heads(rev, strict, heads):
    t = lax.broadcasted_iota(jnp.int32, (CHUNK, heads * CHUNK), 0)
    s = jnp.bitwise_and(lax.broadcasted_iota(jnp.int32, (CHUNK, heads * CHUNK), 1), CHUNK - 1)
    if rev:
        return (s > t) if strict else (s >= t)
    return (s < t) if strict else (s <= t)


def _block_cum(rev, tl):
    t = jnp.arange(tl)[:, None]
    s = jnp.arange(tl)[None, :]
    same = (t // CHUNK) == (s // CHUNK)
    cum = (same & ((s >= t) if rev else (s <= t))).astype(jnp.bfloat16)
    return jnp.concatenate([cum, cum, cum], axis=1)


def _chunk_order(rev, nchunk):
    return range(nchunk - 1, -1, -1) if rev else range(nchunk)


def _row_parts_specs(parts, width):
    specs, starts, step0 = [], [], 0
    for _, row0, rows in parts:
        steps, blk0 = rows // TM_PROJ, row0 // TM_PROJ
        specs.append(pl.BlockSpec(
            (TM_PROJ, width), lambda i, s=step0, n=steps, b=blk0: (b + jnp.clip(i - s, 0, n - 1), 0)))
        starts.append(step0)
        step0 += steps
    return specs, starts


def _whole(arr):
    return (arr, 0, arr.shape[0])


def _row_parts_value(refs, starts):
    i = pl.program_id(0)
    x = refs[0][...]
    for ref, s in zip(refs[1:], starts[1:]):
        x = jnp.where(i >= s, ref[...], x)
    return x


def _in_proj_kernel(widths, starts, s5_index, s5_gw, *refs):
    nparts = len(starts)
    g_ref, w_ref = refs[nparts:nparts + 2]
    o_refs = refs[nparts + 2:nparts + 2 + len(widths)]
    ug_ref, xs, acc = refs[nparts + 2 + len(widths):]
    xn = _rms(_row_parts_value(refs[:nparts], starts), g_ref[...]).astype(MXU_DTYPE)
    offs = [sum(widths[:n]) for n in range(len(widths))]
    for n in [s5_index] + [n for n in range(len(widths)) if n != s5_index]:
        y = jnp.dot(xn, w_ref[:, offs[n]:offs[n] + widths[n]], preferred_element_type=F32)
        o_refs[n][...] = y
        if n == s5_index:
            _rows_to_groups(y[:, 0:widths[n] // 2], s5_gw, ug_ref, xs, acc)


def _in_proj(x_parts, g, w_cat, widths, seq_len, s5_index, s5_groups, s5_gw):
    d = x_parts[0].shape[1]
    t = sum(a.shape[0] for a in x_parts)
    n = w_cat.shape[1]
    x_specs, starts = _row_parts_specs([_whole(a) for a in x_parts], d)
    per_seq = seq_len // TM_PROJ
    nrow = TM_PROJ // S5_CHUNK
    wl = S5_CHUNK * s5_gw
    s5_w = widths[s5_index] // 2
    return pl.pallas_call(
        functools.partial(_in_proj_kernel, widths, tuple(starts), s5_index, s5_gw),
        grid=(t // TM_PROJ,),
        in_specs=x_specs + [_const_spec((1, d)), _const_spec((d, n))],
        out_specs=[pl.BlockSpec((TM_PROJ, wd), lambda i: (i, 0)) for wd in widths] + [
            pl.BlockSpec((s5_groups, None, nrow, wl), lambda i: (0, i // per_seq, i % per_seq, 0))],
        out_shape=[jax.ShapeDtypeStruct((t, wd), F32) for wd in widths] + [
            jax.ShapeDtypeStruct((s5_groups, t // seq_len, seq_len // S5_CHUNK, wl), MXU_DTYPE)],
        scratch_shapes=[pltpu.VMEM((s5_w // LANES, TM_PROJ, LANES), F32), pltpu.VMEM((s5_groups, nrow, wl), F32)],
        compiler_params=_params("parallel"),
        name="in_proj",
    )(*x_parts, g, w_cat)


def _out_proj_kernel(final, gla_heads, dn_heads, s5_gw, starts, *refs):
    nparts = len(starts)
    (ylru_ref, gof_ref, gob_ref, ggate_ref, dof_ref, dob_ref, dgate_ref, s5y_ref, s5p_ref, dsk_ref, wglu_ref,
     bglu_ref, g_ref, gng_ref, dng_ref, wmg_ref, bmg_ref, wbr_ref, wout_ref, fg_ref, o_ref, s5_tmp, s5_nat
     ) = refs[nparts:]

    def finish(of_ref, ob_ref, gate_ref, ng_ref, heads):
        o = of_ref[...] + ob_ref[...]
        dv = o.shape[1] // heads
        normed = [_rms(o[:, h * dv:(h + 1) * dv], ng_ref[...]) for h in range(heads)]
        return jnp.concatenate(normed, axis=1) * _silu(gate_ref[...])

    x = _row_parts_value(refs[:nparts], starts)
    xn = _rms(x, g_ref[...]).astype(MXU_DTYPE)
    branches = (
        lambda: ylru_ref[...],
        lambda: finish(gof_ref, gob_ref, ggate_ref, gng_ref, gla_heads),
        lambda: finish(dof_ref, dob_ref, dgate_ref, dng_ref, dn_heads),
        lambda: _s5_finish(_groups_to_rows(s5y_ref, s5_gw, s5_tmp, s5_nat), s5p_ref[...], dsk_ref[...],
                           wglu_ref[...], bglu_ref[...]))
    merged = None
    for n in (0, 1, 3, 2):
        y = branches[n]()
        gate = _sigmoid(jnp.dot(xn, wmg_ref[n], preferred_element_type=F32) + bmg_ref[n])
        term = gate * _mm(y, wbr_ref[n])
        merged = term if merged is None else merged + term
    out = x + _mm(merged, wout_ref[...])
    if final:
        out = _rms(out, fg_ref[...])
    o_ref[...] = out


def _out_proj(x_parts, row0, seq_len, y_lru, gla, dn, s5, g, wmg, bmg, wbr, wout, fg, final):
    d = x_parts[0][0].shape[1]
    t = sum(rows for _, _, rows in x_parts)
    bw = y_lru.shape[1]
    blk0 = row0 // TM_PROJ
    row = lambda wd, col=0: pl.BlockSpec((TM_PROJ, wd), lambda i: (blk0 + i, col))
    g_of, g_ob, g_proj, g_col, g_ng, g_heads = gla
    d_of, d_ob, d_proj, d_col, d_ng, d_heads = dn
    s5_y, s5_proj, dsk, wglu, bglu, s5_gw = s5
    per_seq = seq_len // TM_PROJ
    s5_spec = pl.BlockSpec((s5_y.shape[0], None, TM_PROJ // S5_CHUNK, s5_y.shape[3]),
                           lambda i: (0, (blk0 + i) // per_seq, (blk0 + i) % per_seq, 0))
    x_specs, starts = _row_parts_specs(x_parts, d)
    return pl.pallas_call(
        functools.partial(_out_proj_kernel, final, g_heads, d_heads, s5_gw, tuple(starts)),
        grid=(t // TM_PROJ,),
        in_specs=x_specs + [row(bw), row(bw), row(bw), row(bw, g_col), row(bw), row(bw), row(bw, d_col),
                            s5_spec, row(2 * bw), _const_spec(dsk.shape), _const_spec(wglu.shape),
                            _const_spec(bglu.shape),
                            _const_spec((1, d)), _const_spec(g_ng.shape), _const_spec(d_ng.shape),
                            _const_spec(wmg.shape), _const_spec(bmg.shape), _const_spec(wbr.shape),
                            _const_spec(wout.shape), _const_spec((1, d))],
        out_specs=pl.BlockSpec((TM_PROJ, d), lambda i: (i, 0)),
        out_shape=jax.ShapeDtypeStruct((t, d), F32),
        scratch_shapes=[pltpu.VMEM((bw // LANES, TM_PROJ, LANES), F32), pltpu.VMEM((TM_PROJ, bw), F32)],
        compiler_params=_params("parallel"),
        name="out_proj",
    )(*[a for a, _, _ in x_parts], y_lru, g_of, g_ob, g_proj, d_of, d_ob, d_proj, s5_y, s5_proj, dsk, wglu, bglu,
      g, g_ng, d_ng, wmg, bmg, wbr, wout, fg)


def _halo_specs(rev, tl, width, seq_len):
    pos = _seq_pos(rev)
    per = tl // CONV_HALO
    last = seq_len // CONV_HALO - 1
    nblk = seq_len // tl
    prev = pl.BlockSpec((None, CONV_HALO, width),
                        lambda b, i: (b, jnp.maximum(pos(i, nblk) * per - 1, 0), 0))
    nxt = pl.BlockSpec((None, CONV_HALO, width),
                       lambda b, i: (b, jnp.minimum((pos(i, nblk) + 1) * per, last), 0))
    return prev, nxt


def _centred_conv(ext_ref, u, prev, nxt, at_start, at_end, w_ref):
    tl = u.shape[0]
    ext_ref[0:CONV_HALO, :] = jnp.where(at_start, 0.0, prev)
    ext_ref[CONV_HALO:CONV_HALO + tl, :] = u
    ext_ref[CONV_HALO + tl:, :] = jnp.where(at_end, 0.0, nxt)
    left = CONV_W // 2
    acc = None
    for j in range(CONV_W):
        term = ext_ref[pl.ds(CONV_HALO + j - left, tl), :] * w_ref[j:j + 1, :]
        acc = term if acc is None else acc + term
    return acc


def _lru_kernel(rev, final, tl, *refs):
    if final:
        (main_ref, prev_ref, next_ref, hf_ref, cw_ref, cb_ref, wg_ref, bg_ref, lam_ref,
         o_ref, carry_ref, ext_ref, a_scr, d_scr) = refs
    else:
        (main_ref, prev_ref, next_ref, cw_ref, cb_ref, wg_ref, bg_ref, lam_ref,
         o_ref, carry_ref, ext_ref, a_scr, d_scr) = refs
    i = pl.program_id(1)
    nblk = pl.num_programs(1)
    blk = _seq_pos(rev)(i, nblk)
    width = o_ref.shape[1]
    nslab = width // LANES
    seg = tl // SUBLANES
    pitch = seg + SEG_PAD

    @pl.when(i == 0)
    def _():
        carry_ref[...] = jnp.zeros_like(carry_ref)

    u = main_ref[:, 0:width]
    xc = _centred_conv(ext_ref, u, prev_ref[...], next_ref[...], blk == 0, blk == nblk - 1, cw_ref) + cb_ref[...]
    c_all = -LRU_C * _softplus(-lam_ref[...])
    for h in range(nslab):
        sl = slice(h * LANES, (h + 1) * LANES)
        xh = xc[:, sl]
        pre = _mm(xh, wg_ref[h]) + bg_ref[h]
        r = _sigmoid(pre[:, 0:LANES])
        ig = _sigmoid(pre[:, LANES:])
        a = jnp.exp(c_all[:, sl] * r)
        drive = jnp.sqrt(1.0 - a * a) * (ig * xh)
        for s in range(SUBLANES):
            a_scr[h, s * pitch:s * pitch + seg, :] = a[s * seg:(s + 1) * seg, :]
            d_scr[h, s * pitch:s * pitch + seg, :] = drive[s * seg:(s + 1) * seg, :]

    order = range(seg - 1, -1, -1) if rev else range(seg)
    for h in range(nslab):
        hloc = jnp.zeros((SUBLANES, LANES), F32)
        prod = jnp.ones((SUBLANES, LANES), F32)
        for k in order:
            ak = a_scr[h, pl.ds(k, SUBLANES, stride=pitch), :]
            dk = d_scr[h, pl.ds(k, SUBLANES, stride=pitch), :]
            hloc = ak * hloc + dk
            prod = prod * ak
            d_scr[h, pl.ds(k, SUBLANES, stride=pitch), :] = hloc
            a_scr[h, pl.ds(k, SUBLANES, stride=pitch), :] = prod

    last = 0 if rev else seg - 1
    seg_order = range(SUBLANES - 1, -1, -1) if rev else range(SUBLANES)
    for h in range(nslab):
        sl = slice(h * LANES, (h + 1) * LANES)
        c = carry_ref[0:1, sl]
        for s in seg_order:
            rows = slice(s * seg, (s + 1) * seg)
            hs = d_scr[h, s * pitch:s * pitch + seg, :] + a_scr[h, s * pitch:s * pitch + seg, :] * c
            if final:
                gate = main_ref[rows, width + h * LANES:width + (h + 1) * LANES]
                o_ref[rows, sl] = (hf_ref[rows, sl] + hs) * _silu(gate)
            else:
                o_ref[rows, sl] = hs
            c = a_scr[h, pl.ds(s * pitch + last, 1), :] * c + d_scr[h, pl.ds(s * pitch + last, 1), :]
        carry_ref[0:1, sl] = c


def _lru_dir(rev, proj, hf, cw, cb, wg, bg, lam):
    nb, seq_len, two_w = proj.shape
    width = two_w // 2
    tl = TL_MIX
    nblk = seq_len // tl
    pos = _seq_pos(rev)
    blk_spec = lambda wd: pl.BlockSpec((None, tl, wd), lambda b, i: (b, pos(i, nblk), 0))
    prev, nxt = _halo_specs(rev, tl, width, seq_len)
    final = hf is not None
    pitch = tl // SUBLANES + SEG_PAD
    ins = [proj, proj, proj] + ([hf] if final else []) + [cw, cb, wg, bg, lam]
    specs = [blk_spec(two_w), prev, nxt] + ([blk_spec(width)] if final else []) + [
        _const_spec(cw.shape), _const_spec(cb.shape), _const_spec(wg.shape), _const_spec(bg.shape),
        _const_spec(lam.shape)]
    return pl.pallas_call(
        functools.partial(_lru_kernel, rev, final, tl),
        grid=(nb, nblk),
        in_specs=specs,
        out_specs=blk_spec(width),
        out_shape=jax.ShapeDtypeStruct((nb, seq_len, width), F32),
        scratch_shapes=[pltpu.VMEM((SUBLANES, width), F32),
                        pltpu.VMEM((tl + 2 * CONV_HALO, width), F32),
                        pltpu.VMEM((width // LANES, SUBLANES * pitch, LANES), F32),
                        pltpu.VMEM((width // LANES, SUBLANES * pitch, LANES), F32)],
        compiler_params=_params("parallel", "arbitrary"),
        name="lru_bwd" if rev else "lru_fwd",
    )(*ins)


def _gla_kernel(tl, heads, dk, dv, mf_ref, mb_ref, sf_ref, sb_ref, wup_ref, bup_ref, cumf_ref, cumb_ref,
                of_ref, ob_ref, st_ref):
    i = pl.program_id(1)
    hk = heads * dk
    nchunk = tl // CHUNK

    @pl.when(i == 0)
    def _():
        st_ref[...] = jnp.zeros_like(st_ref)

    hv = heads * dv
    kk_blocks = _block_mask(heads * CHUNK, hk, CHUNK, dk).astype(MXU_DTYPE)
    kv_blocks = _block_mask(heads * CHUNK, hv, CHUNK, dv).astype(MXU_DTYPE)
    st_blocks = _block_mask(hv, hk, dv, dk).astype(F32)
    dirs = ((False, mf_ref, sf_ref, cumf_ref, of_ref), (True, mb_ref, sb_ref, cumb_ref, ob_ref))
    for d, (rev, m_ref, s_ref, cum_ref, o_ref) in enumerate(dirs):
        incl = _tri_mask_heads(rev, False, heads)
        end = 0 if rev else CHUNK - 1
        pre = _mm(s_ref[...], wup_ref[d]) + bup_ref[d]
        la = (jnp.minimum(pre, 0.0) - jnp.log1p(jnp.exp(-jnp.abs(pre)))) * (1.0 / GLA_TAU)
        b = _cum_dot(cum_ref[...], la)
        q_dec = (m_ref[:, 0:hk] * (dk ** -0.5) * jnp.exp(b)).astype(MXU_DTYPE)
        k = m_ref[:, hk:2 * hk]
        k_inv = (k * jnp.exp(-b)).astype(MXU_DTYPE)
        state = st_ref[d]
        for c in _chunk_order(rev, nchunk):
            rows = slice(c * CHUNK, (c + 1) * CHUNK)
            b_end = b[c * CHUNK + end:c * CHUNK + end + 1, :]
            k_end = k[rows, :] * jnp.exp(b_end - b[rows, :])
            v = m_ref[rows, 2 * hk:2 * hk + hv].astype(MXU_DTYPE)
            k_exp = jnp.concatenate([k_inv[rows, :]] * heads, axis=0) * kk_blocks
            scores = jnp.where(incl, _mm_nt(q_dec[rows, :], k_exp), 0.0)
            kv_t = _mm_tn(v, k_end)
            v_exp = jnp.concatenate([v] * heads, axis=0) * kv_blocks
            o_ref[rows, :] = _mm(scores, v_exp) + _mm_nt(q_dec[rows, :], state)
            state = state * jnp.exp(b_end) + kv_t * st_blocks
        st_ref[d] = state


def _gla_both(proj, small, wup, bup, heads, dk, dv):
    nb, seq_len, pw = proj.shape
    tl = TL_MIX
    nblk = seq_len // tl
    fwd = lambda wd: pl.BlockSpec((None, tl, wd), lambda b, i: (b, i, 0))
    bwd = lambda wd: pl.BlockSpec((None, tl, wd), lambda b, i: (b, nblk - 1 - i, 0))
    hv = heads * dv
    sw = small.shape[2]
    cum_f, cum_b = _block_cum(False, tl), _block_cum(True, tl)
    out = jax.ShapeDtypeStruct((nb, seq_len, hv), F32)
    return pl.pallas_call(
        functools.partial(_gla_kernel, tl, heads, dk, dv),
        grid=(nb, nblk),
        in_specs=[fwd(pw), bwd(pw), fwd(sw), bwd(sw), _const_spec(wup.shape), _const_spec(bup.shape),
                  _const_spec(cum_f.shape), _const_spec(cum_b.shape)],
        out_specs=[fwd(hv), bwd(hv)],
        out_shape=[out, out],
        scratch_shapes=[pltpu.VMEM((2, heads * dv, heads * dk), F32)],
        compiler_params=_params("parallel", "arbitrary"),
        name="gla",
    )(proj, proj, small, small, wup, bup, cum_f, cum_b)


def _dn_gates(small, gc_ref):
    beta = _sigmoid(small)
    g = -jnp.exp(gc_ref[0:1, :]) * _softplus(small + gc_ref[1:2, :])
    return jnp.where(gc_ref[2:3, :] > 0.5, beta, jnp.where(gc_ref[3:4, :] > 0.5, g, 0.0))


def _dn_forms(gx, d, heads):
    assert 2 * heads == SUBLANES and 2 * CHUNK == LANES
    base = 2 * heads * d
    rows8 = gx.T[base:base + SUBLANES, :]
    rows8_rot = pltpu.roll(rows8, CHUNK, axis=1)
    rep = lambda lane: jnp.broadcast_to(gx[:, lane:lane + 1], (gx.shape[0], LANES))
    return rows8, rows8_rot, [rep(base + h) for h in range(heads)], [rep(base + heads + h) for h in range(heads)]


def _head_rows(rows8, rows8_rot, r0, c, heads):
    nchunk = rows8.shape[1] // CHUNK
    low_half = lax.broadcasted_iota(jnp.int32, (1, LANES), 1) < CHUNK
    out = []
    for j in range(heads // 2):
        halves = []
        for h in (2 * j, 2 * j + 1):
            src, slot = (rows8, c) if c % 2 == h % 2 else (rows8_rot, (c + 1) % nchunk)
            halves.append(src[r0 + h:r0 + h + 1, LANES * (slot // 2):LANES * (slot // 2 + 1)])
        out.append(jnp.where(low_half, halves[0], halves[1]))
    return jnp.concatenate(out, axis=1)


def _head_cols(rep, c, heads):
    low_half = lax.broadcasted_iota(jnp.int32, (1, LANES), 1) < CHUNK
    rows = slice(c * CHUNK, (c + 1) * CHUNK)
    return jnp.concatenate([jnp.where(low_half, rep[2 * j][rows, :], rep[2 * j + 1][rows, :])
                            for j in range(heads // 2)], axis=1)


def _dn_gamma(rev, forms, c, heads):
    rows8, rows8_rot, _, g_rep = forms
    incl = _tri_mask_heads(rev, False, heads)
    diff = _head_cols(g_rep, c, heads) - _head_rows(rows8, rows8_rot, heads, c, heads)
    return jnp.where(incl, jnp.exp(jnp.where(incl, diff, 0.0)), 0.0)


def _head_expand(x, blocks, heads):
    return jnp.concatenate([x.astype(MXU_DTYPE)] * heads, axis=0) * blocks


def _dn_prep_kernel(tl, heads, dh, main_ref, prev_ref, next_ref, small_ref, cw_ref, gcst_ref, cumf_ref, cumb_ref,
                    qkv_ref, gx_ref, af_ref, ab_ref, ext_ref):
    i = pl.program_id(1)
    nblk = pl.num_programs(1)
    hw = heads * dh
    x = main_ref[:, 0:3 * hw]
    y = _silu(_centred_conv(ext_ref, x, prev_ref[...], next_ref[...], i == 0, i == nblk - 1, cw_ref))
    for h in range(heads):
        for part, scale in ((0, dh ** -0.5), (1, 1.0)):
            sl = slice(part * hw + h * dh, part * hw + (h + 1) * dh)
            z = y[:, sl]
            qkv_ref[:, sl] = z * lax.rsqrt(jnp.sum(z * z, axis=-1, keepdims=True) + NORM_EPS) * scale
    qkv_ref[:, 2 * hw:] = y[:, 2 * hw:]
    bg = _dn_gates(small_ref[...], gcst_ref)
    lane = lax.broadcasted_iota(jnp.int32, bg.shape, 1)
    run = jnp.where(lane < 2 * heads, _cum_dot(cumf_ref[...], bg), _cum_dot(cumb_ref[...], bg))
    gx = jnp.where(gcst_ref[3:4, :] > 0.5, run, bg)
    gx_ref[...] = gx
    blocks = _block_mask(heads * CHUNK, hw, CHUNK, dh).astype(MXU_DTYPE)
    for d, (rev, a_ref) in enumerate(((False, af_ref), (True, ab_ref))):
        strict = _tri_mask_heads(rev, True, heads)
        forms = _dn_forms(gx, d, heads)
        for c in range(tl // CHUNK):
            k = qkv_ref[c * CHUNK:(c + 1) * CHUNK, hw:2 * hw]
            kk = _mm_nt(k, _head_expand(k, blocks, heads))
            a_ref[c] = jnp.where(strict, kk * _head_cols(forms[2], c, heads) * _dn_gamma(rev, forms, c, heads), 0.0)


def _dn_prep(proj, small, cw, gate_consts, heads, dh):
    nb, seq_len, pw = proj.shape
    tl = TL_MIX
    nblk = seq_len // tl
    hw = heads * dh
    blk_spec = lambda wd: pl.BlockSpec((None, tl, wd), lambda b, i: (b, i, 0))
    prev, nxt = _halo_specs(False, tl, 3 * hw, seq_len)
    a_spec = pl.BlockSpec((None, tl // CHUNK, CHUNK, heads * CHUNK), lambda b, i: (b, i, 0, 0))
    a_shape = jax.ShapeDtypeStruct((nb, seq_len // CHUNK, CHUNK, heads * CHUNK), F32)
    lanes_shape = jax.ShapeDtypeStruct((nb, seq_len, LANES), F32)
    cum_f, cum_b = _block_cum(False, tl), _block_cum(True, tl)
    return pl.pallas_call(
        functools.partial(_dn_prep_kernel, tl, heads, dh),
        grid=(nb, nblk),
        in_specs=[blk_spec(pw), prev, nxt, blk_spec(small.shape[2]), _const_spec(cw.shape),
                  _const_spec(gate_consts.shape), _const_spec(cum_f.shape), _const_spec(cum_b.shape)],
        out_specs=[blk_spec(3 * hw), blk_spec(LANES), a_spec, a_spec],
        out_shape=[jax.ShapeDtypeStruct((nb, seq_len, 3 * hw), F32), lanes_shape, a_shape, a_shape],
        scratch_shapes=[pltpu.VMEM((tl + 2 * CONV_HALO, 3 * hw), F32)],
        compiler_params=_params("parallel", "parallel"),
        name="dn_prep",
    )(proj, proj, proj, small, cw, gate_consts, cum_f, cum_b)


def _dn_solve_kernel(upper, heads, a_ref, t_ref, a_scr, t_scr):
    nl = a_ref.shape[0]
    hc = heads * CHUNK
    half = CHUNK // 2
    for t in range(CHUNK):
        a_scr[t] = a_ref[:, t * hc:(t + 1) * hc].T.reshape(heads, CHUNK, nl)
    col = lax.broadcasted_iota(jnp.int32, (half, nl), 0)
    t_scr[...] = jnp.zeros_like(t_scr)

    def row(idx, carry):
        t = (CHUNK - 1 - idx) if upper else idx

        def term_lo(s, accs):
            return tuple(acc - a_scr[t, h, pl.ds(s, 1), :] * t_scr[s, h, 0:half, :] for h, acc in enumerate(accs))

        def term_hi(s, accs):
            return tuple(acc - a_scr[t, h, pl.ds(s, 1), :] * t_scr[s, h, half:, :] for h, acc in enumerate(accs))

        def pairs(term, count, first, step):
            def body(i, accs):
                s = first + 2 * step * i
                return term(s + step, term(s, accs))
            return functools.partial(lax.fori_loop, 0, (count + 1) // 2, body)

        if upper:
            lo_loop = pairs(term_lo, jnp.maximum(half - 1 - t, 0), half - 1, -1)
            hi_loop = pairs(term_hi, CHUNK - 1 - t, CHUNK - 1, -1)
        else:
            lo_loop = pairs(term_lo, t, 0, 1)
            hi_loop = pairs(term_hi, jnp.maximum(t - half, 0), half, 1)
        lo = lo_loop((jnp.where(col == t, 1.0, 0.0),) * heads)
        hi = hi_loop((jnp.where(col + half == t, 1.0, 0.0),) * heads)
        for h in range(heads):
            t_scr[t, h, 0:half, :] = lo[h]
            t_scr[t, h, half:, :] = hi[h]
        return carry

    lax.fori_loop(0, CHUNK, row, 0)
    for t in range(CHUNK):
        t_ref[:, t * hc:(t + 1) * hc] = t_scr[t].reshape(hc, nl).T.astype(t_ref.dtype)


def _dn_solve(a, upper, heads):
    nb, nc = a.shape[:2]
    flat = a.reshape(nb * nc, CHUNK * heads * CHUNK)
    nl = min(SOLVE_CHUNKS, nb * nc)
    spec = pl.BlockSpec((nl, flat.shape[1]), lambda i: (i, 0))
    scr = pltpu.VMEM((CHUNK, heads, CHUNK, nl), F32)
    return pl.pallas_call(
        functools.partial(_dn_solve_kernel, upper, heads),
        grid=(nb * nc // nl,),
        in_specs=[spec],
        out_specs=spec,
        out_shape=jax.ShapeDtypeStruct(flat.shape, MXU_DTYPE),
        scratch_shapes=[scr, scr],
        compiler_params=_params("parallel"),
        name="dn_solve_bwd" if upper else "dn_solve_fwd",
    )(flat).reshape(a.shape)


def _dn_main_kernel(tl, heads, dh, qf_ref, gxf_ref, tf_ref, qb_ref, gxb_ref, tb_ref, of_ref, ob_ref, s_ref):
    i = pl.program_id(1)
    hw = heads * dh
    nchunk = tl // CHUNK
    npair = heads // 2
    pw = 2 * dh

    @pl.when(i == 0)
    def _():
        s_ref[...] = jnp.zeros_like(s_ref)

    blocks = _block_mask(heads * CHUNK, hw, CHUNK, dh).astype(MXU_DTYPE)
    t_blocks = _block_mask(heads * CHUNK, heads * CHUNK, CHUNK, CHUNK).astype(MXU_DTYPE)
    pair_blocks = _block_mask(pw, pw, dh, dh).astype(F32)
    stack = lambda x: jnp.concatenate([x[:, h * dh:(h + 1) * dh] for h in range(heads)], axis=0)
    dirs = ((False, qf_ref, gxf_ref, tf_ref, of_ref), (True, qb_ref, gxb_ref, tb_ref, ob_ref))
    for d, (rev, qkv_ref, gx_ref, t_ref, o_ref) in enumerate(dirs):
        incl = _tri_mask_heads(rev, False, heads)
        end = 0 if rev else CHUNK - 1
        forms = _dn_forms(gx_ref[...], d, heads)
        b_nat = jnp.concatenate(forms[2], axis=1)
        g_nat = jnp.concatenate(forms[3], axis=1)
        eg_nat = jnp.exp(g_nat)
        states = [s_ref[d, j] for j in range(npair)]
        for c in _chunk_order(rev, nchunk):
            rows = slice(c * CHUNK, (c + 1) * CHUNK)
            q = qkv_ref[rows, 0:hw]
            k = qkv_ref[rows, hw:2 * hw]
            v = qkv_ref[rows, 2 * hw:3 * hw]
            eg = eg_nat[rows, :]
            beta = b_nat[rows, :]
            g_end = g_nat[c * CHUNK + end:c * CHUNK + end + 1, :]
            k_end = k * jnp.exp(g_end - g_nat[rows, :])
            e_end = jnp.exp(g_end)
            t_bd = _head_expand(t_ref[c], t_blocks, heads)
            wu = _mm(t_bd, jnp.concatenate([stack(k * beta * eg), stack(v * beta)], axis=1))
            attn = jnp.where(incl, _mm_nt(q, _head_expand(k, blocks, heads)) * _dn_gamma(rev, forms, c, heads), 0.0)
            w = jnp.concatenate([wu[h * CHUNK:(h + 1) * CHUNK, 0:dh] for h in range(heads)], axis=1)
            u = jnp.concatenate([wu[h * CHUNK:(h + 1) * CHUNK, dh:] for h in range(heads)], axis=1)
            lhs = jnp.concatenate([w, q * eg], axis=0)
            xs = [_mm(lhs[:, j * pw:(j + 1) * pw], states[j]) for j in range(npair)]
            v_new = u - jnp.concatenate([x[0:CHUNK, :] for x in xs], axis=1)
            o_ref[rows, :] = (jnp.concatenate([x[CHUNK:, :] for x in xs], axis=1)
                              + _mm(attn, _head_expand(v_new, blocks, heads)))
            for j in range(npair):
                ps = slice(j * pw, (j + 1) * pw)
                states[j] = states[j] * e_end[:, ps] + _mm_tn(k_end[:, ps], v_new[:, ps]) * pair_blocks
        for j in range(npair):
            s_ref[d, j] = states[j]


def _dn_main(qkv, gx, t_f, t_b, heads, dh):
    nb, seq_len, _ = qkv.shape
    tl = TL_MIX
    nblk = seq_len // tl
    hw = heads * dh
    fwd = lambda wd: pl.BlockSpec((None, tl, wd), lambda b, i: (b, i, 0))
    bwd = lambda wd: pl.BlockSpec((None, tl, wd), lambda b, i: (b, nblk - 1 - i, 0))
    t_blk = (None, tl // CHUNK, CHUNK, heads * CHUNK)
    t_fwd = pl.BlockSpec(t_blk, lambda b, i: (b, i, 0, 0))
    t_bwd = pl.BlockSpec(t_blk, lambda b, i: (b, nblk - 1 - i, 0, 0))
    out = jax.ShapeDtypeStruct((nb, seq_len, hw), F32)
    return pl.pallas_call(
        functools.partial(_dn_main_kernel, tl, heads, dh),
        grid=(nb, nblk),
        in_specs=[fwd(3 * hw), fwd(LANES), t_fwd, bwd(3 * hw), bwd(LANES), t_bwd],
        out_specs=[fwd(hw), bwd(hw)],
        out_shape=[out, out],
        scratch_shapes=[pltpu.VMEM((2, heads // 2, 2 * dh, 2 * dh), F32)],
        compiler_params=_params("parallel", "arbitrary"),
        name="dn_main",
    )(qkv, gx, t_f, qkv, gx, t_b)


def _dn_branch(proj, small, cw, gate_consts, heads, dh):
    qkv, gx, a_f, a_b = _dn_prep(proj, small, cw, gate_consts, heads, dh)
    return _dn_main(qkv, gx, _dn_solve(a_f, False, heads), _dn_solve(a_b, True, heads), heads, dh)


S5_CHUNK = 16


def _cpow(lam_re, lam_im, dt, e):
    mag = jnp.exp(lam_re * dt * e)
    ang = lam_im * dt * e
    return mag * jnp.cos(ang), mag * jnp.sin(ang)


def _s5_wt_kernel(gw, lre_ref, lim_ref, ldt_ref, bre_ref, bim_ref, ctre_ref, ctim_ref, cre_ref, cim_ref,
                  k_ref, zwre_ref, zwim_ref, cwre_ref, cwimn_ref):
    backward = pl.program_id(0) == 1
    lam_re, lam_im, dt = lre_ref[...], lim_ref[...], jnp.exp(ldt_ref[...])
    a_re, a_im = _cpow(lam_re, lam_im, dt, 1.0)
    den = lam_re * lam_re + lam_im * lam_im
    n_re = a_re - 1.0
    f_re = (n_re * lam_re + a_im * lam_im) / den
    f_im = (a_im * lam_re - n_re * lam_im) / den
    bb_re = f_re * bre_ref[...] - f_im * bim_ref[...]
    bb_im = f_re * bim_ref[...] + f_im * bre_ref[...]
    pos = jnp.right_shift(lax.broadcasted_iota(jnp.int32, lam_re.shape, 1), _log2(gw)).astype(F32)
    p_re, p_im = _cpow(lam_re, lam_im, dt, pos)
    e_re = p_re * bb_re - p_im * bb_im
    e_im = p_re * bb_im + p_im * bb_re
    k_ref[...] = (jnp.dot(cre_ref[...], e_re, precision=HI, preferred_element_type=F32)
                  - jnp.dot(cim_ref[...], e_im, precision=HI, preferred_element_type=F32))
    z_re, z_im = _cpow(lam_re, lam_im, dt, jnp.where(backward, pos, (S5_CHUNK - 1.0) - pos))
    zwre_ref[...] = z_re * bb_re - z_im * bb_im
    zwim_ref[...] = z_re * bb_im + z_im * bb_re
    c_re, c_im = _cpow(lam_re, lam_im, dt, jnp.where(backward, S5_CHUNK - pos, pos + 1.0))
    cwre_ref[...] = ctre_ref[...] * c_re - ctim_ref[...] * c_im
    cwimn_ref[...] = -(ctre_ref[...] * c_im + ctim_ref[...] * c_re)


def _s5_weights(lam_re, lam_im, log_dt, b_re, b_im, c_re, c_im):
    _, groups, states, gw = b_re.shape
    wl = S5_CHUNK * gw
    rep = lambda a: jnp.broadcast_to(a[..., None], (2, groups, states, wl))
    tile = lambda a: jnp.tile(a, (1, 1, 1, S5_CHUNK))
    blk = lambda r, c: pl.BlockSpec((None, None, r, c), lambda d, g: (d, g, 0, 0))
    ct = lambda a: tile(a.transpose(0, 1, 3, 2))
    big = jax.ShapeDtypeStruct((2, groups, states, wl), F32)
    return pl.pallas_call(
        functools.partial(_s5_wt_kernel, gw),
        grid=(2, groups),
        in_specs=[blk(states, wl)] * 7 + [blk(gw, states)] * 2,
        out_specs=[blk(gw, wl)] + [blk(states, wl)] * 4,
        out_shape=[jax.ShapeDtypeStruct((2, groups, gw, wl), F32), big, big, big, big],
        compiler_params=_params("parallel", "parallel"),
        name="s5_wt",
    )(rep(lam_re), rep(lam_im), rep(jnp.broadcast_to(log_dt[..., None], lam_re.shape)), tile(b_re), tile(b_im),
      ct(c_re), ct(c_im), c_re, c_im)


def _s5_pow_kernel(seg, lre_ref, lim_ref, ldt_ref, pw_ref):
    a_re, a_im = _cpow(lre_ref[...], lim_ref[...], jnp.exp(ldt_ref[...]), float(S5_CHUNK))
    x_re, x_im = a_re, a_im
    for k in range(seg):
        pw_ref[0, k] = x_re
        pw_ref[1, k] = x_im
        x_re, x_im = x_re * a_re - x_im * a_im, x_re * a_im + x_im * a_re


def _s5_pow(lre, lim, ldt, seg):
    return pl.pallas_call(
        functools.partial(_s5_pow_kernel, seg),
        out_shape=jax.ShapeDtypeStruct((2, seg) + lre.shape, F32),
        name="s5_pow",
    )(lre, lim, ldt)


def _gelu_tanh(x):
    return 0.5 * x * (1.0 + jnp.tanh(math.sqrt(2.0 / math.pi) * (x + 0.044715 * (x * x * x))))


def _s5_mix_kernel(nr, u_ref, tf_ref, tb_ref, zw_ref, cw_ref, pw_ref, y_ref, pad_re, pad_im, f_re, f_im, b_re, b_im):
    seg = nr // SUBLANES
    pitch = seg + SEG_PAD
    half = LANES // 2
    fwd_lanes = lax.broadcasted_iota(jnp.int32, (1, LANES), 1) < half
    w1 = jnp.concatenate([(tf_ref[...] + tb_ref[...]).astype(MXU_DTYPE), zw_ref[...]], axis=1)
    yz = jnp.dot(u_ref[...], w1, preferred_element_type=F32)
    wl = tf_ref.shape[1]

    for pad, col in ((pad_re, wl), (pad_im, wl + LANES)):
        pad[0:SUBLANES, :] = jnp.zeros((SUBLANES, LANES), F32)
        pad[SUBLANES:SUBLANES + nr, :] = yz[:, col:col + LANES]
        pad[SUBLANES + nr:, :] = jnp.zeros((SUBLANES, LANES), F32)
    for s in range(SUBLANES):
        dst = slice(s * pitch, s * pitch + seg)
        for pad, fw, bw in ((pad_re, f_re, b_re), (pad_im, f_im, b_im)):
            fw[dst, :] = pad[pl.ds(SUBLANES - 1 + s * seg, seg), :]
            bw[dst, :] = pad[pl.ds(SUBLANES + 1 + s * seg, seg), :]

    a_re = jnp.broadcast_to(pw_ref[0, 0:1, :], (SUBLANES, LANES))
    a_im = jnp.broadcast_to(pw_ref[1, 0:1, :], (SUBLANES, LANES))
    for re_scr, im_scr, order in ((f_re, f_im, range(seg)), (b_re, b_im, range(seg - 1, -1, -1))):
        sr = jnp.zeros((SUBLANES, LANES), F32)
        si = jnp.zeros((SUBLANES, LANES), F32)
        for k in order:
            sr, si = (a_re * sr - a_im * si + re_scr[pl.ds(k, SUBLANES, stride=pitch), :],
                      a_re * si + a_im * sr + im_scr[pl.ds(k, SUBLANES, stride=pitch), :])
            re_scr[pl.ds(k, SUBLANES, stride=pitch), :] = sr
            im_scr[pl.ds(k, SUBLANES, stride=pitch), :] = si

    pr, pi = pw_ref[2], pw_ref[3]
    sg_re = pw_ref[0, seg - 1:seg, :]
    sg_im = pw_ref[1, seg - 1:seg, :]
    zero = jnp.zeros((1, LANES), F32)
    cf, cb = [(zero, zero)], [(zero, zero)]
    for s in range(SUBLANES - 1):
        er, ei = f_re[pl.ds(s * pitch + seg - 1, 1), :], f_im[pl.ds(s * pitch + seg - 1, 1), :]
        cr, ci = cf[-1]
        cf.append((sg_re * cr - sg_im * ci + er, sg_re * ci + sg_im * cr + ei))
        t = SUBLANES - 1 - s
        er, ei = b_re[pl.ds(t * pitch, 1), :], b_im[pl.ds(t * pitch, 1), :]
        cr, ci = cb[-1]
        cb.append((sg_re * cr - sg_im * ci + er, sg_re * ci + sg_im * cr + ei))
    cb = cb[::-1]
    parts_re, parts_im = [], []
    for s in range(SUBLANES):
        src = slice(s * pitch, s * pitch + seg)
        cr = jnp.where(fwd_lanes, cf[s][0], cb[s][0])
        ci = jnp.where(fwd_lanes, cf[s][1], cb[s][1])
        parts_re.append(jnp.where(fwd_lanes, f_re[src, :], b_re[src, :]) + pr * cr - pi * ci)
        parts_im.append(jnp.where(fwd_lanes, f_im[src, :], b_im[src, :]) + pr * ci + pi * cr)
    states = jnp.concatenate([jnp.concatenate(parts_re, axis=0), jnp.concatenate(parts_im, axis=0)], axis=1)
    y_ref[...] = yz[:, 0:wl] + _mm(states, cw_ref[...])


def _s5_mix(u_g, toep_f, toep_b, zw, cw, pw):
    groups, nb, nr, wl = u_g.shape
    seg = nr // SUBLANES
    pitch = seg + SEG_PAD
    per_g = lambda r, c: pl.BlockSpec((None, r, c), lambda g, b: (g, 0, 0))
    rows = pl.BlockSpec((None, None, nr, wl), lambda g, b: (g, b, 0, 0))
    scan = pltpu.VMEM((SUBLANES * pitch, LANES), F32)
    pad = pltpu.VMEM((nr + 2 * SUBLANES, LANES), F32)
    return pl.pallas_call(
        functools.partial(_s5_mix_kernel, nr),
        grid=(groups, nb),
        in_specs=[rows, per_g(wl, wl), per_g(wl, wl), per_g(wl, 2 * LANES), per_g(2 * LANES, wl),
                  pl.BlockSpec((None, 4, seg, LANES), lambda g, b: (g, 0, 0, 0))],
        out_specs=rows,
        out_shape=jax.ShapeDtypeStruct((groups, nb, nr, wl), F32),
        scratch_shapes=[pad, pad, scan, scan, scan, scan],
        compiler_params=_params("parallel", "parallel"),
        name="s5_mix",
    )(u_g, toep_f, toep_b, zw, cw, pw)


def _rows_to_groups(x, gw, o_ref, xs, acc):
    per = LANES // gw
    nrow = x.shape[0] // S5_CHUNK
    nslab = x.shape[1] // LANES
    for v in range(nslab):
        xs[v] = x[:, v * LANES:(v + 1) * LANES]
    for r in range(S5_CHUNK):
        dst = (r // per) * LANES + (r % per) * gw
        for v in range(nslab):
            rows = xs[v, pl.ds(r, nrow, stride=S5_CHUNK), :]
            for s in range(per):
                rot = rows if s == 0 else pltpu.roll(rows, s * gw, axis=1)
                g = per * v + (r - s) % per
                acc[g, :, dst:dst + gw] = rot[:, (r % per) * gw:(r % per + 1) * gw]
    o_ref[...] = acc[...].astype(o_ref.dtype)


def _groups_to_rows(y_ref, gw, tmp, ynat):
    per = LANES // gw
    tl, width = ynat.shape
    nrow = tl // S5_CHUNK
    nslab = width // LANES
    for g in range(y_ref.shape[0]):
        v, slot = g // per, g % per
        for h in range(S5_CHUNK // per):
            rows = y_ref[g, :, h * LANES:(h + 1) * LANES]
            for s in range(per):
                rot = rows if s == 0 else pltpu.roll(rows, s * gw, axis=1)
                r = h * per + (slot - s) % per
                tmp[v, r * nrow:(r + 1) * nrow, slot * gw:(slot + 1) * gw] = rot[:, slot * gw:(slot + 1) * gw]
    for n in range(nrow):
        for r0 in range(0, S5_CHUNK, SUBLANES):
            for v in range(nslab):
                ynat[n * S5_CHUNK + r0:n * S5_CHUNK + r0 + SUBLANES, v * LANES:(v + 1) * LANES] = (
                    tmp[v, pl.ds(r0 * nrow + n, SUBLANES, stride=nrow), :])
    return ynat[...]


def _s5_finish(y, main, dsk, wglu, bglu):
    width = y.shape[1]
    z = _gelu_tanh(y + dsk * main[:, 0:width])
    z = z * _sigmoid(_mm(z, wglu) + bglu)
    return z * _silu(main[:, width:])


def _s5_toeplitz_branch(u_g, lam_re, lam_im, log_dt, b_re, b_im, c_re, c_im):
    _, groups, states, gw = b_re.shape
    nr = u_g.shape[2]
    seg = nr // SUBLANES
    wl = S5_CHUNK * gw
    kk, zw_re, zw_im, cw_re, cw_imn = _s5_weights(lam_re, lam_im, log_dt, b_re, b_im, c_re, c_im)

    lag = jnp.arange(S5_CHUNK)[None, :] - jnp.arange(S5_CHUNK)[:, None]
    kt = kk.reshape(2, groups, gw, S5_CHUNK, gw).transpose(0, 1, 3, 4, 2)

    def toeplitz(d, lag_d):
        t = jnp.where((lag_d >= 0)[None, :, :, None, None], kt[d][:, jnp.maximum(lag_d, 0)], 0.0)
        return t.transpose(0, 1, 3, 2, 4).reshape(groups, wl, wl)

    toep_f, toep_b = toeplitz(0, lag), toeplitz(1, -lag)
    by_rows = lambda a: a.transpose(0, 2, 1)
    zw = jnp.concatenate([by_rows(zw_re[0]), by_rows(zw_re[1]), by_rows(zw_im[0]), by_rows(zw_im[1])],
                         axis=2).astype(MXU_DTYPE)
    cw = jnp.concatenate([cw_re[0], cw_re[1], cw_imn[0], cw_imn[1]], axis=1).astype(MXU_DTYPE)

    flat = lambda a: a.reshape(2, groups * states // LANES, LANES)
    pw = _s5_pow(flat(lam_re), flat(lam_im), flat(jnp.broadcast_to(log_dt[..., None], lam_re.shape)), seg)
    pw = pw.reshape(2, seg, 2, groups, states).transpose(2, 3, 0, 1, 4)
    plain = jnp.concatenate([pw[0], pw[1]], axis=3)
    by_row = jnp.concatenate([pw[0], jnp.flip(pw[1], axis=2)], axis=3)
    pw_all = jnp.concatenate([plain, by_row], axis=1)

    return _s5_mix(u_g, toep_f, toep_b, zw, cw, pw_all)


def _layer(x_parts, group_rows, seq_len, p, final_g, final):
    d = x_parts[0].shape[1]
    nb = sum(a.shape[0] for a in x_parts) // seq_len
    bw = d // 2
    bf = lambda a: a.astype(MXU_DTYPE)
    row = lambda a: a.reshape(1, -1)

    lru_blocks = p['lru_w_a'].shape[1]
    gla_heads = GLA_HEADS
    gla_dk = p['gla_w_up'].shape[2] // gla_heads
    gla_dv = bw // gla_heads
    gla_rank = p['gla_w_up'].shape[1]
    dn_heads = p['dn_a_log'].shape[1]
    dn_dh = bw // dn_heads
    widths = (bw, bw, gla_heads * gla_dk, gla_heads * gla_dk, bw, bw, 2 * gla_rank, 3 * bw, bw, 4 * dn_heads, bw, bw)
    offs = [0]
    for wd in widths:
        offs.append(offs[-1] + wd)
    w_in = p['w_in']
    cols = lambda a, b: w_in[:, offs[a]:offs[b]]
    n_gate = 4 * dn_heads
    n_small = n_gate + 2 * gla_rank
    w_small = jnp.concatenate([cols(9, 10), cols(6, 7), jnp.zeros((d, LANES - n_small), w_in.dtype)], axis=1)
    w_cat = bf(jnp.concatenate([cols(0, 2), cols(2, 6), cols(7, 9), cols(10, 12), w_small], axis=1))
    out_w = (2 * bw, offs[6] - offs[2], 4 * bw, 2 * bw, LANES)
    s5_groups, s5_gw = p['s5_b_re'].shape[1], p['s5_b_re'].shape[3]
    lru_p, gla_p, dn_p, s5_p, small, s5_ug = _in_proj(x_parts, row(p['norm_g']), w_cat, out_w, seq_len, 3,
                                                      s5_groups, s5_gw)
    seq = lambda a: a.reshape(nb, seq_len, a.shape[1])
    flat = lambda a: a.reshape(nb * seq_len, a.shape[2])
    lru_s, gla_s, dn_s, small_s = seq(lru_p), seq(gla_p), seq(dn_p), seq(small)

    lanes_per_blk = bw // lru_blocks
    y_lru = None
    for dct, rev in enumerate((False, True)):
        wg = bf(jnp.concatenate([p['lru_w_a'][dct], p['lru_w_x'][dct]], axis=2))
        bg = jnp.concatenate([p['lru_b_a'][dct].reshape(lru_blocks, 1, lanes_per_blk),
                              p['lru_b_x'][dct].reshape(lru_blocks, 1, lanes_per_blk)], axis=2)
        y_lru = _lru_dir(rev, lru_s, y_lru, p['lru_conv_w'], row(p['lru_conv_b']), wg, bg, row(p['lru_lambda'][dct]))

    wup = jnp.zeros((2, LANES, gla_heads * gla_dk), F32)
    for dct in range(2):
        lo = n_gate + dct * gla_rank
        wup = wup.at[dct, lo:lo + gla_rank].set(p['gla_w_up'][dct])
    gla_f, gla_b = _gla_both(gla_s, small_s, bf(wup), p['gla_b_up'].reshape(2, 1, -1), gla_heads, gla_dk, gla_dv)

    alpha_lanes = (jnp.arange(2)[:, None] * 2 * dn_heads + dn_heads + jnp.arange(dn_heads)[None, :]).reshape(-1)
    beta_lanes = alpha_lanes - dn_heads
    gate_consts = jnp.zeros((SUBLANES, LANES), F32)
    gate_consts = gate_consts.at[0, alpha_lanes].set(p['dn_a_log'].reshape(-1))
    gate_consts = gate_consts.at[1, alpha_lanes].set(p['dn_dt_bias'].reshape(-1))
    gate_consts = gate_consts.at[2, beta_lanes].set(1.0).at[3, alpha_lanes].set(1.0)
    dn_f, dn_b = _dn_branch(dn_s, small_s, p['dn_conv_w'], gate_consts, dn_heads, dn_dh)

    s5_yg = _s5_toeplitz_branch(s5_ug, p['s5_lambda_re'], p['s5_lambda_im'], p['s5_log_dt'], p['s5_b_re'],
                                p['s5_b_im'], p['s5_c_re'], p['s5_c_im'])

    gla_gate_col = (offs[5] - offs[2]) // bw
    dn_gate_col = (offs[8] - offs[7]) // bw
    rest = (seq_len, flat(y_lru),
            (flat(gla_f), flat(gla_b), gla_p, gla_gate_col, row(p['gla_norm_g']), gla_heads),
            (flat(dn_f), flat(dn_b), dn_p, dn_gate_col, row(p['dn_norm_g']), dn_heads),
            (s5_yg, s5_p, row(p['s5_d']), bf(p['s5_w_glu']), row(p['s5_b_glu']), s5_gw), row(p['norm_g']),
            bf(p['w_merge_gate']), p['b_merge_gate'].reshape(4, 1, d), bf(p['w_branch']), bf(p['w_out']),
            row(final_g), final)
    if not final:
        return [_out_proj([_whole(a) for a in x_parts], 0, *rest)]
    outs, row0 = [], 0
    for gi, rows in enumerate(group_rows):
        x_range = (x_parts[0], row0, rows) if len(x_parts) == 1 else _whole(x_parts[gi])
        outs.append(_out_proj([x_range], row0, *rest))
        row0 += rows
    return outs


_LAYER_PARAM_NAMES = (
    'norm_g', 'w_in', 'lru_conv_w', 'lru_conv_b', 'lru_w_a', 'lru_b_a', 'lru_w_x', 'lru_b_x', 'lru_lambda',
    'gla_w_up', 'gla_b_up', 'gla_norm_g', 'dn_conv_w', 'dn_a_log', 'dn_dt_bias', 'dn_norm_g',
    's5_lambda_re', 's5_lambda_im', 's5_log_dt', 's5_b_re', 's5_b_im', 's5_c_re', 's5_c_im', 's5_d',
    's5_w_glu', 's5_b_glu', 'w_branch', 'w_merge_gate', 'b_merge_gate', 'w_out')


def kernel(x_prompt, x_sample, norm_g, w_in, lru_conv_w, lru_conv_b, lru_w_a, lru_b_a, lru_w_x, lru_b_x, lru_lambda, gla_w_up, gla_b_up, gla_norm_g, dn_conv_w, dn_a_log, dn_dt_bias, dn_norm_g, s5_lambda_re, s5_lambda_im, s5_log_dt, s5_b_re, s5_b_im, s5_c_re, s5_c_im, s5_d, s5_w_glu, s5_b_glu, w_branch, w_merge_gate, b_merge_gate, w_out, final_norm_g):
    stacked = (norm_g, w_in, lru_conv_w, lru_conv_b, lru_w_a, lru_b_a, lru_w_x, lru_b_x, lru_lambda,
               gla_w_up, gla_b_up, gla_norm_g, dn_conv_w, dn_a_log, dn_dt_bias, dn_norm_g,
               s5_lambda_re, s5_lambda_im, s5_log_dt, s5_b_re, s5_b_im, s5_c_re, s5_c_im, s5_d,
               s5_w_glu, s5_b_glu, w_branch, w_merge_gate, b_merge_gate, w_out)
    depth = norm_g.shape[0]
    seq_len, d = x_prompt.shape[1:]
    assert x_sample.shape[1] == seq_len
    groups = (x_prompt, x_sample)
    parts = [a.reshape(-1, d) for a in groups]
    group_rows = [a.shape[0] for a in parts]
    for layer in range(depth):
        p = {name: arr[layer] for name, arr in zip(_LAYER_PARAM_NAMES, stacked)}
        parts = _layer(parts, group_rows, seq_len, p, final_norm_g, layer == depth - 1)
    return tuple(y.reshape(a.shape) for y, a in zip(parts, groups))
```

```python
import functools
import math

import jax
import jax.numpy as jnp
from jax import lax
from jax.experimental import pallas as pl
from jax.experimental.pallas import tpu as pltpu

F32 = jnp.float32
MXU_DTYPE = jnp.bfloat16
HI = lax.Precision.HIGHEST

NORM_EPS = 1e-6
CONV_W = 4
CONV_HALO = 8
LRU_C = 8.0
GLA_TAU = 16.0
GLA_HEADS = 4
CHUNK = 64
LANES = 128
SUBLANES = 8
SEG_PAD = 4
VMEM_LIMIT = 56 * 1024 * 1024

TM_PROJ = 256
TL_MIX = 256
SOLVE_CHUNKS = 128


def _mm(a, b):
    return jnp.dot(a.astype(MXU_DTYPE), b.astype(MXU_DTYPE), preferred_element_type=F32)


def _mm_nt(a, b):
    return lax.dot_general(a.astype(MXU_DTYPE), b.astype(MXU_DTYPE), (((1,), (1,)), ((), ())),
                           preferred_element_type=F32)


def _mm_tn(a, b):
    return lax.dot_general(a.astype(MXU_DTYPE), b.astype(MXU_DTYPE), (((0,), (0,)), ((), ())),
                           preferred_element_type=F32)


def _cum_dot(cum3, x):
    hi = x.astype(jnp.bfloat16)
    r1 = x - hi.astype(F32)
    mid = r1.astype(jnp.bfloat16)
    lo = (r1 - mid.astype(F32)).astype(jnp.bfloat16)
    return jnp.dot(cum3, jnp.concatenate([hi, mid, lo], axis=0), preferred_element_type=F32)


def _sigmoid(x):
    return 0.5 * jnp.tanh(0.5 * x) + 0.5


def _silu(x):
    return x * _sigmoid(x)


def _softplus(x):
    return jnp.maximum(x, 0.0) + jnp.log1p(jnp.exp(-jnp.abs(x)))


def _rms(x, g):
    return x * lax.rsqrt(jnp.mean(x * x, axis=-1, keepdims=True) + NORM_EPS) * g


def _params(*sem):
    return pltpu.CompilerParams(dimension_semantics=sem, vmem_limit_bytes=VMEM_LIMIT)


def _const_spec(shape):
    nd = len(shape)
    return pl.BlockSpec(shape, lambda *_: (0,) * nd)


def _seq_pos(rev):
    if rev:
        return lambda i, n: n - 1 - i
    return lambda i, n: i


def _log2(n):
    assert n > 0 and n & (n - 1) == 0, n
    return n.bit_length() - 1


def _block_mask(nrow, ncol, row_blk, col_blk):
    r = jnp.right_shift(lax.broadcasted_iota(jnp.int32, (nrow, ncol), 0), _log2(row_blk))
    c = jnp.right_shift(lax.broadcasted_iota(jnp.int32, (nrow, ncol), 1), _log2(col_blk))
    return r == c


def _tri_mask_heads(rev, strict, heads):
    t = lax.broadcasted_iota(jnp.int32, (CHUNK, heads * CHUNK), 0)
    s = jnp.bitwise_and(lax.broadcasted_iota(jnp.int32, (CHUNK, heads * CHUNK), 1), CHUNK - 1)
    if rev:
        return (s > t) if strict else (s >= t)
    return (s < t) if strict else (s <= t)


def _block_cum(rev, tl):
    t = jnp.arange(tl)[:, None]
    s = jnp.arange(tl)[None, :]
    same = (t // CHUNK) == (s // CHUNK)
    cum = (same & ((s >= t) if rev else (s <= t))).astype(jnp.bfloat16)
    return jnp.concatenate([cum, cum, cum], axis=1)


def _chunk_order(rev, nchunk):
    return range(nchunk - 1, -1, -1) if rev else range(nchunk)


def _row_parts_specs(parts, width):
    specs, starts, step0 = [], [], 0
    for _, row0, rows in parts:
        steps, blk0 = rows // TM_PROJ, row0 // TM_PROJ
        specs.append(pl.BlockSpec(
            (TM_PROJ, width), lambda i, s=step0, n=steps, b=blk0: (b + jnp.clip(i - s, 0, n - 1), 0)))
        starts.append(step0)
        step0 += steps
    return specs, starts


def _whole(arr):
    return (arr, 0, arr.shape[0])


def _row_parts_value(refs, starts):
    i = pl.program_id(0)
    x = refs[0][...]
    for ref, s in zip(refs[1:], starts[1:]):
        x = jnp.where(i >= s, ref[...], x)
    return x


def _in_proj_kernel(widths, starts, s5_index, s5_gw, *refs):
    nparts = len(starts)
    g_ref, w_ref = refs[nparts:nparts + 2]
    o_refs = refs[nparts + 2:nparts + 2 + len(widths)]
    ug_ref, xs, acc = refs[nparts + 2 + len(widths):]
    xn = _rms(_row_parts_value(refs[:nparts], starts), g_ref[...]).astype(MXU_DTYPE)
    offs = [sum(widths[:n]) for n in range(len(widths))]
    for n in [s5_index] + [n for n in range(len(widths)) if n != s5_index]:
        y = jnp.dot(xn, w_ref[:, offs[n]:offs[n] + widths[n]], preferred_element_type=F32)
        o_refs[n][...] = y
        if n == s5_index:
            _rows_to_groups(y[:, 0:widths[n] // 2], s5_gw, ug_ref, xs, acc)


def _in_proj(x_parts, g, w_cat, widths, seq_len, s5_index, s5_groups, s5_gw):
    d = x_parts[0].shape[1]
    t = sum(a.shape[0] for a in x_parts)
    n = w_cat.shape[1]
    x_specs, starts = _row_parts_specs([_whole(a) for a in x_parts], d)
    per_seq = seq_len // TM_PROJ
    nrow = TM_PROJ // S5_CHUNK
    wl = S5_CHUNK * s5_gw
    s5_w = widths[s5_index] // 2
    return pl.pallas_call(
        functools.partial(_in_proj_kernel, widths, tuple(starts), s5_index, s5_gw),
        grid=(t // TM_PROJ,),
        in_specs=x_specs + [_const_spec((1, d)), _const_spec((d, n))],
        out_specs=[pl.BlockSpec((TM_PROJ, wd), lambda i: (i, 0)) for wd in widths] + [
            pl.BlockSpec((s5_groups, None, nrow, wl), lambda i: (0, i // per_seq, i % per_seq, 0))],
        out_shape=[jax.ShapeDtypeStruct((t, wd), F32) for wd in widths] + [
            jax.ShapeDtypeStruct((s5_groups, t // seq_len, seq_len // S5_CHUNK, wl), MXU_DTYPE)],
        scratch_shapes=[pltpu.VMEM((s5_w // LANES, TM_PROJ, LANES), F32), pltpu.VMEM((s5_groups, nrow, wl), F32)],
        compiler_params=_params("parallel"),
        name="in_proj",
    )(*x_parts, g, w_cat)


def _out_proj_kernel(final, gla_heads, dn_heads, s5_gw, starts, *refs):
    nparts = len(starts)
    (ylru_ref, gof_ref, gob_ref, ggate_ref, dof_ref, dob_ref, dgate_ref, s5y_ref, s5p_ref, dsk_ref, wglu_ref,
     bglu_ref, g_ref, gng_ref, dng_ref, wmg_ref, bmg_ref, wbr_ref, wout_ref, fg_ref, o_ref, s5_tmp, s5_nat
     ) = refs[nparts:]

    def finish(of_ref, ob_ref, gate_ref, ng_ref, heads):
        o = of_ref[...] + ob_ref[...]
        dv = o.shape[1] // heads
        normed = [_rms(o[:, h * dv:(h + 1) * dv], ng_ref[...]) for h in range(heads)]
        return jnp.concatenate(normed, axis=1) * _silu(gate_ref[...])

    x = _row_parts_value(refs[:nparts], starts)
    xn = _rms(x, g_ref[...]).astype(MXU_DTYPE)
    branches = (
        lambda: ylru_ref[...],
        lambda: finish(gof_ref, gob_ref, ggate_ref, gng_ref, gla_heads),
        lambda: finish(dof_ref, dob_ref, dgate_ref, dng_ref, dn_heads),
        lambda: _s5_finish(_groups_to_rows(s5y_ref, s5_gw, s5_tmp, s5_nat), s5p_ref[...], dsk_ref[...],
                           wglu_ref[...], bglu_ref[...]))
    merged = None
    for n in (0, 1, 3, 2):
        y = branches[n]()
        gate = _sigmoid(jnp.dot(xn, wmg_ref[n], preferred_element_type=F32) + bmg_ref[n])
        term = gate * _mm(y, wbr_ref[n])
        merged = term if merged is None else merged + term
    out = x + _mm(merged, wout_ref[...])
    if final:
        out = _rms(out, fg_ref[...])
    o_ref[...] = out


def _out_proj(x_parts, row0, seq_len, y_lru, gla, dn, s5, g, wmg, bmg, wbr, wout, fg, final):
    d = x_parts[0][0].shape[1]
    t = sum(rows for _, _, rows in x_parts)
    bw = y_lru.shape[1]
    blk0 = row0 // TM_PROJ
    row = lambda wd, col=0: pl.BlockSpec((TM_PROJ, wd), lambda i: (blk0 + i, col))
    g_of, g_ob, g_proj, g_col, g_ng, g_heads = gla
    d_of, d_ob, d_proj, d_col, d_ng, d_heads = dn
    s5_y, s5_proj, dsk, wglu, bglu, s5_gw = s5
    per_seq = seq_len // TM_PROJ
    s5_spec = pl.BlockSpec((s5_y.shape[0], None, TM_PROJ // S5_CHUNK, s5_y.shape[3]),
                           lambda i: (0, (blk0 + i) // per_seq, (blk0 + i) % per_seq, 0))
    x_specs, starts = _row_parts_specs(x_parts, d)
    return pl.pallas_call(
        functools.partial(_out_proj_kernel, final, g_heads, d_heads, s5_gw, tuple(starts)),
        grid=(t // TM_PROJ,),
        in_specs=x_specs + [row(bw), row(bw), row(bw), row(bw, g_col), row(bw), row(bw), row(bw, d_col),
                            s5_spec, row(2 * bw), _const_spec(dsk.shape), _const_spec(wglu.shape),
                            _const_spec(bglu.shape),
                            _const_spec((1, d)), _const_spec(g_ng.shape), _const_spec(d_ng.shape),
                            _const_spec(wmg.shape), _const_spec(bmg.shape), _const_spec(wbr.shape),
                            _const_spec(wout.shape), _const_spec((1, d))],
        out_specs=pl.BlockSpec((TM_PROJ, d), lambda i: (i, 0)),
        out_shape=jax.ShapeDtypeStruct((t, d), F32),
        scratch_shapes=[pltpu.VMEM((bw // LANES, TM_PROJ, LANES), F32), pltpu.VMEM((TM_PROJ, bw), F32)],
        compiler_params=_params("parallel"),
        name="out_proj",
    )(*[a for a, _, _ in x_parts], y_lru, g_of, g_ob, g_proj, d_of, d_ob, d_proj, s5_y, s5_proj, dsk, wglu, bglu,
      g, g_ng, d_ng, wmg, bmg, wbr, wout, fg)


def _halo_specs(rev, tl, width, seq_len):
    pos = _seq_pos(rev)
    per = tl // CONV_HALO
    last = seq_len // CONV_HALO - 1
    nblk = seq_len // tl
    prev = pl.BlockSpec((None, CONV_HALO, width),
                        lambda b, i: (b, jnp.maximum(pos(i, nblk) * per - 1, 0), 0))
    nxt = pl.BlockSpec((None, CONV_HALO, width),
                       lambda b, i: (b, jnp.minimum((pos(i, nblk) + 1) * per, last), 0))
    return prev, nxt


def _centred_conv(ext_ref, u, prev, nxt, at_start, at_end, w_ref):
    tl = u.shape[0]
    ext_ref[0:CONV_HALO, :] = jnp.where(at_start, 0.0, prev)
    ext_ref[CONV_HALO:CONV_HALO + tl, :] = u
    ext_ref[CONV_HALO + tl:, :] = jnp.where(at_end, 0.0, nxt)
    left = CONV_W // 2
    acc = None
    for j in range(CONV_W):
        term = ext_ref[pl.ds(CONV_HALO + j - left, tl), :] * w_ref[j:j + 1, :]
        acc = term if acc is None else acc + term
    return acc


def _lru_kernel(rev, final, tl, *refs):
    if final:
        (main_ref, prev_ref, next_ref, hf_ref, cw_ref, cb_ref, wg_ref, bg_ref, lam_ref,
         o_ref, carry_ref, ext_ref, a_scr, d_scr) = refs
    else:
        (main_ref, prev_ref, next_ref, cw_ref, cb_ref, wg_ref, bg_ref, lam_ref,
         o_ref, carry_ref, ext_ref, a_scr, d_scr) = refs
    i = pl.program_id(1)
    nblk = pl.num_programs(1)
    blk = _seq_pos(rev)(i, nblk)
    width = o_ref.shape[1]
    nslab = width // LANES
    seg = tl // SUBLANES
    pitch = seg + SEG_PAD

    @pl.when(i == 0)
    def _():
        carry_ref[...] = jnp.zeros_like(carry_ref)

    u = main_ref[:, 0:width]
    xc = _centred_conv(ext_ref, u, prev_ref[...], next_ref[...], blk == 0, blk == nblk - 1, cw_ref) + cb_ref[...]
    c_all = -LRU_C * _softplus(-lam_ref[...])
    for h in range(nslab):
        sl = slice(h * LANES, (h + 1) * LANES)
        xh = xc[:, sl]
        pre = _mm(xh, wg_ref[h]) + bg_ref[h]
        r = _sigmoid(pre[:, 0:LANES])
        ig = _sigmoid(pre[:, LANES:])
        a = jnp.exp(c_all[:, sl] * r)
        drive = jnp.sqrt(1.0 - a * a) * (ig * xh)
        for s in range(SUBLANES):
            a_scr[h, s * pitch:s * pitch + seg, :] = a[s * seg:(s + 1) * seg, :]
            d_scr[h, s * pitch:s * pitch + seg, :] = drive[s * seg:(s + 1) * seg, :]

    order = range(seg - 1, -1, -1) if rev else range(seg)
    for h in range(nslab):
        hloc = jnp.zeros((SUBLANES, LANES), F32)
        prod = jnp.ones((SUBLANES, LANES), F32)
        for k in order:
            ak = a_scr[h, pl.ds(k, SUBLANES, stride=pitch), :]
            dk = d_scr[h, pl.ds(k, SUBLANES, stride=pitch), :]
            hloc = ak * hloc + dk
            prod = prod * ak
            d_scr[h, pl.ds(k, SUBLANES, stride=pitch), :] = hloc
            a_scr[h, pl.ds(k, SUBLANES, stride=pitch), :] = prod

    last = 0 if rev else seg - 1
    seg_order = range(SUBLANES - 1, -1, -1) if rev else range(SUBLANES)
    for h in range(nslab):
        sl = slice(h * LANES, (h + 1) * LANES)
        c = carry_ref[0:1, sl]
        for s in seg_order:
            rows = slice(s * seg, (s + 1) * seg)
            hs = d_scr[h, s * pitch:s * pitch + seg, :] + a_scr[h, s * pitch:s * pitch + seg, :] * c
            if final:
                gate = main_ref[rows, width + h * LANES:width + (h + 1) * LANES]
                o_ref[rows, sl] = (hf_ref[rows, sl] + hs) * _silu(gate)
            else:
                o_ref[rows, sl] = hs
            c = a_scr[h, pl.ds(s * pitch + last, 1), :] * c + d_scr[h, pl.ds(s * pitch + last, 1), :]
        carry_ref[0:1, sl] = c


def _lru_dir(rev, proj, hf, cw, cb, wg, bg, lam):
    nb, seq_len, two_w = proj.shape
    width = two_w // 2
    tl = TL_MIX
    nblk = seq_len // tl
    pos = _seq_pos(rev)
    blk_spec = lambda wd: pl.BlockSpec((None, tl, wd), lambda b, i: (b, pos(i, nblk), 0))
    prev, nxt = _halo_specs(rev, tl, width, seq_len)
    final = hf is not None
    pitch = tl // SUBLANES + SEG_PAD
    ins = [proj, proj, proj] + ([hf] if final else []) + [cw, cb, wg, bg, lam]
    specs = [blk_spec(two_w), prev, nxt] + ([blk_spec(width)] if final else []) + [
        _const_spec(cw.shape), _const_spec(cb.shape), _const_spec(wg.shape), _const_spec(bg.shape),
        _const_spec(lam.shape)]
    return pl.pallas_call(
        functools.partial(_lru_kernel, rev, final, tl),
        grid=(nb, nblk),
        in_specs=specs,
        out_specs=blk_spec(width),
        out_shape=jax.ShapeDtypeStruct((nb, seq_len, width), F32),
        scratch_shapes=[pltpu.VMEM((SUBLANES, width), F32),
                        pltpu.VMEM((tl + 2 * CONV_HALO, width), F32),
                        pltpu.VMEM((width // LANES, SUBLANES * pitch, LANES), F32),
                        pltpu.VMEM((width // LANES, SUBLANES * pitch, LANES), F32)],
        compiler_params=_params("parallel", "arbitrary"),
        name="lru_bwd" if rev else "lru_fwd",
    )(*ins)


def _gla_kernel(tl, heads, dk, dv, mf_ref, mb_ref, sf_ref, sb_ref, wup_ref, bup_ref, cumf_ref, cumb_ref,
                of_ref, ob_ref, st_ref):
    i = pl.program_id(1)
    hk = heads * dk
    nchunk = tl // CHUNK

    @pl.when(i == 0)
    def _():
        st_ref[...] = jnp.zeros_like(st_ref)

    hv = heads * dv
    kk_blocks = _block_mask(heads * CHUNK, hk, CHUNK, dk).astype(MXU_DTYPE)
    kv_blocks = _block_mask(heads * CHUNK, hv, CHUNK, dv).astype(MXU_DTYPE)
    st_blocks = _block_mask(hv, hk, dv, dk).astype(F32)
    dirs = ((False, mf_ref, sf_ref, cumf_ref, of_ref), (True, mb_ref, sb_ref, cumb_ref, ob_ref))
    for d, (rev, m_ref, s_ref, cum_ref, o_ref) in enumerate(dirs):
        incl = _tri_mask_heads(rev, False, heads)
        end = 0 if rev else CHUNK - 1
        pre = _mm(s_ref[...], wup_ref[d]) + bup_ref[d]
        la = (jnp.minimum(pre, 0.0) - jnp.log1p(jnp.exp(-jnp.abs(pre)))) * (1.0 / GLA_TAU)
        b = _cum_dot(cum_ref[...], la)
        q_dec = (m_ref[:, 0:hk] * (dk ** -0.5) * jnp.exp(b)).astype(MXU_DTYPE)
        k = m_ref[:, hk:2 * hk]
        k_inv = (k * jnp.exp(-b)).astype(MXU_DTYPE)
        state = st_ref[d]
        for c in _chunk_order(rev, nchunk):
            rows = slice(c * CHUNK, (c + 1) * CHUNK)
            b_end = b[c * CHUNK + end:c * CHUNK + end + 1, :]
            k_end = k[rows, :] * jnp.exp(b_end - b[rows, :])
            v = m_ref[rows, 2 * hk:2 * hk + hv].astype(MXU_DTYPE)
            k_exp = jnp.concatenate([k_inv[rows, :]] * heads, axis=0) * kk_blocks
            scores = jnp.where(incl, _mm_nt(q_dec[rows, :], k_exp), 0.0)
            kv_t = _mm_tn(v, k_end)
            v_exp = jnp.concatenate([v] * heads, axis=0) * kv_blocks
            o_ref[rows, :] = _mm(scores, v_exp) + _mm_nt(q_dec[rows, :], state)
            state = state * jnp.exp(b_end) + kv_t * st_blocks
        st_ref[d] = state


def _gla_both(proj, small, wup, bup, heads, dk, dv):
    nb, seq_len, pw = proj.shape
    tl = TL_MIX
    nblk = seq_len // tl
    fwd = lambda wd: pl.BlockSpec((None, tl, wd), lambda b, i: (b, i, 0))
    bwd = lambda wd: pl.BlockSpec((None, tl, wd), lambda b, i: (b, nblk - 1 - i, 0))
    hv = heads * dv
    sw = small.shape[2]
    cum_f, cum_b = _block_cum(False, tl), _block_cum(True, tl)
    out = jax.ShapeDtypeStruct((nb, seq_len, hv), F32)
    return pl.pallas_call(
        functools.partial(_gla_kernel, tl, heads, dk, dv),
        grid=(nb, nblk),
        in_specs=[fwd(pw), bwd(pw), fwd(sw), bwd(sw), _const_spec(wup.shape), _const_spec(bup.shape),
                  _const_spec(cum_f.shape), _const_spec(cum_b.shape)],
        out_specs=[fwd(hv), bwd(hv)],
        out_shape=[out, out],
        scratch_shapes=[pltpu.VMEM((2, heads * dv, heads * dk), F32)],
        compiler_params=_params("parallel", "arbitrary"),
        name="gla",
    )(proj, proj, small, small, wup, bup, cum_f, cum_b)


def _dn_gates(small, gc_ref):
    beta = _sigmoid(small)
    g = -jnp.exp(gc_ref[0:1, :]) * _softplus(small + gc_ref[1:2, :])
    return jnp.where(gc_ref[2:3, :] > 0.5, beta, jnp.where(gc_ref[3:4, :] > 0.5, g, 0.0))


def _dn_forms(gx, d, heads):
    assert 2 * heads == SUBLANES and 2 * CHUNK == LANES
    base = 2 * heads * d
    rows8 = gx.T[base:base + SUBLANES, :]
    rows8_rot = pltpu.roll(rows8, CHUNK, axis=1)
    rep = lambda lane: jnp.broadcast_to(gx[:, lane:lane + 1], (gx.shape[0], LANES))
    return rows8, rows8_rot, [rep(base + h) for h in range(heads)], [rep(base + heads + h) for h in range(heads)]


def _head_rows(rows8, rows8_rot, r0, c, heads):
    nchunk = rows8.shape[1] // CHUNK
    low_half = lax.broadcasted_iota(jnp.int32, (1, LANES), 1) < CHUNK
    out = []
    for j in range(heads // 2):
        halves = []
        for h in (2 * j, 2 * j + 1):
            src, slot = (rows8, c) if c % 2 == h % 2 else (rows8_rot, (c + 1) % nchunk)
            halves.append(src[r0 + h:r0 + h + 1, LANES * (slot // 2):LANES * (slot // 2 + 1)])
        out.append(jnp.where(low_half, halves[0], halves[1]))
    return jnp.concatenate(out, axis=1)


def _head_cols(rep, c, heads):
    low_half = lax.broadcasted_iota(jnp.int32, (1, LANES), 1) < CHUNK
    rows = slice(c * CHUNK, (c + 1) * CHUNK)
    return jnp.concatenate([jnp.where(low_half, rep[2 * j][rows, :], rep[2 * j + 1][rows, :])
                            for j in range(heads // 2)], axis=1)


def _dn_gamma(rev, forms, c, heads):
    rows8, rows8_rot, _, g_rep = forms
    incl = _tri_mask_heads(rev, False, heads)
    diff = _head_cols(g_rep, c, heads) - _head_rows(rows8, rows8_rot, heads, c, heads)
    return jnp.where(incl, jnp.exp(jnp.where(incl, diff, 0.0)), 0.0)


def _head_expand(x, blocks, heads):
    return jnp.concatenate([x.astype(MXU_DTYPE)] * heads, axis=0) * blocks


def _dn_prep_kernel(tl, heads, dh, main_ref, prev_ref, next_ref, small_ref, cw_ref, gcst_ref, cumf_ref, cumb_ref,
                    qkv_ref, gx_ref, af_ref, ab_ref, ext_ref):
    i = pl.program_id(1)
    nblk = pl.num_programs(1)
    hw = heads * dh
    x = main_ref[:, 0:3 * hw]
    y = _silu(_centred_conv(ext_ref, x, prev_ref[...], next_ref[...], i == 0, i == nblk - 1, cw_ref))
    for h in range(heads):
        for part, scale in ((0, dh ** -0.5), (1, 1.0)):
            sl = slice(part * hw + h * dh, part * hw + (h + 1) * dh)
            z = y[:, sl]
            qkv_ref[:, sl] = z * lax.rsqrt(jnp.sum(z * z, axis=-1, keepdims=True) + NORM_EPS) * scale
    qkv_ref[:, 2 * hw:] = y[:, 2 * hw:]
    bg = _dn_gates(small_ref[...], gcst_ref)
    lane = lax.broadcasted_iota(jnp.int32, bg.shape, 1)
    run = jnp.where(lane < 2 * heads, _cum_dot(cumf_ref[...], bg), _cum_dot(cumb_ref[...], bg))
    gx = jnp.where(gcst_ref[3:4, :] > 0.5, run, bg)
    gx_ref[...] = gx
    blocks = _block_mask(heads * CHUNK, hw, CHUNK, dh).astype(MXU_DTYPE)
    for d, (rev, a_ref) in enumerate(((False, af_ref), (True, ab_ref))):
        strict = _tri_mask_heads(rev, True, heads)
        forms = _dn_forms(gx, d, heads)
        for c in range(tl // CHUNK):
            k = qkv_ref[c * CHUNK:(c + 1) * CHUNK, hw:2 * hw]
            kk = _mm_nt(k, _head_expand(k, blocks, heads))
            a_ref[c] = jnp.where(strict, kk * _head_cols(forms[2], c, heads) * _dn_gamma(rev, forms, c, heads), 0.0)


def _dn_prep(proj, small, cw, gate_consts, heads, dh):
    nb, seq_len, pw = proj.shape
    tl = TL_MIX
    nblk = seq_len // tl
    hw = heads * dh
    blk_spec = lambda wd: pl.BlockSpec((None, tl, wd), lambda b, i: (b, i, 0))
    prev, nxt = _halo_specs(False, tl, 3 * hw, seq_len)
    a_spec = pl.BlockSpec((None, tl // CHUNK, CHUNK, heads * CHUNK), lambda b, i: (b, i, 0, 0))
    a_shape = jax.ShapeDtypeStruct((nb, seq_len // CHUNK, CHUNK, heads * CHUNK), F32)
    lanes_shape = jax.ShapeDtypeStruct((nb, seq_len, LANES), F32)
    cum_f, cum_b = _block_cum(False, tl), _block_cum(True, tl)
    return pl.pallas_call(
        functools.partial(_dn_prep_kernel, tl, heads, dh),
        grid=(nb, nblk),
        in_specs=[blk_spec(pw), prev, nxt, blk_spec(small.shape[2]), _const_spec(cw.shape),
                  _const_spec(gate_consts.shape), _const_spec(cum_f.shape), _const_spec(cum_b.shape)],
        out_specs=[blk_spec(3 * hw), blk_spec(LANES), a_spec, a_spec],
        out_shape=[jax.ShapeDtypeStruct((nb, seq_len, 3 * hw), F32), lanes_shape, a_shape, a_shape],
        scratch_shapes=[pltpu.VMEM((tl + 2 * CONV_HALO, 3 * hw), F32)],
        compiler_params=_params("parallel", "parallel"),
        name="dn_prep",
    )(proj, proj, proj, small, cw, gate_consts, cum_f, cum_b)


def _dn_solve_kernel(upper, heads, a_ref, t_ref, a_scr, t_scr):
    nl = a_ref.shape[0]
    hc = heads * CHUNK
    half = CHUNK // 2
    for t in range(CHUNK):
        a_scr[t] = a_ref[:, t * hc:(t + 1) * hc].T.reshape(heads, CHUNK, nl)
    col = lax.broadcasted_iota(jnp.int32, (half, nl), 0)
    t_scr[...] = jnp.zeros_like(t_scr)

    def row(idx, carry):
        t = (CHUNK - 1 - idx) if upper else idx

        def term_lo(s, accs):
            return tuple(acc - a_scr[t, h, pl.ds(s, 1), :] * t_scr[s, h, 0:half, :] for h, acc in enumerate(accs))

        def term_hi(s, accs):
            return tuple(acc - a_scr[t, h, pl.ds(s, 1), :] * t_scr[s, h, half:, :] for h, acc in enumerate(accs))

        def pairs(term, count, first, step):
            def body(i, accs):
                s = first + 2 * step * i
                return term(s + step, term(s, accs))
            return functools.partial(lax.fori_loop, 0, (count + 1) // 2, body)

        if upper:
            lo_loop = pairs(term_lo, jnp.maximum(half - 1 - t, 0), half - 1, -1)
            hi_loop = pairs(term_hi, CHUNK - 1 - t, CHUNK - 1, -1)
        else:
            lo_loop = pairs(term_lo, t, 0, 1)
            hi_loop = pairs(term_hi, jnp.maximum(t - half, 0), half, 1)
        lo = lo_loop((jnp.where(col == t, 1.0, 0.0),) * heads)
        hi = hi_loop((jnp.where(col + half == t, 1.0, 0.0),) * heads)
        for h in range(heads):
            t_scr[t, h, 0:half, :] = lo[h]
            t_scr[t, h, half:, :] = hi[h]
        return carry

    lax.fori_loop(0, CHUNK, row, 0)
    for t in range(CHUNK):
        t_ref[:, t * hc:(t + 1) * hc] = t_scr[t].reshape(hc, nl).T.astype(t_ref.dtype)


def _dn_solve(a, upper, heads):
    nb, nc = a.shape[:2]
    flat = a.reshape(nb * nc, CHUNK * heads * CHUNK)
    nl = min(SOLVE_CHUNKS, nb * nc)
    spec = pl.BlockSpec((nl, flat.shape[1]), lambda i: (i, 0))
    scr = pltpu.VMEM((CHUNK, heads, CHUNK, nl), F32)
    return pl.pallas_call(
        functools.partial(_dn_solve_kernel, upper, heads),
        grid=(nb * nc // nl,),
        in_specs=[spec],
        out_specs=spec,
        out_shape=jax.ShapeDtypeStruct(flat.shape, MXU_DTYPE),
        scratch_shapes=[scr, scr],
        compiler_params=_params("parallel"),
        name="dn_solve_bwd" if upper else "dn_solve_fwd",
    )(flat).reshape(a.shape)


def _dn_main_kernel(tl, heads, dh, qf_ref, gxf_ref, tf_ref, qb_ref, gxb_ref, tb_ref, of_ref, ob_ref, s_ref):
    i = pl.program_id(1)
    hw = heads * dh
    nchunk = tl // CHUNK
    npair = heads // 2
    pw = 2 * dh

    @pl.when(i == 0)
    def _():
        s_ref[...] = jnp.zeros_like(s_ref)

    blocks = _block_mask(heads * CHUNK, hw, CHUNK, dh).astype(MXU_DTYPE)
    t_blocks = _block_mask(heads * CHUNK, heads * CHUNK, CHUNK, CHUNK).astype(MXU_DTYPE)
    pair_blocks = _block_mask(pw, pw, dh, dh).astype(F32)
    stack = lambda x: jnp.concatenate([x[:, h * dh:(h + 1) * dh] for h in range(heads)], axis=0)
    dirs = ((False, qf_ref, gxf_ref, tf_ref, of_ref), (True, qb_ref, gxb_ref, tb_ref, ob_ref))
    for d, (rev, qkv_ref, gx_ref, t_ref, o_ref) in enumerate(dirs):
        incl = _tri_mask_heads(rev, False, heads)
        end = 0 if rev else CHUNK - 1
        forms = _dn_forms(gx_ref[...], d, heads)
        b_nat = jnp.concatenate(forms[2], axis=1)
        g_nat = jnp.concatenate(forms[3], axis=1)
        eg_nat = jnp.exp(g_nat)
        states = [s_ref[d, j] for j in range(npair)]
        for c in _chunk_order(rev, nchunk):
            rows = slice(c * CHUNK, (c + 1) * CHUNK)
            q = qkv_ref[rows, 0:hw]
            k = qkv_ref[rows, hw:2 * hw]
            v = qkv_ref[rows, 2 * hw:3 * hw]
            eg = eg_nat[rows, :]
            beta = b_nat[rows, :]
            g_end = g_nat[c * CHUNK + end:c * CHUNK + end + 1, :]
            k_end = k * jnp.exp(g_end - g_nat[rows, :])
            e_end = jnp.exp(g_end)
            t_bd = _head_expand(t_ref[c], t_blocks, heads)
            wu = _mm(t_bd, jnp.concatenate([stack(k * beta * eg), stack(v * beta)], axis=1))
            attn = jnp.where(incl, _mm_nt(q, _head_expand(k, blocks, heads)) * _dn_gamma(rev, forms, c, heads), 0.0)
            w = jnp.concatenate([wu[h * CHUNK:(h + 1) * CHUNK, 0:dh] for h in range(heads)], axis=1)
            u = jnp.concatenate([wu[h * CHUNK:(h + 1) * CHUNK, dh:] for h in range(heads)], axis=1)
            lhs = jnp.concatenate([w, q * eg], axis=0)
            xs = [_mm(lhs[:, j * pw:(j + 1) * pw], states[j]) for j in range(npair)]
            v_new = u - jnp.concatenate([x[0:CHUNK, :] for x in xs], axis=1)
            o_ref[rows, :] = (jnp.concatenate([x[CHUNK:, :] for x in xs], axis=1)
                              + _mm(attn, _head_expand(v_new, blocks, heads)))
            for j in range(npair):
                ps = slice(j * pw, (j + 1) * pw)
                states[j] = states[j] * e_end[:, ps] + _mm_tn(k_end[:, ps], v_new[:, ps]) * pair_blocks
        for j in range(npair):
            s_ref[d, j] = states[j]


def _dn_main(qkv, gx, t_f, t_b, heads, dh):
    nb, seq_len, _ = qkv.shape
    tl = TL_MIX
    nblk = seq_len // tl
    hw = heads * dh
    fwd = lambda wd: pl.BlockSpec((None, tl, wd), lambda b, i: (b, i, 0))
    bwd = lambda wd: pl.BlockSpec((None, tl, wd), lambda b, i: (b, nblk - 1 - i, 0))
    t_blk = (None, tl // CHUNK, CHUNK, heads * CHUNK)
    t_fwd = pl.BlockSpec(t_blk, lambda b, i: (b, i, 0, 0))
    t_bwd = pl.BlockSpec(t_blk, lambda b, i: (b, nblk - 1 - i, 0, 0))
    out = jax.ShapeDtypeStruct((nb, seq_len, hw), F32)
    return pl.pallas_call(
        functools.partial(_dn_main_kernel, tl, heads, dh),
        grid=(nb, nblk),
        in_specs=[fwd(3 * hw), fwd(LANES), t_fwd, bwd(3 * hw), bwd(LANES), t_bwd],
        out_specs=[fwd(hw), bwd(hw)],
        out_shape=[out, out],
        scratch_shapes=[pltpu.VMEM((2, heads // 2, 2 * dh, 2 * dh), F32)],
        compiler_params=_params("parallel", "arbitrary"),
        name="dn_main",
    )(qkv, gx, t_f, qkv, gx, t_b)


def _dn_branch(proj, small, cw, gate_consts, heads, dh):
    qkv, gx, a_f, a_b = _dn_prep(proj, small, cw, gate_consts, heads, dh)
    return _dn_main(qkv, gx, _dn_solve(a_f, False, heads), _dn_solve(a_b, True, heads), heads, dh)


S5_CHUNK = 16


def _cpow(lam_re, lam_im, dt, e):
    mag = jnp.exp(lam_re * dt * e)
    ang = lam_im * dt * e
    return mag * jnp.cos(ang), mag * jnp.sin(ang)


def _s5_wt_kernel(gw, lre_ref, lim_ref, ldt_ref, bre_ref, bim_ref, ctre_ref, ctim_ref, cre_ref, cim_ref,
                  k_ref, zwre_ref, zwim_ref, cwre_ref, cwimn_ref):
    backward = pl.program_id(0) == 1
    lam_re, lam_im, dt = lre_ref[...], lim_ref[...], jnp.exp(ldt_ref[...])
    a_re, a_im = _cpow(lam_re, lam_im, dt, 1.0)
    den = lam_re * lam_re + lam_im * lam_im
    n_re = a_re - 1.0
    f_re = (n_re * lam_re + a_im * lam_im) / den
    f_im = (a_im * lam_re - n_re * lam_im) / den
    bb_re = f_re * bre_ref[...] - f_im * bim_ref[...]
    bb_im = f_re * bim_ref[...] + f_im * bre_ref[...]
    pos = jnp.right_shift(lax.broadcasted_iota(jnp.int32, lam_re.shape, 1), _log2(gw)).astype(F32)
    p_re, p_im = _cpow(lam_re, lam_im, dt, pos)
    e_re = p_re * bb_re - p_im * bb_im
    e_im = p_re * bb_im + p_im * bb_re
    k_ref[...] = (jnp.dot(cre_ref[...], e_re, precision=HI, preferred_element_type=F32)
                  - jnp.dot(cim_ref[...], e_im, precision=HI, preferred_element_type=F32))
    z_re, z_im = _cpow(lam_re, lam_im, dt, jnp.where(backward, pos, (S5_CHUNK - 1.0) - pos))
    zwre_ref[...] = z_re * bb_re - z_im * bb_im
    zwim_ref[...] = z_re * bb_im + z_im * bb_re
    c_re, c_im = _cpow(lam_re, lam_im, dt, jnp.where(backward, S5_CHUNK - pos, pos + 1.0))
    cwre_ref[...] = ctre_ref[...] * c_re - ctim_ref[...] * c_im
    cwimn_ref[...] = -(ctre_ref[...] * c_im + ctim_ref[...] * c_re)


def _s5_weights(lam_re, lam_im, log_dt, b_re, b_im, c_re, c_im):
    _, groups, states, gw = b_re.shape
    wl = S5_CHUNK * gw
    rep = lambda a: jnp.broadcast_to(a[..., None], (2, groups, states, wl))
    tile = lambda a: jnp.tile(a, (1, 1, 1, S5_CHUNK))
    blk = lambda r, c: pl.BlockSpec((None, None, r, c), lambda d, g: (d, g, 0, 0))
    ct = lambda a: tile(a.transpose(0, 1, 3, 2))
    big = jax.ShapeDtypeStruct((2, groups, states, wl), F32)
    return pl.pallas_call(
        functools.partial(_s5_wt_kernel, gw),
        grid=(2, groups),
        in_specs=[blk(states, wl)] * 7 + [blk(gw, states)] * 2,
        out_specs=[blk(gw, wl)] + [blk(states, wl)] * 4,
        out_shape=[jax.ShapeDtypeStruct((2, groups, gw, wl), F32), big, big, big, big],
        compiler_params=_params("parallel", "parallel"),
        name="s5_wt",
    )(rep(lam_re), rep(lam_im), rep(jnp.broadcast_to(log_dt[..., None], lam_re.shape)), tile(b_re), tile(b_im),
      ct(c_re), ct(c_im), c_re, c_im)


def _s5_pow_kernel(seg, lre_ref, lim_ref, ldt_ref, pw_ref):
    a_re, a_im = _cpow(lre_ref[...], lim_ref[...], jnp.exp(ldt_ref[...]), float(S5_CHUNK))
    x_re, x_im = a_re, a_im
    for k in range(seg):
        pw_ref[0, k] = x_re
        pw_ref[1, k] = x_im
        x_re, x_im = x_re * a_re - x_im * a_im, x_re * a_im + x_im * a_re


def _s5_pow(lre, lim, ldt, seg):
    return pl.pallas_call(
        functools.partial(_s5_pow_kernel, seg),
        out_shape=jax.ShapeDtypeStruct((2, seg) + lre.shape, F32),
        name="s5_pow",
    )(lre, lim, ldt)


def _gelu_tanh(x):
    return 0.5 * x * (1.0 + jnp.tanh(math.sqrt(2.0 / math.pi) * (x + 0.044715 * (x * x * x))))


def _s5_mix_kernel(nr, u_ref, tf_ref, tb_ref, zw_ref, cw_ref, pw_ref, y_ref, pad_re, pad_im, f_re, f_im, b_re, b_im):
    seg = nr // SUBLANES
    pitch = seg + SEG_PAD
    half = LANES // 2
    fwd_lanes = lax.broadcasted_iota(jnp.int32, (1, LANES), 1) < half
    w1 = jnp.concatenate([(tf_ref[...] + tb_ref[...]).astype(MXU_DTYPE), zw_ref[...]], axis=1)
    yz = jnp.dot(u_ref[...], w1, preferred_element_type=F32)
    wl = tf_ref.shape[1]

    for pad, col in ((pad_re, wl), (pad_im, wl + LANES)):
        pad[0:SUBLANES, :] = jnp.zeros((SUBLANES, LANES), F32)
        pad[SUBLANES:SUBLANES + nr, :] = yz[:, col:col + LANES]
        pad[SUBLANES + nr:, :] = jnp.zeros((SUBLANES, LANES), F32)
    for s in range(SUBLANES):
        dst = slice(s * pitch, s * pitch + seg)
        for pad, fw, bw in ((pad_re, f_re, b_re), (pad_im, f_im, b_im)):
            fw[dst, :] = pad[pl.ds(SUBLANES - 1 + s * seg, seg), :]
            bw[dst, :] = pad[pl.ds(SUBLANES + 1 + s * seg, seg), :]

    a_re = jnp.broadcast_to(pw_ref[0, 0:1, :], (SUBLANES, LANES))
    a_im = jnp.broadcast_to(pw_ref[1, 0:1, :], (SUBLANES, LANES))
    for re_scr, im_scr, order in ((f_re, f_im, range(seg)), (b_re, b_im, range(seg - 1, -1, -1))):
        sr = jnp.zeros((SUBLANES, LANES), F32)
        si = jnp.zeros((SUBLANES, LANES), F32)
        for k in order:
            sr, si = (a_re * sr - a_im * si + re_scr[pl.ds(k, SUBLANES, stride=pitch), :],
                      a_re * si + a_im * sr + im_scr[pl.ds(k, SUBLANES, stride=pitch), :])
            re_scr[pl.ds(k, SUBLANES, stride=pitch), :] = sr
            im_scr[pl.ds(k, SUBLANES, stride=pitch), :] = si

    pr, pi = pw_ref[2], pw_ref[3]
    sg_re = pw_ref[0, seg - 1:seg, :]
    sg_im = pw_ref[1, seg - 1:seg, :]
    zero = jnp.zeros((1, LANES), F32)
    cf, cb = [(zero, zero)], [(zero, zero)]
    for s in range(SUBLANES - 1):
        er, ei = f_re[pl.ds(s * pitch + seg - 1, 1), :], f_im[pl.ds(s * pitch + seg - 1, 1), :]
        cr, ci = cf[-1]
        cf.append((sg_re * cr - sg_im * ci + er, sg_re * ci + sg_im * cr + ei))
        t = SUBLANES - 1 - s
        er, ei = b_re[pl.ds(t * pitch, 1), :], b_im[pl.ds(t * pitch, 1), :]
        cr, ci = cb[-1]
        cb.append((sg_re * cr - sg_im * ci + er, sg_re * ci + sg_im * cr + ei))
    cb = cb[::-1]
    parts_re, parts_im = [], []
    for s in range(SUBLANES):
        src = slice(s * pitch, s * pitch + seg)
        cr = jnp.where(fwd_lanes, cf[s][0], cb[s][0])
        ci = jnp.where(fwd_lanes, cf[s][1], cb[s][1])
        parts_re.append(jnp.where(fwd_lanes, f_re[src, :], b_re[src, :]) + pr * cr - pi * ci)
        parts_im.append(jnp.where(fwd_lanes, f_im[src, :], b_im[src, :]) + pr * ci + pi * cr)
    states = jnp.concatenate([jnp.concatenate(parts_re, axis=0), jnp.concatenate(parts_im, axis=0)], axis=1)
    y_ref[...] = yz[:, 0:wl] + _mm(states, cw_ref[...])


def _s5_mix(u_g, toep_f, toep_b, zw, cw, pw):
    groups, nb, nr, wl = u_g.shape
    seg = nr // SUBLANES
    pitch = seg + SEG_PAD
    per_g = lambda r, c: pl.BlockSpec((None, r, c), lambda g, b: (g, 0, 0))
    rows = pl.BlockSpec((None, None, nr, wl), lambda g, b: (g, b, 0, 0))
    scan = pltpu.VMEM((SUBLANES * pitch, LANES), F32)
    pad = pltpu.VMEM((nr + 2 * SUBLANES, LANES), F32)
    return pl.pallas_call(
        functools.partial(_s5_mix_kernel, nr),
        grid=(groups, nb),
        in_specs=[rows, per_g(wl, wl), per_g(wl, wl), per_g(wl, 2 * LANES), per_g(2 * LANES, wl),
                  pl.BlockSpec((None, 4, seg, LANES), lambda g, b: (g, 0, 0, 0))],
        out_specs=rows,
        out_shape=jax.ShapeDtypeStruct((groups, nb, nr, wl), F32),
        scratch_shapes=[pad, pad, scan, scan, scan, scan],
        compiler_params=_params("parallel", "parallel"),
        name="s5_mix",
    )(u_g, toep_f, toep_b, zw, cw, pw)


def _rows_to_groups(x, gw, o_ref, xs, acc):
    per = LANES // gw
    nrow = x.shape[0] // S5_CHUNK
    nslab = x.shape[1] // LANES
    for v in range(nslab):
        xs[v] = x[:, v * LANES:(v + 1) * LANES]
    for r in range(S5_CHUNK):
        dst = (r // per) * LANES + (r % per) * gw
        for v in range(nslab):
            rows = xs[v, pl.ds(r, nrow, stride=S5_CHUNK), :]
            for s in range(per):
                rot = rows if s == 0 else pltpu.roll(rows, s * gw, axis=1)
                g = per * v + (r - s) % per
                acc[g, :, dst:dst + gw] = rot[:, (r % per) * gw:(r % per + 1) * gw]
    o_ref[...] = acc[...].astype(o_ref.dtype)


def _groups_to_rows(y_ref, gw, tmp, ynat):
    per = LANES // gw
    tl, width = ynat.shape
    nrow = tl // S5_CHUNK
    nslab = width // LANES
    for g in range(y_ref.shape[0]):
        v, slot = g // per, g % per
        for h in range(S5_CHUNK // per):
            rows = y_ref[g, :, h * LANES:(h + 1) * LANES]
            for s in range(per):
                rot = rows if s == 0 else pltpu.roll(rows, s * gw, axis=1)
                r = h * per + (slot - s) % per
                tmp[v, r * nrow:(r + 1) * nrow, slot * gw:(slot + 1) * gw] = rot[:, slot * gw:(slot + 1) * gw]
    for n in range(nrow):
        for r0 in range(0, S5_CHUNK, SUBLANES):
            for v in range(nslab):
                ynat[n * S5_CHUNK + r0:n * S5_CHUNK + r0 + SUBLANES, v * LANES:(v + 1) * LANES] = (
                    tmp[v, pl.ds(r0 * nrow + n, SUBLANES, stride=nrow), :])
    return ynat[...]


def _s5_finish(y, main, dsk, wglu, bglu):
    width = y.shape[1]
    z = _gelu_tanh(y + dsk * main[:, 0:width])
    z = z * _sigmoid(_mm(z, wglu) + bglu)
    return z * _silu(main[:, width:])


def _s5_toeplitz_branch(u_g, lam_re, lam_im, log_dt, b_re, b_im, c_re, c_im):
    _, groups, states, gw = b_re.shape
    nr = u_g.shape[2]
    seg = nr // SUBLANES
    wl = S5_CHUNK * gw
    kk, zw_re, zw_im, cw_re, cw_imn = _s5_weights(lam_re, lam_im, log_dt, b_re, b_im, c_re, c_im)

    lag = jnp.arange(S5_CHUNK)[None, :] - jnp.arange(S5_CHUNK)[:, None]
    kt = kk.reshape(2, groups, gw, S5_CHUNK, gw).transpose(0, 1, 3, 4, 2)

    def toeplitz(d, lag_d):
        t = jnp.where((lag_d >= 0)[None, :, :, None, None], kt[d][:, jnp.maximum(lag_d, 0)], 0.0)
        return t.transpose(0, 1, 3, 2, 4).reshape(groups, wl, wl)

    toep_f, toep_b = toeplitz(0, lag), toeplitz(1, -lag)
    by_rows = lambda a: a.transpose(0, 2, 1)
    zw = jnp.concatenate([by_rows(zw_re[0]), by_rows(zw_re[1]), by_rows(zw_im[0]), by_rows(zw_im[1])],
                         axis=2).astype(MXU_DTYPE)
    cw = jnp.concatenate([cw_re[0], cw_re[1], cw_imn[0], cw_imn[1]], axis=1).astype(MXU_DTYPE)

    flat = lambda a: a.reshape(2, groups * states // LANES, LANES)
    pw = _s5_pow(flat(lam_re), flat(lam_im), flat(jnp.broadcast_to(log_dt[..., None], lam_re.shape)), seg)
    pw = pw.reshape(2, seg, 2, groups, states).transpose(2, 3, 0, 1, 4)
    plain = jnp.concatenate([pw[0], pw[1]], axis=3)
    by_row = jnp.concatenate([pw[0], jnp.flip(pw[1], axis=2)], axis=3)
    pw_all = jnp.concatenate([plain, by_row], axis=1)

    return _s5_mix(u_g, toep_f, toep_b, zw, cw, pw_all)


def _layer(x_parts, group_rows, seq_len, p, final_g, final):
    d = x_parts[0].shape[1]
    nb = sum(a.shape[0] for a in x_parts) // seq_len
    bw = d // 2
    bf = lambda a: a.astype(MXU_DTYPE)
    row = lambda a: a.reshape(1, -1)

    lru_blocks = p['lru_w_a'].shape[1]
    gla_heads = GLA_HEADS
    gla_dk = p['gla_w_up'].shape[2] // gla_heads
    gla_dv = bw // gla_heads
    gla_rank = p['gla_w_up'].shape[1]
    dn_heads = p['dn_a_log'].shape[1]
    dn_dh = bw // dn_heads
    widths = (bw, bw, gla_heads * gla_dk, gla_heads * gla_dk, bw, bw, 2 * gla_rank, 3 * bw, bw, 4 * dn_heads, bw, bw)
    offs = [0]
    for wd in widths:
        offs.append(offs[-1] + wd)
    w_in = p['w_in']
    cols = lambda a, b: w_in[:, offs[a]:offs[b]]
    n_gate = 4 * dn_heads
    n_small = n_gate + 2 * gla_rank
    w_small = jnp.concatenate([cols(9, 10), cols(6, 7), jnp.zeros((d, LANES - n_small), w_in.dtype)], axis=1)
    w_cat = bf(jnp.concatenate([cols(0, 2), cols(2, 6), cols(7, 9), cols(10, 12), w_small], axis=1))
    out_w = (2 * bw, offs[6] - offs[2], 4 * bw, 2 * bw, LANES)
    s5_groups, s5_gw = p['s5_b_re'].shape[1], p['s5_b_re'].shape[3]
    lru_p, gla_p, dn_p, s5_p, small, s5_ug = _in_proj(x_parts, row(p['norm_g']), w_cat, out_w, seq_len, 3,
                                                      s5_groups, s5_gw)
    seq = lambda a: a.reshape(nb, seq_len, a.shape[1])
    flat = lambda a: a.reshape(nb * seq_len, a.shape[2])
    lru_s, gla_s, dn_s, small_s = seq(lru_p), seq(gla_p), seq(dn_p), seq(small)

    lanes_per_blk = bw // lru_blocks
    y_lru = None
    for dct, rev in enumerate((False, True)):
        wg = bf(jnp.concatenate([p['lru_w_a'][dct], p['lru_w_x'][dct]], axis=2))
        bg = jnp.concatenate([p['lru_b_a'][dct].reshape(lru_blocks, 1, lanes_per_blk),
                              p['lru_b_x'][dct].reshape(lru_blocks, 1, lanes_per_blk)], axis=2)
        y_lru = _lru_dir(rev, lru_s, y_lru, p['lru_conv_w'], row(p['lru_conv_b']), wg, bg, row(p['lru_lambda'][dct]))

    wup = jnp.zeros((2, LANES, gla_heads * gla_dk), F32)
    for dct in range(2):
        lo = n_gate + dct * gla_rank
        wup = wup.at[dct, lo:lo + gla_rank].set(p['gla_w_up'][dct])
    gla_f, gla_b = _gla_both(gla_s, small_s, bf(wup), p['gla_b_up'].reshape(2, 1, -1), gla_heads, gla_dk, gla_dv)

    alpha_lanes = (jnp.arange(2)[:, None] * 2 * dn_heads + dn_heads + jnp.arange(dn_heads)[None, :]).reshape(-1)
    beta_lanes = alpha_lanes - dn_heads
    gate_consts = jnp.zeros((SUBLANES, LANES), F32)
    gate_consts = gate_consts.at[0, alpha_lanes].set(p['dn_a_log'].reshape(-1))
    gate_consts = gate_consts.at[1, alpha_lanes].set(p['dn_dt_bias'].reshape(-1))
    gate_consts = gate_consts.at[2, beta_lanes].set(1.0).at[3, alpha_lanes].set(1.0)
    dn_f, dn_b = _dn_branch(dn_s, small_s, p['dn_conv_w'], gate_consts, dn_heads, dn_dh)

    s5_yg = _s5_toeplitz_branch(s5_ug, p['s5_lambda_re'], p['s5_lambda_im'], p['s5_log_dt'], p['s5_b_re'],
                                p['s5_b_im'], p['s5_c_re'], p['s5_c_im'])

    gla_gate_col = (offs[5] - offs[2]) // bw
    dn_gate_col = (offs[8] - offs[7]) // bw
    rest = (seq_len, flat(y_lru),
            (flat(gla_f), flat(gla_b), gla_p, gla_gate_col, row(p['gla_norm_g']), gla_heads),
            (flat(dn_f), flat(dn_b), dn_p, dn_gate_col, row(p['dn_norm_g']), dn_heads),
            (s5_yg, s5_p, row(p['s5_d']), bf(p['s5_w_glu']), row(p['s5_b_glu']), s5_gw), row(p['norm_g']),
            bf(p['w_merge_gate']), p['b_merge_gate'].reshape(4, 1, d), bf(p['w_branch']), bf(p['w_out']),
            row(final_g), final)
    if not final:
        return [_out_proj([_whole(a) for a in x_parts], 0, *rest)]
    outs, row0 = [], 0
    for gi, rows in enumerate(group_rows):
        x_range = (x_parts[0], row0, rows) if len(x_parts) == 1 else _whole(x_parts[gi])
        outs.append(_out_proj([x_range], row0, *rest))
        row0 += rows
    return outs


_LAYER_PARAM_NAMES = (
    'norm_g', 'w_in', 'lru_conv_w', 'lru_conv_b', 'lru_w_a', 'lru_b_a', 'lru_w_x', 'lru_b_x', 'lru_lambda',
    'gla_w_up', 'gla_b_up', 'gla_norm_g', 'dn_conv_w', 'dn_a_log', 'dn_dt_bias', 'dn_norm_g',
    's5_lambda_re', 's5_lambda_im', 's5_log_dt', 's5_b_re', 's5_b_im', 's5_c_re', 's5_c_im', 's5_d',
    's5_w_glu', 's5_b_glu', 'w_branch', 'w_merge_gate', 'b_merge_gate', 'w_out')


def kernel(x_prompt, x_sample, norm_g, w_in, lru_conv_w, lru_conv_b, lru_w_a, lru_b_a, lru_w_x, lru_b_x, lru_lambda, gla_w_up, gla_b_up, gla_norm_g, dn_conv_w, dn_a_log, dn_dt_bias, dn_norm_g, s5_lambda_re, s5_lambda_im, s5_log_dt, s5_b_re, s5_b_im, s5_c_re, s5_c_im, s5_d, s5_w_glu, s5_b_glu, w_branch, w_merge_gate, b_merge_gate, w_out, final_norm_g):
    stacked = (norm_g, w_in, lru_conv_w, lru_conv_b, lru_w_a, lru_b_a, lru_w_x, lru_b_x, lru_lambda,
               gla_w_up, gla_b_up, gla_norm_g, dn_conv_w, dn_a_log, dn_dt_bias, dn_norm_g,
               s5_lambda_re, s5_lambda_im, s5_log_dt, s5_b_re, s5_b_im, s5_c_re, s5_c_im, s5_d,
               s5_w_glu, s5_b_glu, w_branch, w_merge_gate, b_merge_gate, w_out)
    depth = norm_g.shape[0]
    seq_len, d = x_prompt.shape[1:]
    assert x_sample.shape[1] == seq_len
    groups = (x_prompt, x_sample)
    parts = [a.reshape(-1, d) for a in groups]
    group_rows = [a.shape[0] for a in parts]
    for layer in range(depth):
        p = {name: arr[layer] for name, arr in zip(_LAYER_PARAM_NAMES, stacked)}
        parts = _layer(parts, group_rows, seq_len, p, final_norm_g, layer == depth - 1)
    return tuple(y.reshape(a.shape) for y, a in zip(parts, groups))
```

```python
import functools
import math

import jax
import jax.numpy as jnp
from jax import lax
from jax.experimental import pallas as pl
from jax.experimental.pallas import tpu as pltpu

F32 = jnp.float32
MXU_DTYPE = jnp.bfloat16
HI = lax.Precision.HIGHEST

NORM_EPS = 1e-6
CONV_W = 4
CONV_HALO = 8
LRU_C = 8.0
GLA_TAU = 16.0
GLA_HEADS = 4
CHUNK = 64
LANES = 128
SUBLANES = 8
SEG_PAD = 4
VMEM_LIMIT = 56 * 1024 * 1024

TM_PROJ = 256
TL_MIX = 256
SOLVE_CHUNKS = 128


def _mm(a, b):
    return jnp.dot(a.astype(MXU_DTYPE), b.astype(MXU_DTYPE), preferred_element_type=F32)


def _mm_nt(a, b):
    return lax.dot_general(a.astype(MXU_DTYPE), b.astype(MXU_DTYPE), (((1,), (1,)), ((), ())),
                           preferred_element_type=F32)


def _mm_tn(a, b):
    return lax.dot_general(a.astype(MXU_DTYPE), b.astype(MXU_DTYPE), (((0,), (0,)), ((), ())),
                           preferred_element_type=F32)


def _cum_dot(cum3, x):
    hi = x.astype(jnp.bfloat16)
    r1 = x - hi.astype(F32)
    mid = r1.astype(jnp.bfloat16)
    lo = (r1 - mid.astype(F32)).astype(jnp.bfloat16)
    return jnp.dot(cum3, jnp.concatenate([hi, mid, lo], axis=0), preferred_element_type=F32)


def _sigmoid(x):
    return 0.5 * jnp.tanh(0.5 * x) + 0.5


def _silu(x):
    return x * _sigmoid(x)


def _softplus(x):
    return jnp.maximum(x, 0.0) + jnp.log1p(jnp.exp(-jnp.abs(x)))


def _rms(x, g):
    return x * lax.rsqrt(jnp.mean(x * x, axis=-1, keepdims=True) + NORM_EPS) * g


def _params(*sem):
    return pltpu.CompilerParams(dimension_semantics=sem, vmem_limit_bytes=VMEM_LIMIT)


def _const_spec(shape):
    nd = len(shape)
    return pl.BlockSpec(shape, lambda *_: (0,) * nd)


def _seq_pos(rev):
    if rev:
        return lambda i, n: n - 1 - i
    return lambda i, n: i


def _log2(n):
    assert n > 0 and n & (n - 1) == 0, n
    return n.bit_length() - 1


def _block_mask(nrow, ncol, row_blk, col_blk):
    r = jnp.right_shift(lax.broadcasted_iota(jnp.int32, (nrow, ncol), 0), _log2(row_blk))
    c = jnp.right_shift(lax.broadcasted_iota(jnp.int32, (nrow, ncol), 1), _log2(col_blk))
    return r == c


def _tri_mask_heads(rev, strict, heads):
    t = lax.broadcasted_iota(jnp.int32, (CHUNK, heads * CHUNK), 0)
    s = jnp.bitwise_and(lax.broadcasted_iota(jnp.int32, (CHUNK, heads * CHUNK), 1), CHUNK - 1)
    if rev:
        return (s > t) if strict else (s >= t)
    return (s < t) if strict else (s <= t)


def _block_cum(rev, tl):
    t = jnp.arange(tl)[:, None]
    s = jnp.arange(tl)[None, :]
    same = (t // CHUNK) == (s // CHUNK)
    cum = (same & ((s >= t) if rev else (s <= t))).astype(jnp.bfloat16)
    return jnp.concatenate([cum, cum, cum], axis=1)


def _chunk_order(rev, nchunk):
    return range(nchunk - 1, -1, -1) if rev else range(nchunk)


def _row_parts_specs(parts, width):
    specs, starts, step0 = [], [], 0
    for _, row0, rows in parts:
        steps, blk0 = rows // TM_PROJ, row0 // TM_PROJ
        specs.append(pl.BlockSpec(
            (TM_PROJ, width), lambda i, s=step0, n=steps, b=blk0: (b + jnp.clip(i - s, 0, n - 1), 0)))
        starts.append(step0)
        step0 += steps
    return specs, starts


def _whole(arr):
    return (arr, 0, arr.shape[0])


def _row_parts_value(refs, starts):
    i = pl.program_id(0)
    x = refs[0][...]
    for ref, s in zip(refs[1:], starts[1:]):
        x = jnp.where(i >= s, ref[...], x)
    return x


def _in_proj_kernel(widths, starts, s5_index, s5_gw, *refs):
    nparts = len(starts)
    g_ref, w_ref = refs[nparts:nparts + 2]
    o_refs = refs[nparts + 2:nparts + 2 + len(widths)]
    ug_ref, xs, acc = refs[nparts + 2 + len(widths):]
    xn = _rms(_row_parts_value(refs[:nparts], starts), g_ref[...]).astype(MXU_DTYPE)
    offs = [sum(widths[:n]) for n in range(len(widths))]
    for n in [s5_index] + [n for n in range(len(widths)) if n != s5_index]:
        y = jnp.dot(xn, w_ref[:, offs[n]:offs[n] + widths[n]], preferred_element_type=F32)
        o_refs[n][...] = y
        if n == s5_index:
            _rows_to_groups(y[:, 0:widths[n] // 2], s5_gw, ug_ref, xs, acc)


def _in_proj(x_parts, g, w_cat, widths, seq_len, s5_index, s5_groups, s5_gw):
    d = x_parts[0].shape[1]
    t = sum(a.shape[0] for a in x_parts)
    n = w_cat.shape[1]
    x_specs, starts = _row_parts_specs([_whole(a) for a in x_parts], d)
    per_seq = seq_len // TM_PROJ
    nrow = TM_PROJ // S5_CHUNK
    wl = S5_CHUNK * s5_gw
    s5_w = widths[s5_index] // 2
    return pl.pallas_call(
        functools.partial(_in_proj_kernel, widths, tuple(starts), s5_index, s5_gw),
        grid=(t // TM_PROJ,),
        in_specs=x_specs + [_const_spec((1, d)), _const_spec((d, n))],
        out_specs=[pl.BlockSpec((TM_PROJ, wd), lambda i: (i, 0)) for wd in widths] + [
            pl.BlockSpec((s5_groups, None, nrow, wl), lambda i: (0, i // per_seq, i % per_seq, 0))],
        out_shape=[jax.ShapeDtypeStruct((t, wd), F32) for wd in widths] + [
            jax.ShapeDtypeStruct((s5_groups, t // seq_len, seq_len // S5_CHUNK, wl), MXU_DTYPE)],
        scratch_shapes=[pltpu.VMEM((s5_w // LANES, TM_PROJ, LANES), F32), pltpu.VMEM((s5_groups, nrow, wl), F32)],
        compiler_params=_params("parallel"),
        name="in_proj",
    )(*x_parts, g, w_cat)


def _out_proj_kernel(final, gla_heads, dn_heads, s5_gw, starts, *refs):
    nparts = len(starts)
    (ylru_ref, gof_ref, gob_ref, ggate_ref, dof_ref, dob_ref, dgate_ref, s5y_ref, s5p_ref, dsk_ref, wglu_ref,
     bglu_ref, g_ref, gng_ref, dng_ref, wmg_ref, bmg_ref, wbr_ref, wout_ref, fg_ref, o_ref, s5_tmp, s5_nat
     ) = refs[nparts:]

    def finish(of_ref, ob_ref, gate_ref, ng_ref, heads):
        o = of_ref[...] + ob_ref[...]
        dv = o.shape[1] // heads
        normed = [_rms(o[:, h * dv:(h + 1) * dv], ng_ref[...]) for h in range(heads)]
        return jnp.concatenate(normed, axis=1) * _silu(gate_ref[...])

    x = _row_parts_value(refs[:nparts], starts)
    xn = _rms(x, g_ref[...]).astype(MXU_DTYPE)
    branches = (
        lambda: ylru_ref[...],
        lambda: finish(gof_ref, gob_ref, ggate_ref, gng_ref, gla_heads),
        lambda: finish(dof_ref, dob_ref, dgate_ref, dng_ref, dn_heads),
        lambda: _s5_finish(_groups_to_rows(s5y_ref, s5_gw, s5_tmp, s5_nat), s5p_ref[...], dsk_ref[...],
                           wglu_ref[...], bglu_ref[...]))
    merged = None
    for n in (0, 1, 3, 2):
        y = branches[n]()
        gate = _sigmoid(jnp.dot(xn, wmg_ref[n], preferred_element_type=F32) + bmg_ref[n])
        term = gate * _mm(y, wbr_ref[n])
        merged = term if merged is None else merged + term
    out = x + _mm(merged, wout_ref[...])
    if final:
        out = _rms(out, fg_ref[...])
    o_ref[...] = out


def _out_proj(x_parts, row0, seq_len, y_lru, gla, dn, s5, g, wmg, bmg, wbr, wout, fg, final):
    d = x_parts[0][0].shape[1]
    t = sum(rows for _, _, rows in x_parts)
    bw = y_lru.shape[1]
    blk0 = row0 // TM_PROJ
    row = lambda wd, col=0: pl.BlockSpec((TM_PROJ, wd), lambda i: (blk0 + i, col))
    g_of, g_ob, g_proj, g_col, g_ng, g_heads = gla
    d_of, d_ob, d_proj, d_col, d_ng, d_heads = dn
    s5_y, s5_proj, dsk, wglu, bglu, s5_gw = s5
    per_seq = seq_len // TM_PROJ
    s5_spec = pl.BlockSpec((s5_y.shape[0], None, TM_PROJ // S5_CHUNK, s5_y.shape[3]),
                           lambda i: (0, (blk0 + i) // per_seq, (blk0 + i) % per_seq, 0))
    x_specs, starts = _row_parts_specs(x_parts, d)
    return pl.pallas_call(
        functools.partial(_out_proj_kernel, final, g_heads, d_heads, s5_gw, tuple(starts)),
        grid=(t // TM_PROJ,),
        in_specs=x_specs + [row(bw), row(bw), row(bw), row(bw, g_col), row(bw), row(bw), row(bw, d_col),
                            s5_spec, row(2 * bw), _const_spec(dsk.shape), _const_spec(wglu.shape),
                            _const_spec(bglu.shape),
                            _const_spec((1, d)), _const_spec(g_ng.shape), _const_spec(d_ng.shape),
                            _const_spec(wmg.shape), _const_spec(bmg.shape), _const_spec(wbr.shape),
                            _const_spec(wout.shape), _const_spec((1, d))],
        out_specs=pl.BlockSpec((TM_PROJ, d), lambda i: (i, 0)),
        out_shape=jax.ShapeDtypeStruct((t, d), F32),
        scratch_shapes=[pltpu.VMEM((bw // LANES, TM_PROJ, LANES), F32), pltpu.VMEM((TM_PROJ, bw), F32)],
        compiler_params=_params("parallel"),
        name="out_proj",
    )(*[a for a, _, _ in x_parts], y_lru, g_of, g_ob, g_proj, d_of, d_ob, d_proj, s5_y, s5_proj, dsk, wglu, bglu,
      g, g_ng, d_ng, wmg, bmg, wbr, wout, fg)


def _halo_specs(rev, tl, width, seq_len):
    pos = _seq_pos(rev)
    per = tl // CONV_HALO
    last = seq_len // CONV_HALO - 1
    nblk = seq_len // tl
    prev = pl.BlockSpec((None, CONV_HALO, width),
                        lambda b, i: (b, jnp.maximum(pos(i, nblk) * per - 1, 0), 0))
    nxt = pl.BlockSpec((None, CONV_HALO, width),
                       lambda b, i: (b, jnp.minimum((pos(i, nblk) + 1) * per, last), 0))
    return prev, nxt


def _conv_stage(ext_ref, u, prev, nxt, at_start, at_end):
    tl = u.shape[0]
    ext_ref[0:CONV_HALO, :] = jnp.where(at_start, 0.0, prev)
    ext_ref[CONV_HALO:CONV_HALO + tl, :] = u
    ext_ref[CONV_HALO + tl:, :] = jnp.where(at_end, 0.0, nxt)


def _centred_conv(ext_ref, w_ref, tl, lanes):
    left = CONV_W // 2
    acc = None
    for j in range(CONV_W):
        term = ext_ref[pl.ds(CONV_HALO + j - left, tl), lanes] * w_ref[j:j + 1, lanes]
        acc = term if acc is None else acc + term
    return acc


def _lru_kernel(rev, final, tl, *refs):
    if final:
        (main_ref, prev_ref, next_ref, hf_ref, cw_ref, cb_ref, wg_ref, bg_ref, lam_ref,
         o_ref, carry_ref, ext_ref, a_scr, d_scr) = refs
    else:
        (main_ref, prev_ref, next_ref, cw_ref, cb_ref, wg_ref, bg_ref, lam_ref,
         o_ref, carry_ref, ext_ref, a_scr, d_scr) = refs
    i = pl.program_id(1)
    nblk = pl.num_programs(1)
    blk = _seq_pos(rev)(i, nblk)
    width = o_ref.shape[1]
    nslab = width // LANES
    seg = tl // SUBLANES
    pitch = seg + SEG_PAD

    @pl.when(i == 0)
    def _():
        carry_ref[...] = jnp.zeros_like(carry_ref)

    _conv_stage(ext_ref, main_ref[:, 0:width], prev_ref[...], next_ref[...], blk == 0, blk == nblk - 1)
    c_all = -LRU_C * _softplus(-lam_ref[...])
    for h in range(nslab):
        sl = slice(h * LANES, (h + 1) * LANES)
        xh = _centred_conv(ext_ref, cw_ref, tl, sl) + cb_ref[:, sl]
        pre = _mm(xh, wg_ref[h]) + bg_ref[h]
        r = _sigmoid(pre[:, 0:LANES])
        ig = _sigmoid(pre[:, LANES:])
        a = jnp.exp(c_all[:, sl] * r)
        drive = jnp.sqrt(1.0 - a * a) * (ig * xh)
        for s in range(SUBLANES):
            a_scr[h, s * pitch:s * pitch + seg, :] = a[s * seg:(s + 1) * seg, :]
            d_scr[h, s * pitch:s * pitch + seg, :] = drive[s * seg:(s + 1) * seg, :]

    order = range(seg - 1, -1, -1) if rev else range(seg)
    for h in range(nslab):
        hloc = jnp.zeros((SUBLANES, LANES), F32)
        prod = jnp.ones((SUBLANES, LANES), F32)
        for k in order:
            ak = a_scr[h, pl.ds(k, SUBLANES, stride=pitch), :]
            dk = d_scr[h, pl.ds(k, SUBLANES, stride=pitch), :]
            hloc = ak * hloc + dk
            prod = prod * ak
            d_scr[h, pl.ds(k, SUBLANES, stride=pitch), :] = hloc
            a_scr[h, pl.ds(k, SUBLANES, stride=pitch), :] = prod

    last = 0 if rev else seg - 1
    seg_order = range(SUBLANES - 1, -1, -1) if rev else range(SUBLANES)
    for h in range(nslab):
        sl = slice(h * LANES, (h + 1) * LANES)
        c = carry_ref[0:1, sl]
        for s in seg_order:
            rows = slice(s * seg, (s + 1) * seg)
            hs = d_scr[h, s * pitch:s * pitch + seg, :] + a_scr[h, s * pitch:s * pitch + seg, :] * c
            if final:
                gate = main_ref[rows, width + h * LANES:width + (h + 1) * LANES]
                o_ref[rows, sl] = (hf_ref[rows, sl] + hs) * _silu(gate)
            else:
                o_ref[rows, sl] = hs
            c = a_scr[h, pl.ds(s * pitch + last, 1), :] * c + d_scr[h, pl.ds(s * pitch + last, 1), :]
        carry_ref[0:1, sl] = c


def _lru_dir(rev, proj, hf, cw, cb, wg, bg, lam):
    nb, seq_len, two_w = proj.shape
    width = two_w // 2
    tl = TL_MIX
    nblk = seq_len // tl
    pos = _seq_pos(rev)
    blk_spec = lambda wd: pl.BlockSpec((None, tl, wd), lambda b, i: (b, pos(i, nblk), 0))
    prev, nxt = _halo_specs(rev, tl, width, seq_len)
    final = hf is not None
    pitch = tl // SUBLANES + SEG_PAD
    ins = [proj, proj, proj] + ([hf] if final else []) + [cw, cb, wg, bg, lam]
    specs = [blk_spec(two_w), prev, nxt] + ([blk_spec(width)] if final else []) + [
        _const_spec(cw.shape), _const_spec(cb.shape), _const_spec(wg.shape), _const_spec(bg.shape),
        _const_spec(lam.shape)]
    return pl.pallas_call(
        functools.partial(_lru_kernel, rev, final, tl),
        grid=(nb, nblk),
        in_specs=specs,
        out_specs=blk_spec(width),
        out_shape=jax.ShapeDtypeStruct((nb, seq_len, width), F32),
        scratch_shapes=[pltpu.VMEM((SUBLANES, width), F32),
                        pltpu.VMEM((tl + 2 * CONV_HALO, width), F32),
                        pltpu.VMEM((width // LANES, SUBLANES * pitch, LANES), F32),
                        pltpu.VMEM((width // LANES, SUBLANES * pitch, LANES), F32)],
        compiler_params=_params("parallel", "arbitrary"),
        name="lru_bwd" if rev else "lru_fwd",
    )(*ins)


def _gla_kernel(tl, heads, dk, dv, mf_ref, mb_ref, sf_ref, sb_ref, wup_ref, bup_ref, cumf_ref, cumb_ref,
                of_ref, ob_ref, st_ref):
    i = pl.program_id(1)
    hk = heads * dk
    nchunk = tl // CHUNK

    @pl.when(i == 0)
    def _():
        st_ref[...] = jnp.zeros_like(st_ref)

    hv = heads * dv
    kk_blocks = _block_mask(heads * CHUNK, hk, CHUNK, dk).astype(MXU_DTYPE)
    kv_blocks = _block_mask(heads * CHUNK, hv, CHUNK, dv).astype(MXU_DTYPE)
    st_blocks = _block_mask(hv, hk, dv, dk).astype(F32)
    dirs = ((False, mf_ref, sf_ref, cumf_ref, of_ref), (True, mb_ref, sb_ref, cumb_ref, ob_ref))
    for d, (rev, m_ref, s_ref, cum_ref, o_ref) in enumerate(dirs):
        incl = _tri_mask_heads(rev, False, heads)
        end = 0 if rev else CHUNK - 1
        pre = _mm(s_ref[...], wup_ref[d]) + bup_ref[d]
        la = (jnp.minimum(pre, 0.0) - jnp.log1p(jnp.exp(-jnp.abs(pre)))) * (1.0 / GLA_TAU)
        b = _cum_dot(cum_ref[...], la)
        q_dec = (m_ref[:, 0:hk] * (dk ** -0.5) * jnp.exp(b)).astype(MXU_DTYPE)
        k = m_ref[:, hk:2 * hk]
        k_inv = (k * jnp.exp(-b)).astype(MXU_DTYPE)
        state = st_ref[d]
        for c in _chunk_order(rev, nchunk):
            rows = slice(c * CHUNK, (c + 1) * CHUNK)
            b_end = b[c * CHUNK + end:c * CHUNK + end + 1, :]
            k_end = k[rows, :] * jnp.exp(b_end - b[rows, :])
            v = m_ref[rows, 2 * hk:2 * hk + hv].astype(MXU_DTYPE)
            k_exp = jnp.concatenate([k_inv[rows, :]] * heads, axis=0) * kk_blocks
            scores = jnp.where(incl, _mm_nt(q_dec[rows, :], k_exp), 0.0)
            kv_t = _mm_tn(v, k_end)
            v_exp = jnp.concatenate([v] * heads, axis=0) * kv_blocks
            o_ref[rows, :] = _mm(scores, v_exp) + _mm_nt(q_dec[rows, :], state)
            state = state * jnp.exp(b_end) + kv_t * st_blocks
        st_ref[d] = state


def _gla_both(proj, small, wup, bup, heads, dk, dv):
    nb, seq_len, pw = proj.shape
    tl = TL_MIX
    nblk = seq_len // tl
    fwd = lambda wd: pl.BlockSpec((None, tl, wd), lambda b, i: (b, i, 0))
    bwd = lambda wd: pl.BlockSpec((None, tl, wd), lambda b, i: (b, nblk - 1 - i, 0))
    hv = heads * dv
    sw = small.shape[2]
    cum_f, cum_b = _block_cum(False, tl), _block_cum(True, tl)
    out = jax.ShapeDtypeStruct((nb, seq_len, hv), F32)
    return pl.pallas_call(
        functools.partial(_gla_kernel, tl, heads, dk, dv),
        grid=(nb, nblk),
        in_specs=[fwd(pw), bwd(pw), fwd(sw), bwd(sw), _const_spec(wup.shape), _const_spec(bup.shape),
                  _const_spec(cum_f.shape), _const_spec(cum_b.shape)],
        out_specs=[fwd(hv), bwd(hv)],
        out_shape=[out, out],
        scratch_shapes=[pltpu.VMEM((2, heads * dv, heads * dk), F32)],
        compiler_params=_params("parallel", "arbitrary"),
        name="gla",
    )(proj, proj, small, small, wup, bup, cum_f, cum_b)


def _dn_gates(small, gc_ref):
    beta = _sigmoid(small)
    g = -jnp.exp(gc_ref[0:1, :]) * _softplus(small + gc_ref[1:2, :])
    return jnp.where(gc_ref[2:3, :] > 0.5, beta, jnp.where(gc_ref[3:4, :] > 0.5, g, 0.0))


def _dn_forms(gx, d, heads):
    assert 2 * heads == SUBLANES and 2 * CHUNK == LANES
    base = 2 * heads * d
    rows8 = gx.T[base:base + SUBLANES, :]
    rows8_rot = pltpu.roll(rows8, CHUNK, axis=1)
    rep = lambda lane: jnp.broadcast_to(gx[:, lane:lane + 1], (gx.shape[0], LANES))
    return rows8, rows8_rot, [rep(base + h) for h in range(heads)], [rep(base + heads + h) for h in range(heads)]


def _head_rows(rows8, rows8_rot, r0, c, heads):
    nchunk = rows8.shape[1] // CHUNK
    low_half = lax.broadcasted_iota(jnp.int32, (1, LANES), 1) < CHUNK
    out = []
    for j in range(heads // 2):
        halves = []
        for h in (2 * j, 2 * j + 1):
            src, slot = (rows8, c) if c % 2 == h % 2 else (rows8_rot, (c + 1) % nchunk)
            halves.append(src[r0 + h:r0 + h + 1, LANES * (slot // 2):LANES * (slot // 2 + 1)])
        out.append(jnp.where(low_half, halves[0], halves[1]))
    return jnp.concatenate(out, axis=1)


def _head_cols(rep, c, heads):
    low_half = lax.broadcasted_iota(jnp.int32, (1, LANES), 1) < CHUNK
    rows = slice(c * CHUNK, (c + 1) * CHUNK)
    return jnp.concatenate([jnp.where(low_half, rep[2 * j][rows, :], rep[2 * j + 1][rows, :])
                            for j in range(heads // 2)], axis=1)


def _dn_gamma(rev, forms, c, heads):
    rows8, rows8_rot, _, g_rep = forms
    incl = _tri_mask_heads(rev, False, heads)
    diff = _head_cols(g_rep, c, heads) - _head_rows(rows8, rows8_rot, heads, c, heads)
    return jnp.where(incl, jnp.exp(jnp.where(incl, diff, 0.0)), 0.0)


def _head_expand(x, blocks, heads):
    return jnp.concatenate([x.astype(MXU_DTYPE)] * heads, axis=0) * blocks


def _dn_prep_kernel(tl, heads, dh, main_ref, prev_ref, next_ref, small_ref, cw_ref, gcst_ref, cumf_ref, cumb_ref,
                    qkv_ref, gx_ref, af_ref, ab_ref, ext_ref):
    i = pl.program_id(1)
    nblk = pl.num_programs(1)
    hw = heads * dh
    _conv_stage(ext_ref, main_ref[:, 0:3 * hw], prev_ref[...], next_ref[...], i == 0, i == nblk - 1)
    for part, scale in ((0, dh ** -0.5), (1, 1.0), (2, None)):
        for h in range(heads):
            sl = slice(part * hw + h * dh, part * hw + (h + 1) * dh)
            z = _silu(_centred_conv(ext_ref, cw_ref, tl, sl))
            if scale is not None:
                z = z * lax.rsqrt(jnp.sum(z * z, axis=-1, keepdims=True) + NORM_EPS) * scale
            qkv_ref[:, sl] = z
    bg = _dn_gates(small_ref[...], gcst_ref)
    lane = lax.broadcasted_iota(jnp.int32, bg.shape, 1)
    run = jnp.where(lane < 2 * heads, _cum_dot(cumf_ref[...], bg), _cum_dot(cumb_ref[...], bg))
    gx = jnp.where(gcst_ref[3:4, :] > 0.5, run, bg)
    gx_ref[...] = gx
    blocks = _block_mask(heads * CHUNK, hw, CHUNK, dh).astype(MXU_DTYPE)
    for d, (rev, a_ref) in enumerate(((False, af_ref), (True, ab_ref))):
        strict = _tri_mask_heads(rev, True, heads)
        forms = _dn_forms(gx, d, heads)
        for c in range(tl // CHUNK):
            k = qkv_ref[c * CHUNK:(c + 1) * CHUNK, hw:2 * hw]
            kk = _mm_nt(k, _head_expand(k, blocks, heads))
            a_ref[c] = jnp.where(strict, kk * _head_cols(forms[2], c, heads) * _dn_gamma(rev, forms, c, heads), 0.0)


def _dn_prep(proj, small, cw, gate_consts, heads, dh):
    nb, seq_len, pw = proj.shape
    tl = TL_MIX
    nblk = seq_len // tl
    hw = heads * dh
    blk_spec = lambda wd: pl.BlockSpec((None, tl, wd), lambda b, i: (b, i, 0))
    prev, nxt = _halo_specs(False, tl, 3 * hw, seq_len)
    a_spec = pl.BlockSpec((None, tl // CHUNK, CHUNK, heads * CHUNK), lambda b, i: (b, i, 0, 0))
    a_shape = jax.ShapeDtypeStruct((nb, seq_len // CHUNK, CHUNK, heads * CHUNK), F32)
    lanes_shape = jax.ShapeDtypeStruct((nb, seq_len, LANES), F32)
    cum_f, cum_b = _block_cum(False, tl), _block_cum(True, tl)
    return pl.pallas_call(
        functools.partial(_dn_prep_kernel, tl, heads, dh),
        grid=(nb, nblk),
        in_specs=[blk_spec(pw), prev, nxt, blk_spec(small.shape[2]), _const_spec(cw.shape),
                  _const_spec(gate_consts.shape), _const_spec(cum_f.shape), _const_spec(cum_b.shape)],
        out_specs=[blk_spec(3 * hw), blk_spec(LANES), a_spec, a_spec],
        out_shape=[jax.ShapeDtypeStruct((nb, seq_len, 3 * hw), F32), lanes_shape, a_shape, a_shape],
        scratch_shapes=[pltpu.VMEM((tl + 2 * CONV_HALO, 3 * hw), F32)],
        compiler_params=_params("parallel", "parallel"),
        name="dn_prep",
    )(proj, proj, proj, small, cw, gate_consts, cum_f, cum_b)


def _dn_solve_kernel(upper, heads, a_ref, t_ref, a_scr, t_scr):
    nl = a_ref.shape[0]
    hc = heads * CHUNK
    half = CHUNK // 2
    for t in range(CHUNK):
        a_scr[t] = a_ref[:, t * hc:(t + 1) * hc].T.reshape(heads, CHUNK, nl)
    col = lax.broadcasted_iota(jnp.int32, (half, nl), 0)
    t_scr[...] = jnp.zeros_like(t_scr)

    def row(idx, carry):
        t = (CHUNK - 1 - idx) if upper else idx

        def term_lo(s, accs):
            return tuple(acc - a_scr[t, h, pl.ds(s, 1), :] * t_scr[s, h, 0:half, :] for h, acc in enumerate(accs))

        def term_hi(s, accs):
            return tuple(acc - a_scr[t, h, pl.ds(s, 1), :] * t_scr[s, h, half:, :] for h, acc in enumerate(accs))

        def pairs(term, count, first, step):
            def body(i, accs):
                s = first + 2 * step * i
                return term(s + step, term(s, accs))
            return functools.partial(lax.fori_loop, 0, (count + 1) // 2, body)

        if upper:
            lo_loop = pairs(term_lo, jnp.maximum(half - 1 - t, 0), half - 1, -1)
            hi_loop = pairs(term_hi, CHUNK - 1 - t, CHUNK - 1, -1)
        else:
            lo_loop = pairs(term_lo, t, 0, 1)
            hi_loop = pairs(term_hi, jnp.maximum(t - half, 0), half, 1)
        lo = lo_loop((jnp.where(col == t, 1.0, 0.0),) * heads)
        hi = hi_loop((jnp.where(col + half == t, 1.0, 0.0),) * heads)
        for h in range(heads):
            t_scr[t, h, 0:half, :] = lo[h]
            t_scr[t, h, half:, :] = hi[h]
        return carry

    lax.fori_loop(0, CHUNK, row, 0)
    for t in range(CHUNK):
        t_ref[:, t * hc:(t + 1) * hc] = t_scr[t].reshape(hc, nl).T.astype(t_ref.dtype)


def _dn_solve(a, upper, heads):
    nb, nc = a.shape[:2]
    flat = a.reshape(nb * nc, CHUNK * heads * CHUNK)
    nl = min(SOLVE_CHUNKS, nb * nc)
    spec = pl.BlockSpec((nl, flat.shape[1]), lambda i: (i, 0))
    scr = pltpu.VMEM((CHUNK, heads, CHUNK, nl), F32)
    return pl.pallas_call(
        functools.partial(_dn_solve_kernel, upper, heads),
        grid=(nb * nc // nl,),
        in_specs=[spec],
        out_specs=spec,
        out_shape=jax.ShapeDtypeStruct(flat.shape, MXU_DTYPE),
        scratch_shapes=[scr, scr],
        compiler_params=_params("parallel"),
        name="dn_solve_bwd" if upper else "dn_solve_fwd",
    )(flat).reshape(a.shape)


def _dn_main_kernel(tl, heads, dh, qf_ref, gxf_ref, tf_ref, qb_ref, gxb_ref, tb_ref, of_ref, ob_ref, s_ref):
    i = pl.program_id(1)
    hw = heads * dh
    nchunk = tl // CHUNK
    npair = heads // 2
    pw = 2 * dh

    @pl.when(i == 0)
    def _():
        s_ref[...] = jnp.zeros_like(s_ref)

    blocks = _block_mask(heads * CHUNK, hw, CHUNK, dh).astype(MXU_DTYPE)
    t_blocks = _block_mask(heads * CHUNK, heads * CHUNK, CHUNK, CHUNK).astype(MXU_DTYPE)
    pair_blocks = _block_mask(pw, pw, dh, dh).astype(F32)
    stack = lambda x: jnp.concatenate([x[:, h * dh:(h + 1) * dh] for h in range(heads)], axis=0)
    dirs = ((False, qf_ref, gxf_ref, tf_ref, of_ref), (True, qb_ref, gxb_ref, tb_ref, ob_ref))
    for d, (rev, qkv_ref, gx_ref, t_ref, o_ref) in enumerate(dirs):
        incl = _tri_mask_heads(rev, False, heads)
        end = 0 if rev else CHUNK - 1
        forms = _dn_forms(gx_ref[...], d, heads)
        b_nat = jnp.concatenate(forms[2], axis=1)
        g_nat = jnp.concatenate(forms[3], axis=1)
        eg_nat = jnp.exp(g_nat)
        states = [s_ref[d, j] for j in range(npair)]
        for c in _chunk_order(rev, nchunk):
            rows = slice(c * CHUNK, (c + 1) * CHUNK)
            q = qkv_ref[rows, 0:hw]
            k = qkv_ref[rows, hw:2 * hw]
            v = qkv_ref[rows, 2 * hw:3 * hw]
            eg = eg_nat[rows, :]
            beta = b_nat[rows, :]
            g_end = g_nat[c * CHUNK + end:c * CHUNK + end + 1, :]
            k_end = k * jnp.exp(g_end - g_nat[rows, :])
            e_end = jnp.exp(g_end)
            t_bd = _head_expand(t_ref[c], t_blocks, heads)
            wu = _mm(t_bd, jnp.concatenate([stack(k * beta * eg), stack(v * beta)], axis=1))
            attn = jnp.where(incl, _mm_nt(q, _head_expand(k, blocks, heads)) * _dn_gamma(rev, forms, c, heads), 0.0)
            w = jnp.concatenate([wu[h * CHUNK:(h + 1) * CHUNK, 0:dh] for h in range(heads)], axis=1)
            u = jnp.concatenate([wu[h * CHUNK:(h + 1) * CHUNK, dh:] for h in range(heads)], axis=1)
            lhs = jnp.concatenate([w, q * eg], axis=0)
            xs = [_mm(lhs[:, j * pw:(j + 1) * pw], states[j]) for j in range(npair)]
            v_new = u - jnp.concatenate([x[0:CHUNK, :] for x in xs], axis=1)
            o_ref[rows, :] = (jnp.concatenate([x[CHUNK:, :] for x in xs], axis=1)
                              + _mm(attn, _head_expand(v_new, blocks, heads)))
            for j in range(npair):
                ps = slice(j * pw, (j + 1) * pw)
                states[j] = states[j] * e_end[:, ps] + _mm_tn(k_end[:, ps], v_new[:, ps]) * pair_blocks
        for j in range(npair):
            s_ref[d, j] = states[j]


def _dn_main(qkv, gx, t_f, t_b, heads, dh):
    nb, seq_len, _ = qkv.shape
    tl = TL_MIX
    nblk = seq_len // tl
    hw = heads * dh
    fwd = lambda wd: pl.BlockSpec((None, tl, wd), lambda b, i: (b, i, 0))
    bwd = lambda wd: pl.BlockSpec((None, tl, wd), lambda b, i: (b, nblk - 1 - i, 0))
    t_blk = (None, tl // CHUNK, CHUNK, heads * CHUNK)
    t_fwd = pl.BlockSpec(t_blk, lambda b, i: (b, i, 0, 0))
    t_bwd = pl.BlockSpec(t_blk, lambda b, i: (b, nblk - 1 - i, 0, 0))
    out = jax.ShapeDtypeStruct((nb, seq_len, hw), F32)
    return pl.pallas_call(
        functools.partial(_dn_main_kernel, tl, heads, dh),
        grid=(nb, nblk),
        in_specs=[fwd(3 * hw), fwd(LANES), t_fwd, bwd(3 * hw), bwd(LANES), t_bwd],
        out_specs=[fwd(hw), bwd(hw)],
        out_shape=[out, out],
        scratch_shapes=[pltpu.VMEM((2, heads // 2, 2 * dh, 2 * dh), F32)],
        compiler_params=_params("parallel", "arbitrary"),
        name="dn_main",
    )(qkv, gx, t_f, qkv, gx, t_b)


def _dn_branch(proj, small, cw, gate_consts, heads, dh):
    qkv, gx, a_f, a_b = _dn_prep(proj, small, cw, gate_consts, heads, dh)
    return _dn_main(qkv, gx, _dn_solve(a_f, False, heads), _dn_solve(a_b, True, heads), heads, dh)


S5_CHUNK = 16


def _cpow(lam_re, lam_im, dt, e):
    mag = jnp.exp(lam_re * dt * e)
    ang = lam_im * dt * e
    return mag * jnp.cos(ang), mag * jnp.sin(ang)


def _s5_wt_kernel(gw, lre_ref, lim_ref, ldt_ref, bre_ref, bim_ref, ctre_ref, ctim_ref, cre_ref, cim_ref,
                  k_ref, zwre_ref, zwim_ref, cwre_ref, cwimn_ref):
    backward = pl.program_id(0) == 1
    lam_re, lam_im, dt = lre_ref[...], lim_ref[...], jnp.exp(ldt_ref[...])
    a_re, a_im = _cpow(lam_re, lam_im, dt, 1.0)
    den = lam_re * lam_re + lam_im * lam_im
    n_re = a_re - 1.0
    f_re = (n_re * lam_re + a_im * lam_im) / den
    f_im = (a_im * lam_re - n_re * lam_im) / den
    bb_re = f_re * bre_ref[...] - f_im * bim_ref[...]
    bb_im = f_re * bim_ref[...] + f_im * bre_ref[...]
    pos = jnp.right_shift(lax.broadcasted_iota(jnp.int32, lam_re.shape, 1), _log2(gw)).astype(F32)
    p_re, p_im = _cpow(lam_re, lam_im, dt, pos)
    e_re = p_re * bb_re - p_im * bb_im
    e_im = p_re * bb_im + p_im * bb_re
    k_ref[...] = (jnp.dot(cre_ref[...], e_re, precision=HI, preferred_element_type=F32)
                  - jnp.dot(cim_ref[...], e_im, precision=HI, preferred_element_type=F32))
    z_re, z_im = _cpow(lam_re, lam_im, dt, jnp.where(backward, pos, (S5_CHUNK - 1.0) - pos))
    zwre_ref[...] = z_re * bb_re - z_im * bb_im
    zwim_ref[...] = z_re * bb_im + z_im * bb_re
    c_re, c_im = _cpow(lam_re, lam_im, dt, jnp.where(backward, S5_CHUNK - pos, pos + 1.0))
    cwre_ref[...] = ctre_ref[...] * c_re - ctim_ref[...] * c_im
    cwimn_ref[...] = -(ctre_ref[...] * c_im + ctim_ref[...] * c_re)


def _s5_weights(lam_re, lam_im, log_dt, b_re, b_im, c_re, c_im):
    _, groups, states, gw = b_re.shape
    wl = S5_CHUNK * gw
    rep = lambda a: jnp.broadcast_to(a[..., None], (2, groups, states, wl))
    tile = lambda a: jnp.tile(a, (1, 1, 1, S5_CHUNK))
    blk = lambda r, c: pl.BlockSpec((None, None, r, c), lambda d, g: (d, g, 0, 0))
    ct = lambda a: tile(a.transpose(0, 1, 3, 2))
    big = jax.ShapeDtypeStruct((2, groups, states, wl), F32)
    return pl.pallas_call(
        functools.partial(_s5_wt_kernel, gw),
        grid=(2, groups),
        in_specs=[blk(states, wl)] * 7 + [blk(gw, states)] * 2,
        out_specs=[blk(gw, wl)] + [blk(states, wl)] * 4,
        out_shape=[jax.ShapeDtypeStruct((2, groups, gw, wl), F32), big, big, big, big],
        compiler_params=_params("parallel", "parallel"),
        name="s5_wt",
    )(rep(lam_re), rep(lam_im), rep(jnp.broadcast_to(log_dt[..., None], lam_re.shape)), tile(b_re), tile(b_im),
      ct(c_re), ct(c_im), c_re, c_im)


def _s5_pow_kernel(seg, lre_ref, lim_ref, ldt_ref, pw_ref):
    a_re, a_im = _cpow(lre_ref[...], lim_ref[...], jnp.exp(ldt_ref[...]), float(S5_CHUNK))
    x_re, x_im = a_re, a_im
    for k in range(seg):
        pw_ref[0, k] = x_re
        pw_ref[1, k] = x_im
        x_re, x_im = x_re * a_re - x_im * a_im, x_re * a_im + x_im * a_re


def _s5_pow(lre, lim, ldt, seg):
    return pl.pallas_call(
        functools.partial(_s5_pow_kernel, seg),
        out_shape=jax.ShapeDtypeStruct((2, seg) + lre.shape, F32),
        name="s5_pow",
    )(lre, lim, ldt)


def _gelu_tanh(x):
    return 0.5 * x * (1.0 + jnp.tanh(math.sqrt(2.0 / math.pi) * (x + 0.044715 * (x * x * x))))


def _s5_mix_kernel(nr, u_ref, tf_ref, tb_ref, zw_ref, cw_ref, pw_ref, y_ref, pad_re, pad_im, f_re, f_im, b_re, b_im):
    seg = nr // SUBLANES
    pitch = seg + SEG_PAD
    half = LANES // 2
    fwd_lanes = lax.broadcasted_iota(jnp.int32, (1, LANES), 1) < half
    w1 = jnp.concatenate([(tf_ref[...] + tb_ref[...]).astype(MXU_DTYPE), zw_ref[...]], axis=1)
    yz = jnp.dot(u_ref[...], w1, preferred_element_type=F32)
    wl = tf_ref.shape[1]

    for pad, col in ((pad_re, wl), (pad_im, wl + LANES)):
        pad[0:SUBLANES, :] = jnp.zeros((SUBLANES, LANES), F32)
        pad[SUBLANES:SUBLANES + nr, :] = yz[:, col:col + LANES]
        pad[SUBLANES + nr:, :] = jnp.zeros((SUBLANES, LANES), F32)
    for s in range(SUBLANES):
        dst = slice(s * pitch, s * pitch + seg)
        for pad, fw, bw in ((pad_re, f_re, b_re), (pad_im, f_im, b_im)):
            fw[dst, :] = pad[pl.ds(SUBLANES - 1 + s * seg, seg), :]
            bw[dst, :] = pad[pl.ds(SUBLANES + 1 + s * seg, seg), :]

    a_re = jnp.broadcast_to(pw_ref[0, 0:1, :], (SUBLANES, LANES))
    a_im = jnp.broadcast_to(pw_ref[1, 0:1, :], (SUBLANES, LANES))
    for re_scr, im_scr, order in ((f_re, f_im, range(seg)), (b_re, b_im, range(seg - 1, -1, -1))):
        sr = jnp.zeros((SUBLANES, LANES), F32)
        si = jnp.zeros((SUBLANES, LANES), F32)
        for k in order:
            sr, si = (a_re * sr - a_im * si + re_scr[pl.ds(k, SUBLANES, stride=pitch), :],
                      a_re * si + a_im * sr + im_scr[pl.ds(k, SUBLANES, stride=pitch), :])
            re_scr[pl.ds(k, SUBLANES, stride=pitch), :] = sr
            im_scr[pl.ds(k, SUBLANES, stride=pitch), :] = si

    pr, pi = pw_ref[2], pw_ref[3]
    sg_re = pw_ref[0, seg - 1:seg, :]
    sg_im = pw_ref[1, seg - 1:seg, :]
    zero = jnp.zeros((1, LANES), F32)
    cf, cb = [(zero, zero)], [(zero, zero)]
    for s in range(SUBLANES - 1):
        er, ei = f_re[pl.ds(s * pitch + seg - 1, 1), :], f_im[pl.ds(s * pitch + seg - 1, 1), :]
        cr, ci = cf[-1]
        cf.append((sg_re * cr - sg_im * ci + er, sg_re * ci + sg_im * cr + ei))
        t = SUBLANES - 1 - s
        er, ei = b_re[pl.ds(t * pitch, 1), :], b_im[pl.ds(t * pitch, 1), :]
        cr, ci = cb[-1]
        cb.append((sg_re * cr - sg_im * ci + er, sg_re * ci + sg_im * cr + ei))
    cb = cb[::-1]
    parts_re, parts_im = [], []
    for s in range(SUBLANES):
        src = slice(s * pitch, s * pitch + seg)
        cr = jnp.where(fwd_lanes, cf[s][0], cb[s][0])
        ci = jnp.where(fwd_lanes, cf[s][1], cb[s][1])
        parts_re.append(jnp.where(fwd_lanes, f_re[src, :], b_re[src, :]) + pr * cr - pi * ci)
        parts_im.append(jnp.where(fwd_lanes, f_im[src, :], b_im[src, :]) + pr * ci + pi * cr)
    states = jnp.concatenate([jnp.concatenate(parts_re, axis=0), jnp.concatenate(parts_im, axis=0)], axis=1)
    y_ref[...] = yz[:, 0:wl] + _mm(states, cw_ref[...])


def _s5_mix(u_g, toep_f, toep_b, zw, cw, pw):
    groups, nb, nr, wl = u_g.shape
    seg = nr // SUBLANES
    pitch = seg + SEG_PAD
    per_g = lambda r, c: pl.BlockSpec((None, r, c), lambda g, b: (g, 0, 0))
    rows = pl.BlockSpec((None, None, nr, wl), lambda g, b: (g, b, 0, 0))
    scan = pltpu.VMEM((SUBLANES * pitch, LANES), F32)
    pad = pltpu.VMEM((nr + 2 * SUBLANES, LANES), F32)
    return pl.pallas_call(
        functools.partial(_s5_mix_kernel, nr),
        grid=(groups, nb),
        in_specs=[rows, per_g(wl, wl), per_g(wl, wl), per_g(wl, 2 * LANES), per_g(2 * LANES, wl),
                  pl.BlockSpec((None, 4, seg, LANES), lambda g, b: (g, 0, 0, 0))],
        out_specs=rows,
        out_shape=jax.ShapeDtypeStruct((groups, nb, nr, wl), F32),
        scratch_shapes=[pad, pad, scan, scan, scan, scan],
        compiler_params=_params("parallel", "parallel"),
        name="s5_mix",
    )(u_g, toep_f, toep_b, zw, cw, pw)


def _rows_to_groups(x, gw, o_ref, xs, acc):
    per = LANES // gw
    nrow = x.shape[0] // S5_CHUNK
    nslab = x.shape[1] // LANES
    for v in range(nslab):
        xs[v] = x[:, v * LANES:(v + 1) * LANES]
    for r in range(S5_CHUNK):
        dst = (r // per) * LANES + (r % per) * gw
        for v in range(nslab):
            rows = xs[v, pl.ds(r, nrow, stride=S5_CHUNK), :]
            for s in range(per):
                rot = rows if s == 0 else pltpu.roll(rows, s * gw, axis=1)
                g = per * v + (r - s) % per
                acc[g, :, dst:dst + gw] = rot[:, (r % per) * gw:(r % per + 1) * gw]
    o_ref[...] = acc[...].astype(o_ref.dtype)


def _groups_to_rows(y_ref, gw, tmp, ynat):
    per = LANES // gw
    tl, width = ynat.shape
    nrow = tl // S5_CHUNK
    nslab = width // LANES
    for g in range(y_ref.shape[0]):
        v, slot = g // per, g % per
        for h in range(S5_CHUNK // per):
            rows = y_ref[g, :, h * LANES:(h + 1) * LANES]
            for s in range(per):
                rot = rows if s == 0 else pltpu.roll(rows, s * gw, axis=1)
                r = h * per + (slot - s) % per
                tmp[v, r * nrow:(r + 1) * nrow, slot * gw:(slot + 1) * gw] = rot[:, slot * gw:(slot + 1) * gw]
    for n in range(nrow):
        for r0 in range(0, S5_CHUNK, SUBLANES):
            for v in range(nslab):
                ynat[n * S5_CHUNK + r0:n * S5_CHUNK + r0 + SUBLANES, v * LANES:(v + 1) * LANES] = (
                    tmp[v, pl.ds(r0 * nrow + n, SUBLANES, stride=nrow), :])
    return ynat[...]


def _s5_finish(y, main, dsk, wglu, bglu):
    width = y.shape[1]
    z = _gelu_tanh(y + dsk * main[:, 0:width])
    z = z * _sigmoid(_mm(z, wglu) + bglu)
    return z * _silu(main[:, width:])


def _s5_toeplitz_branch(u_g, lam_re, lam_im, log_dt, b_re, b_im, c_re, c_im):
    _, groups, states, gw = b_re.shape
    nr = u_g.shape[2]
    seg = nr // SUBLANES
    wl = S5_CHUNK * gw
    kk, zw_re, zw_im, cw_re, cw_imn = _s5_weights(lam_re, lam_im, log_dt, b_re, b_im, c_re, c_im)

    lag = jnp.arange(S5_CHUNK)[None, :] - jnp.arange(S5_CHUNK)[:, None]
    kt = kk.reshape(2, groups, gw, S5_CHUNK, gw).transpose(0, 1, 3, 4, 2)

    def toeplitz(d, lag_d):
        t = jnp.where((lag_d >= 0)[None, :, :, None, None], kt[d][:, jnp.maximum(lag_d, 0)], 0.0)
        return t.transpose(0, 1, 3, 2, 4).reshape(groups, wl, wl)

    toep_f, toep_b = toeplitz(0, lag), toeplitz(1, -lag)
    by_rows = lambda a: a.transpose(0, 2, 1)
    zw = jnp.concatenate([by_rows(zw_re[0]), by_rows(zw_re[1]), by_rows(zw_im[0]), by_rows(zw_im[1])],
                         axis=2).astype(MXU_DTYPE)
    cw = jnp.concatenate([cw_re[0], cw_re[1], cw_imn[0], cw_imn[1]], axis=1).astype(MXU_DTYPE)

    flat = lambda a: a.reshape(2, groups * states // LANES, LANES)
    pw = _s5_pow(flat(lam_re), flat(lam_im), flat(jnp.broadcast_to(log_dt[..., None], lam_re.shape)), seg)
    pw = pw.reshape(2, seg, 2, groups, states).transpose(2, 3, 0, 1, 4)
    plain = jnp.concatenate([pw[0], pw[1]], axis=3)
    by_row = jnp.concatenate([pw[0], jnp.flip(pw[1], axis=2)], axis=3)
    pw_all = jnp.concatenate([plain, by_row], axis=1)

    return _s5_mix(u_g, toep_f, toep_b, zw, cw, pw_all)


def _layer(x_parts, group_rows, seq_len, p, final_g, final):
    d = x_parts[0].shape[1]
    nb = sum(a.shape[0] for a in x_parts) // seq_len
    bw = d // 2
    bf = lambda a: a.astype(MXU_DTYPE)
    row = lambda a: a.reshape(1, -1)

    lru_blocks = p['lru_w_a'].shape[1]
    gla_heads = GLA_HEADS
    gla_dk = p['gla_w_up'].shape[2] // gla_heads
    gla_dv = bw // gla_heads
    gla_rank = p['gla_w_up'].shape[1]
    dn_heads = p['dn_a_log'].shape[1]
    dn_dh = bw // dn_heads
    widths = (bw, bw, gla_heads * gla_dk, gla_heads * gla_dk, bw, bw, 2 * gla_rank, 3 * bw, bw, 4 * dn_heads, bw, bw)
    offs = [0]
    for wd in widths:
        offs.append(offs[-1] + wd)
    w_in = p['w_in']
    cols = lambda a, b: w_in[:, offs[a]:offs[b]]
    n_gate = 4 * dn_heads
    n_small = n_gate + 2 * gla_rank
    w_small = jnp.concatenate([cols(9, 10), cols(6, 7), jnp.zeros((d, LANES - n_small), w_in.dtype)], axis=1)
    w_cat = bf(jnp.concatenate([cols(0, 2), cols(2, 6), cols(7, 9), cols(10, 12), w_small], axis=1))
    out_w = (2 * bw, offs[6] - offs[2], 4 * bw, 2 * bw, LANES)
    s5_groups, s5_gw = p['s5_b_re'].shape[1], p['s5_b_re'].shape[3]
    lru_p, gla_p, dn_p, s5_p, small, s5_ug = _in_proj(x_parts, row(p['norm_g']), w_cat, out_w, seq_len, 3,
                                                      s5_groups, s5_gw)
    seq = lambda a: a.reshape(nb, seq_len, a.shape[1])
    flat = lambda a: a.reshape(nb * seq_len, a.shape[2])
    lru_s, gla_s, dn_s, small_s = seq(lru_p), seq(gla_p), seq(dn_p), seq(small)

    lanes_per_blk = bw // lru_blocks
    y_lru = None
    for dct, rev in enumerate((False, True)):
        wg = bf(jnp.concatenate([p['lru_w_a'][dct], p['lru_w_x'][dct]], axis=2))
        bg = jnp.concatenate([p['lru_b_a'][dct].reshape(lru_blocks, 1, lanes_per_blk),
                              p['lru_b_x'][dct].reshape(lru_blocks, 1, lanes_per_blk)], axis=2)
        y_lru = _lru_dir(rev, lru_s, y_lru, p['lru_conv_w'], row(p['lru_conv_b']), wg, bg, row(p['lru_lambda'][dct]))

    wup = jnp.zeros((2, LANES, gla_heads * gla_dk), F32)
    for dct in range(2):
        lo = n_gate + dct * gla_rank
        wup = wup.at[dct, lo:lo + gla_rank].set(p['gla_w_up'][dct])
    gla_f, gla_b = _gla_both(gla_s, small_s, bf(wup), p['gla_b_up'].reshape(2, 1, -1), gla_heads, gla_dk, gla_dv)

    alpha_lanes = (jnp.arange(2)[:, None] * 2 * dn_heads + dn_heads + jnp.arange(dn_heads)[None, :]).reshape(-1)
    beta_lanes = alpha_lanes - dn_heads
    gate_consts = jnp.zeros((SUBLANES, LANES), F32)
    gate_consts = gate_consts.at[0, alpha_lanes].set(p['dn_a_log'].reshape(-1))
    gate_consts = gate_consts.at[1, alpha_lanes].set(p['dn_dt_bias'].reshape(-1))
    gate_consts = gate_consts.at[2, beta_lanes].set(1.0).at[3, alpha_lanes].set(1.0)
    dn_f, dn_b = _dn_branch(dn_s, small_s, p['dn_conv_w'], gate_consts, dn_heads, dn_dh)

    s5_yg = _s5_toeplitz_branch(s5_ug, p['s5_lambda_re'], p['s5_lambda_im'], p['s5_log_dt'], p['s5_b_re'],
                                p['s5_b_im'], p['s5_c_re'], p['s5_c_im'])

    gla_gate_col = (offs[5] - offs[2]) // bw
    dn_gate_col = (offs[8] - offs[7]) // bw
    rest = (seq_len, flat(y_lru),
            (flat(gla_f), flat(gla_b), gla_p, gla_gate_col, row(p['gla_norm_g']), gla_heads),
            (flat(dn_f), flat(dn_b), dn_p, dn_gate_col, row(p['dn_norm_g']), dn_heads),
            (s5_yg, s5_p, row(p['s5_d']), bf(p['s5_w_glu']), row(p['s5_b_glu']), s5_gw), row(p['norm_g']),
            bf(p['w_merge_gate']), p['b_merge_gate'].reshape(4, 1, d), bf(p['w_branch']), bf(p['w_out']),
            row(final_g), final)
    if not final:
        return [_out_proj([_whole(a) for a in x_parts], 0, *rest)]
    outs, row0 = [], 0
    for gi, rows in enumerate(group_rows):
        x_range = (x_parts[0], row0, rows) if len(x_parts) == 1 else _whole(x_parts[gi])
        outs.append(_out_proj([x_range], row0, *rest))
        row0 += rows
    return outs


_LAYER_PARAM_NAMES = (
    'norm_g', 'w_in', 'lru_conv_w', 'lru_conv_b', 'lru_w_a', 'lru_b_a', 'lru_w_x', 'lru_b_x', 'lru_lambda',
    'gla_w_up', 'gla_b_up', 'gla_norm_g', 'dn_conv_w', 'dn_a_log', 'dn_dt_bias', 'dn_norm_g',
    's5_lambda_re', 's5_lambda_im', 's5_log_dt', 's5_b_re', 's5_b_im', 's5_c_re', 's5_c_im', 's5_d',
    's5_w_glu', 's5_b_glu', 'w_branch', 'w_merge_gate', 'b_merge_gate', 'w_out')


def kernel(x_prompt, x_sample, norm_g, w_in, lru_conv_w, lru_conv_b, lru_w_a, lru_b_a, lru_w_x, lru_b_x, lru_lambda, gla_w_up, gla_b_up, gla_norm_g, dn_conv_w, dn_a_log, dn_dt_bias, dn_norm_g, s5_lambda_re, s5_lambda_im, s5_log_dt, s5_b_re, s5_b_im, s5_c_re, s5_c_im, s5_d, s5_w_glu, s5_b_glu, w_branch, w_merge_gate, b_merge_gate, w_out, final_norm_g):
    stacked = (norm_g, w_in, lru_conv_w, lru_conv_b, lru_w_a, lru_b_a, lru_w_x, lru_b_x, lru_lambda,
               gla_w_up, gla_b_up, gla_norm_g, dn_conv_w, dn_a_log, dn_dt_bias, dn_norm_g,
               s5_lambda_re, s5_lambda_im, s5_log_dt, s5_b_re, s5_b_im, s5_c_re, s5_c_im, s5_d,
               s5_w_glu, s5_b_glu, w_branch, w_merge_gate, b_merge_gate, w_out)
    depth = norm_g.shape[0]
    seq_len, d = x_prompt.shape[1:]
    assert x_sample.shape[1] == seq_len
    groups = (x_prompt, x_sample)
    parts = [a.reshape(-1, d) for a in groups]
    group_rows = [a.shape[0] for a in parts]
    for layer in range(depth):
        p = {name: arr[layer] for name, arr in zip(_LAYER_PARAM_NAMES, stacked)}
        parts = _layer(parts, group_rows, seq_len, p, final_norm_g, layer == depth - 1)
    return tuple(y.reshape(a.shape) for y, a in zip(parts, groups))
```
